```python
import math
import jax, jax.numpy as jnp
from jax import lax
import numpy as np

D_MODEL = 1024
BATCH = 2
SEQ = 8192
DEPTH = 2

GRID_W = 64
CTX_LEN = 256
HEAD_DIM = 64
N_HEADS_A = 8
N_KV_A = 2
N_HEADS_B = 8
N_HEADS_C = 8
N_KV_C = 2
MIX_WIDTH = HEAD_DIM * (N_HEADS_A + N_HEADS_B + N_HEADS_C)
Q_BLOCK = 128
NB_ROWS = 8
NB_COLS = 16
WINDOW = 128
ROPE_THETA = 10000.0
EPS = 1e-6
IN_SIZES = (
    N_HEADS_A * HEAD_DIM, N_KV_A * HEAD_DIM, N_KV_A * HEAD_DIM,
    N_HEADS_B * HEAD_DIM, N_HEADS_B * HEAD_DIM, N_HEADS_B * HEAD_DIM,
    N_HEADS_C * HEAD_DIM, N_KV_C * HEAD_DIM, N_KV_C * HEAD_DIM,
    MIX_WIDTH,
)
IN_WIDTH = sum(IN_SIZES)

kernel_name = "hybrid_parallel_groups_flow_backbone"


def rmsnorm(x, w):
    xf = x.astype(jnp.float32)
    y = xf * lax.rsqrt(jnp.mean(xf * xf, axis=-1, keepdims=True) + EPS)
    return y.astype(x.dtype) * w


def softmax_f32(s, dtype):
    return jax.nn.softmax(s.astype(jnp.float32), axis=-1).astype(dtype)


def heads(z, h):
    return z.reshape(z.shape[:-1] + (h, HEAD_DIM))


def split_cols(z):
    outs, off = [], 0
    for sz in IN_SIZES:
        outs.append(z[..., off:off + sz])
        off += sz
    return outs


def axial_rope(n, dtype):
    t = jnp.arange(n, dtype=jnp.int32)
    rows = (t // GRID_W).astype(jnp.float32)
    cols = (t % GRID_W).astype(jnp.float32)
    n_freq = HEAD_DIM // 4
    freq = ROPE_THETA ** (-jnp.arange(n_freq, dtype=jnp.float32) / n_freq)
    ang = jnp.concatenate([rows[:, None] * freq, cols[:, None] * freq], axis=-1)
    return jnp.cos(ang).astype(dtype), jnp.sin(ang).astype(dtype)


def apply_rope(x, cos, sin):
    x1, x2 = jnp.split(x, 2, axis=-1)
    c = cos[None, :, None, :]
    s = sin[None, :, None, :]
    return jnp.concatenate([x1 * c - x2 * s, x1 * s + x2 * c], axis=-1)


def global_gqa(q, k, v, qc, kc, vc, cos, sin, qn, kn, need_ctx):
    B, N = q.shape[0], q.shape[1]
    C = qc.shape[1]
    G = N_HEADS_A // N_KV_A
    scale = HEAD_DIM ** -0.5
    q = apply_rope(rmsnorm(heads(q, N_HEADS_A), qn), cos, sin) * scale
    k = apply_rope(rmsnorm(heads(k, N_KV_A), kn), cos, sin)
    v = heads(v, N_KV_A)
    qc = rmsnorm(heads(qc, N_HEADS_A), qn) * scale
    kc = rmsnorm(heads(kc, N_KV_A), kn)
    vc = heads(vc, N_KV_A)
    k_all = jnp.concatenate([k, kc], axis=1)
    v_all = jnp.concatenate([v, vc], axis=1)
    nblk = N // Q_BLOCK
    qb = q.reshape(B, nblk, Q_BLOCK, N_KV_A, G, HEAD_DIM).transpose(1, 0, 2, 3, 4, 5)

    def block(qi):
        s = jnp.einsum('bqhgd,bkhd->bhgqk', qi, k_all)
        p = softmax_f32(s, v_all.dtype)
        return jnp.einsum('bhgqk,bkhd->bqhgd', p, v_all)

    y = lax.map(block, qb).transpose(1, 0, 2, 3, 4, 5).reshape(B, N, N_HEADS_A * HEAD_DIM)
    yc = None
    if need_ctx:
        qcg = qc.reshape(B, C, N_KV_A, G, HEAD_DIM)
        p = softmax_f32(jnp.einsum('bqhgd,bkhd->bhgqk', qcg, kc), vc.dtype)
        yc = jnp.einsum('bhgqk,bkhd->bqhgd', p, vc).reshape(B, C, N_HEADS_A * HEAD_DIM)
    return y, yc


def neighbourhood_attn(q, k, v, qc, kc, vc, rpb, need_ctx):
    B, N = q.shape[0], q.shape[1]
    C = qc.shape[1]
    H = N_HEADS_B
    scale = HEAD_DIM ** -0.5
    rows = N // GRID_W
    kh = min(NB_ROWS, rows)
    kw = NB_COLS
    t = jnp.arange(N, dtype=jnp.int32)
    r = t // GRID_W
    col = t % GRID_W
    rs = jnp.clip(r - kh // 2, 0, rows - kh)
    cs = jnp.clip(col - kw // 2, 0, GRID_W - kw)
    key_r = rs[:, None, None] + jnp.arange(kh, dtype=jnp.int32)[None, :, None]
    key_c = cs[:, None, None] + jnp.arange(kw, dtype=jnp.int32)[None, None, :]
    idx = (key_r * GRID_W + key_c).reshape(N, kh * kw)
    rel = ((key_r - r[:, None, None] + NB_ROWS - 1) * (2 * NB_COLS - 1)
           + (key_c - col[:, None, None] + NB_COLS - 1)).reshape(N, kh * kw)
    rpb_flat = rpb.reshape(H, -1)
    q = heads(q, H) * scale
    k = heads(k, H)
    v = heads(v, H)
    qc = heads(qc, H) * scale
    kc = heads(kc, H)
    vc = heads(vc, H)
    nblk = N // Q_BLOCK
    kn = kh * kw
    xs = (q.reshape(B, nblk, Q_BLOCK, H, HEAD_DIM).transpose(1, 0, 2, 3, 4),
          idx.reshape(nblk, Q_BLOCK, kn), rel.reshape(nblk, Q_BLOCK, kn))

    def block(args):
        qi, ii, ri = args
        kg = jnp.take(k, ii, axis=1)
        vg = jnp.take(v, ii, axis=1)
        s_nb = jnp.einsum('bqhd,bqkhd->bhqk', qi, kg) + jnp.take(rpb_flat, ri, axis=1)[None]
        s_ctx = jnp.einsum('bqhd,bchd->bhqc', qi, kc)
        p = softmax_f32(jnp.concatenate([s_nb, s_ctx], axis=-1), v.dtype)
        return (jnp.einsum('bhqk,bqkhd->bqhd', p[..., :kn], vg)
                + jnp.einsum('bhqc,bchd->bqhd', p[..., kn:], vc))

    y = lax.map(block, xs).transpose(1, 0, 2, 3, 4).reshape(B, N, H * HEAD_DIM)
    yc = None
    if need_ctx:
        p = softmax_f32(jnp.einsum('bqhd,bchd->bhqc', qc, kc), vc.dtype)
        yc = jnp.einsum('bhqc,bchd->bqhd', p, vc).reshape(B, C, H * HEAD_DIM)
    return y, yc


def window_gqa(q, k, v, qc, kc, vc, cos, sin, sink, need_ctx):
    B, N = q.shape[0], q.shape[1]
    C = qc.shape[1]
    G = N_HEADS_C // N_KV_C
    scale = HEAD_DIM ** -0.5
    q = (apply_rope(heads(q, N_HEADS_C), cos, sin) * scale).reshape(B, N, N_KV_C, G, HEAD_DIM)
    k = apply_rope(heads(k, N_KV_C), cos, sin)
    v = heads(v, N_KV_C)
    qc = (heads(qc, N_HEADS_C) * scale).reshape(B, C, N_KV_C, G, HEAD_DIM)
    kc = heads(kc, N_KV_C)
    vc = heads(vc, N_KV_C)
    pad = ((0, 0), (WINDOW, WINDOW), (0, 0), (0, 0))
    kp = jnp.pad(k, pad)
    vp = jnp.pad(v, pad)
    span = Q_BLOCK + 2 * WINDOW
    sink_l = sink.astype(jnp.float32).reshape(N_KV_C, G)
    neg = jnp.finfo(jnp.float32).min
    nblk = N // Q_BLOCK
    qs = q.reshape(B, nblk, Q_BLOCK, N_KV_C, G, HEAD_DIM).transpose(1, 0, 2, 3, 4, 5)

    def block(args):
        i, qi = args
        start = i * Q_BLOCK
        kb = lax.dynamic_slice_in_dim(kp, start, span, axis=1)
        vb = lax.dynamic_slice_in_dim(vp, start, span, axis=1)
        qpos = start + jnp.arange(Q_BLOCK, dtype=jnp.int32)
        kpos = start - WINDOW + jnp.arange(span, dtype=jnp.int32)
        valid = (jnp.abs(qpos[:, None] - kpos[None, :]) <= WINDOW) & (kpos >= 0)[None, :] & (kpos < N)[None, :]
        s = jnp.where(valid, jnp.einsum('bqhgd,bkhd->bhgqk', qi, kb).astype(jnp.float32), neg)
        s_ctx = jnp.einsum('bqhgd,bkhd->bhgqk', qi, kc).astype(jnp.float32)
        s_sink = jnp.broadcast_to(sink_l[None, :, :, None, None], s.shape[:-1] + (1,))
        p = softmax_f32(jnp.concatenate([s, s_ctx, s_sink], axis=-1), v.dtype)
        return (jnp.einsum('bhgqk,bkhd->bqhgd', p[..., :span], vb)
                + jnp.einsum('bhgqk,bkhd->bqhgd', p[..., span:span + C], vc))

    y = lax.map(block, (jnp.arange(nblk, dtype=jnp.int32), qs))
    y = y.transpose(1, 0, 2, 3, 4, 5).reshape(B, N, N_HEADS_C * HEAD_DIM)
    yc = None
    if need_ctx:
        s = jnp.einsum('bqhgd,bkhd->bhgqk', qc, kc).astype(jnp.float32)
        s_sink = jnp.broadcast_to(sink_l[None, :, :, None, None], s.shape[:-1] + (1,))
        p = softmax_f32(jnp.concatenate([s, s_sink], axis=-1), vc.dtype)
        yc = jnp.einsum('bhgqk,bkhd->bqhgd', p[..., :C], vc).reshape(B, C, N_HEADS_C * HEAD_DIM)
    return y, yc


def layer(x, cx, c_silu, cctx_silu, norm_w, ada_w, ada_b, w_in, w_out, qn, kn, rpb, sink, cos, sin, need_ctx):
    mod = c_silu @ ada_w + ada_b
    mod_c = cctx_silu @ ada_w + ada_b
    shift, scale, gate = jnp.split(mod, 3, axis=-1)
    shift_c, scale_c, gate_c = jnp.split(mod_c, 3, axis=-1)
    hx = rmsnorm(x, norm_w) * (1 + scale[:, None, :]) + shift[:, None, :]
    hc = rmsnorm(cx, norm_w) * (1 + scale_c) + shift_c
    px = split_cols(hx @ w_in)
    pc = split_cols(hc @ w_in)
    ya, yca = global_gqa(px[0], px[1], px[2], pc[0], pc[1], pc[2], cos, sin, qn, kn, need_ctx)
    yb, ycb = neighbourhood_attn(px[3], px[4], px[5], pc[3], pc[4], pc[5], rpb, need_ctx)
    yc_, ycc = window_gqa(px[6], px[7], px[8], pc[6], pc[7], pc[8], cos, sin, sink, need_ctx)
    ux = jnp.concatenate([ya, yb, yc_], axis=-1) * jax.nn.silu(px[9])
    x = x + gate[:, None, :] * (ux @ w_out)
    if need_ctx:
        uc = jnp.concatenate([yca, ycb, ycc], axis=-1) * jax.nn.silu(pc[9])
        cx = cx + gate_c * (uc @ w_out)
    return x, cx


def setup_inputs(seed: int = 0) -> dict:
    key = jax.random.key(seed)
    ks = jax.random.split(key, 14)
    f32 = jnp.float32
    n_rel = (2 * NB_ROWS - 1, 2 * NB_COLS - 1)
    return {
        "x": jax.random.normal(ks[0], (BATCH, SEQ, D_MODEL), f32),
        "c": jax.random.normal(ks[1], (BATCH, D_MODEL), f32),
        "ctx": jax.random.normal(ks[2], (BATCH, CTX_LEN, D_MODEL), f32),
        "c_ctx": jax.random.normal(ks[3], (D_MODEL,), f32),
        "norm_w": 1.0 + 0.05 * jax.random.normal(ks[4], (DEPTH, D_MODEL), f32),
        "ada_w": jax.random.normal(ks[5], (DEPTH, D_MODEL, 3 * D_MODEL), f32) * D_MODEL ** -0.5,
        "ada_b": 0.02 * jax.random.normal(ks[6], (DEPTH, 3 * D_MODEL), f32),
        "w_in": jax.random.normal(ks[7], (DEPTH, D_MODEL, IN_WIDTH), f32) * D_MODEL ** -0.5,
        "w_out": jax.random.normal(ks[8], (DEPTH, MIX_WIDTH, D_MODEL), f32) * MIX_WIDTH ** -0.5,
        "q_norm_a": 1.0 + 0.05 * jax.random.normal(ks[9], (DEPTH, HEAD_DIM), f32),
        "k_norm_a": 1.0 + 0.05 * jax.random.normal(ks[10], (DEPTH, HEAD_DIM), f32),
        "rpb_b": 0.1 * jax.random.normal(ks[11], (DEPTH, N_HEADS_B) + n_rel, f32),
        "sink_c": 0.5 * jax.random.normal(ks[12], (DEPTH, N_HEADS_C), f32),
        "final_norm_w": 1.0 + 0.05 * jax.random.normal(ks[13], (D_MODEL,), f32),
    }


def reference(x, c, ctx, c_ctx, norm_w, ada_w, ada_b, w_in, w_out, q_norm_a, k_norm_a, rpb_b, sink_c, final_norm_w):
    n_tok = x.shape[1]
    cos, sin = axial_rope(n_tok, x.dtype)
    c_silu = jax.nn.silu(c)
    cctx_silu = jax.nn.silu(c_ctx)
    cx = ctx
    for l in range(DEPTH):
        x, cx = layer(x, cx, c_silu, cctx_silu, norm_w[l], ada_w[l], ada_b[l], w_in[l], w_out[l],
                      q_norm_a[l], k_norm_a[l], rpb_b[l], sink_c[l], cos, sin, l < DEPTH - 1)
    return rmsnorm(x, final_norm_w)
```

```python
import functools

import jax
import jax.numpy as jnp
from jax import lax
from jax.experimental import pallas as pl
from jax.experimental.pallas import tpu as pltpu

D_MODEL = 1024
SEQ = 8192
DEPTH = 2
GRID_W = 64
GRID_H = SEQ // GRID_W
CTX_LEN = 256
HEAD_DIM = 64
N_HEADS = 8
N_KV = 2
GROUP = N_HEADS // N_KV
Q_W = N_HEADS * HEAD_DIM
KV_W = N_KV * HEAD_DIM
MIX_WIDTH = 3 * Q_W
IN_WIDTH = 3 * Q_W + 2 * KV_W + 2 * Q_W + 2 * KV_W + MIX_WIDTH
NB_ROWS = 8
NB_COLS = 16
WINDOW = 128
ROPE_THETA = 10000.0
EPS = 1e-6
SCALE = HEAD_DIM ** -0.5
NEG = -1e30

LANES = 128
VMEM_LIMIT = 56 * 1024 * 1024

OFF_AQ, OFF_AK, OFF_AV = 0, 512, 640
OFF_BQ, OFF_BK, OFF_BV = 768, 1280, 1792
OFF_CQ, OFF_CK, OFF_CV = 2304, 2816, 2944
OFF_G = 3072

NB_TQ = 2 * GRID_W
NB_WIN_ROWS = 10
NB_WIN = NB_WIN_ROWS * GRID_W
NB_CASE_R0 = (0, 2, 4, GRID_H - 4, GRID_H - 2)
N_REL = (2 * NB_ROWS - 1) * (2 * NB_COLS - 1)

F32 = jnp.float32
BF16 = jnp.bfloat16


def _cparams(sem):
    return pltpu.CompilerParams(dimension_semantics=sem, vmem_limit_bytes=VMEM_LIMIT)


def _dot(a, b):
    return jnp.dot(a, b, preferred_element_type=F32)


def _dot_nt(a, b):
    return lax.dot_general(a, b, (((1,), (1,)), ((), ())), preferred_element_type=F32)


def _silu(x):
    return x * (1.0 / (1.0 + jnp.exp(-x)))


def _mod_kernel(c_ref, w_ref, b_ref, o_ref):
    cs = _silu(c_ref[...])
    o_ref[0] = jnp.dot(cs, w_ref[0], preferred_element_type=F32,
                       precision=lax.Precision.HIGHEST) + b_ref[0]


def _modulation(cvecs, ada_w, ada_b):
    tn = 1024
    return pl.pallas_call(
        _mod_kernel,
        grid=(DEPTH, 3 * D_MODEL // tn),
        in_specs=[
            pl.BlockSpec((8, D_MODEL), lambda l, j: (0, 0)),
            pl.BlockSpec((1, D_MODEL, tn), lambda l, j: (l, 0, j)),
            pl.BlockSpec((1, 1, tn), lambda l, j: (l, 0, j)),
        ],
        out_specs=pl.BlockSpec((1, 8, tn), lambda l, j: (l, 0, j)),
        out_shape=jax.ShapeDtypeStruct((DEPTH, 8, 3 * D_MODEL), F32),
        compiler_params=_cparams(("arbitrary", "arbitrary")),
        name="adaln_mod",
    )(cvecs, ada_w, ada_b.reshape(DEPTH, 1, 3 * D_MODEL))


def _head_rms(z, wn):
    w = z.shape[1]
    r = lax.broadcasted_iota(jnp.int32, (w, w), 0) // HEAD_DIM
    c = lax.broadcasted_iota(jnp.int32, (w, w), 1) // HEAD_DIM
    ones = (r == c).astype(BF16)
    z2 = z * z
    hi = z2.astype(BF16)
    lo = (z2 - hi.astype(F32)).astype(BF16)
    ss = _dot(hi, ones) + _dot(lo, ones)
    return (z * lax.rsqrt(ss * (1.0 / HEAD_DIM) + EPS)) * wn


def _rope(z, cos_t, sin_t):
    w = z.shape[1]
    reps = w // LANES
    if reps > 1:
        cos_t = jnp.concatenate([cos_t] * reps, axis=1)
        sin_t = jnp.concatenate([sin_t] * reps, axis=1)
    lane = lax.broadcasted_iota(jnp.int32, z.shape, 1)
    partner = jnp.where((lane & (HEAD_DIM // 2)) == 0,
                        pltpu.roll(z, w - HEAD_DIM // 2, 1), pltpu.roll(z, HEAD_DIM // 2, 1))
    return z * cos_t + partner * sin_t


def _in_proj_kernel(*refs, rope):
    if rope:
        (x_ref, mod_ref, nw_ref, w_ref, qn_ref, kn_ref, cos_ref, sin_ref,
         qa_ref, ka_ref, va_ref, qb_ref, kb_ref, vb_ref, qc_ref, kc_ref, vc_ref, sg_ref) = refs
        cos_t, sin_t = cos_ref[...], sin_ref[...]
        rot = lambda z: _rope(z, cos_t, sin_t)
    else:
        (x_ref, mod_ref, nw_ref, w_ref, qn_ref, kn_ref,
         qa_ref, ka_ref, va_ref, qb_ref, kb_ref, vb_ref, qc_ref, kc_ref, vc_ref, sg_ref) = refs
        rot = lambda z: z
    x = x_ref[0]
    y = (x * lax.rsqrt(jnp.mean(x * x, axis=-1, keepdims=True) + EPS)) * nw_ref[...]
    h = (y * (1.0 + mod_ref[0, 1:2, :]) + mod_ref[0, 0:1, :]).astype(BF16)

    def proj(off, width):
        return _dot(h, w_ref[:, off:off + width])

    qa_ref[0] = (rot(_head_rms(proj(OFF_AQ, Q_W), qn_ref[...])) * SCALE).astype(BF16)
    ka_ref[0] = rot(_head_rms(proj(OFF_AK, KV_W), kn_ref[...])).astype(BF16)
    va_ref[0] = proj(OFF_AV, KV_W).astype(BF16)
    qb_ref[0] = (proj(OFF_BQ, Q_W) * SCALE).astype(BF16)
    kb_ref[0] = proj(OFF_BK, Q_W).astype(BF16)
    vb_ref[0] = proj(OFF_BV, Q_W).astype(BF16)
    qc_ref[0] = (rot(proj(OFF_CQ, Q_W)) * SCALE).astype(BF16)
    kc_ref[0] = rot(proj(OFF_CK, KV_W)).astype(BF16)
    vc_ref[0] = proj(OFF_CV, KV_W).astype(BF16)
    sg_ref[0] = _silu(proj(OFF_G, MIX_WIDTH)).astype(BF16)


def _in_proj(x, mod, nw, w_bf, qn_t, kn_t, rope_tabs, tm):
    b, n, _ = x.shape
    rope = rope_tabs is not None
    mod_idx = (lambda bi, i: (bi, 0, 0)) if mod.shape[0] > 1 else (lambda bi, i: (0, 0, 0))
    const2 = lambda bi, i: (0, 0)
    in_specs = [
        pl.BlockSpec((1, tm, D_MODEL), lambda bi, i: (bi, i, 0)),
        pl.BlockSpec((1, 3, D_MODEL), mod_idx),
        pl.BlockSpec((1, D_MODEL), const2),
        pl.BlockSpec((D_MODEL, IN_WIDTH), const2, pipeline_mode=pl.Buffered(1)),
        pl.BlockSpec((1, Q_W), const2),
        pl.BlockSpec((1, KV_W), const2),
    ]
    args = [x, mod, nw, w_bf, qn_t, kn_t]
    if rope:
        in_specs += [pl.BlockSpec((tm, LANES), lambda bi, i: (i, 0))] * 2
        args += list(rope_tabs)
    widths = (Q_W, KV_W, KV_W, Q_W, Q_W, Q_W, Q_W, KV_W, KV_W, MIX_WIDTH)
    return pl.pallas_call(
        functools.partial(_in_proj_kernel, rope=rope),
        grid=(b, n // tm),
        in_specs=in_specs,
        out_specs=[pl.BlockSpec((1, tm, w), lambda bi, i: (bi, i, 0)) for w in widths],
        out_shape=[jax.ShapeDtypeStruct((b, n, w), BF16) for w in widths],
        compiler_params=_cparams(("parallel", "parallel")),
        name="in_proj_rope" if rope else "in_proj_ctx",
    )(*args)


def _stack_group_q(q, g):
    t = q.shape[0]
    lane = lax.broadcasted_iota(jnp.int32, (t, LANES), 1)
    keep = (lane // HEAD_DIM) == g
    parts = []
    for j in range(GROUP):
        h = GROUP * g + j
        chunk = q[:, (h // 2) * LANES:(h // 2 + 1) * LANES].astype(F32)
        if h % 2 != g:
            chunk = pltpu.roll(chunk, HEAD_DIM, 1)
        parts.append(jnp.where(keep, chunk, 0.0).astype(BF16))
    return jnp.concatenate(parts, axis=0)


def _unstack_group_out(o, g, t):
    lane = lax.broadcasted_iota(jnp.int32, (t, LANES), 1)
    left = lane < HEAD_DIM
    chunks = []
    for cl in range(2):
        even = o[(2 * cl) * t:(2 * cl + 1) * t]
        odd = o[(2 * cl + 1) * t:(2 * cl + 2) * t]
        if g == 0:
            odd = pltpu.roll(odd, HEAD_DIM, 1)
        else:
            even = pltpu.roll(even, HEAD_DIM, 1)
        chunks.append(jnp.where(left, even, odd))
    return chunks


def _softmax_pv(s_list, v_list, extra_logit=None):
    m = s_list[0].max(axis=1, keepdims=True)
    for s in s_list[1:]:
        m = jnp.maximum(m, s.max(axis=1, keepdims=True))
    if extra_logit is not None:
        m = jnp.maximum(m, extra_logit)
    l = None
    o = None
    for s, v in zip(s_list, v_list):
        p = jnp.exp(s - m)
        ps = p.sum(axis=1, keepdims=True)
        po = _dot(p.astype(BF16), v)
        l = ps if l is None else l + ps
        o = po if o is None else o + po
    if extra_logit is not None:
        l = l + jnp.exp(extra_logit - m)
    return o / l


def _sink_column(sink_ref, g, t):
    row = lax.broadcasted_iota(jnp.int32, (GROUP * t, 1), 0) // t
    col = jnp.full((GROUP * t, 1), sink_ref[GROUP * g], F32)
    for j in range(1, GROUP):
        col = jnp.where(row == j, sink_ref[GROUP * g + j], col)
    return col


def _attn_a_kernel(q_ref, k_ref, v_ref, kc_ref, vc_ref, o_ref, qs_ref, m_ref, l_ref, acc_ref,
                   *, tq, tk):
    def step(k, v):
        s = _dot_nt(qs_ref[...], k)
        m_prev = m_ref[...]
        m_new = jnp.maximum(m_prev, s.max(axis=1, keepdims=True))
        alpha = jnp.exp(m_prev - m_new)
        p = jnp.exp(s - m_new)
        l_ref[...] = alpha * l_ref[...] + p.sum(axis=1, keepdims=True)
        acc_ref[...] = alpha * acc_ref[...] + _dot(p.astype(BF16), v)
        m_ref[...] = m_new

    for g in range(N_KV):
        qs_ref[...] = _stack_group_q(q_ref[0], g)
        m_ref[...] = jnp.full(m_ref.shape, NEG, F32)
        l_ref[...] = jnp.zeros(l_ref.shape, F32)
        acc_ref[...] = jnp.zeros(acc_ref.shape, F32)

        def body(j, carry):
            off = pl.multiple_of(j * tk, tk)
            step(k_ref[0, pl.ds(off, tk), :], v_ref[0, pl.ds(off, tk), :])
            return carry

        lax.fori_loop(0, SEQ // tk, body, 0)
        step(kc_ref[0], vc_ref[0])
        chunks = _unstack_group_out(acc_ref[...] / l_ref[...], g, tq)
        for cl in range(2):
            c = 2 * g + cl
            o_ref[0, :, c * LANES:(c + 1) * LANES] = chunks[cl].astype(BF16)


def _attn_a(q, k, v, kc, vc, tq=256, tk=512):
    b = q.shape[0]
    res = lambda n: pl.BlockSpec((1, n, KV_W), lambda bi, i: (bi, 0, 0))
    return pl.pallas_call(
        functools.partial(_attn_a_kernel, tq=tq, tk=tk),
        grid=(b, SEQ // tq),
        in_specs=[pl.BlockSpec((1, tq, Q_W), lambda bi, i: (bi, i, 0)),
                  res(SEQ), res(SEQ), res(CTX_LEN), res(CTX_LEN)],
        out_specs=pl.BlockSpec((1, tq, Q_W), lambda bi, i: (bi, i, 0)),
        out_shape=jax.ShapeDtypeStruct((b, SEQ, Q_W), BF16),
        scratch_shapes=[pltpu.VMEM((GROUP * tq, LANES), BF16),
                        pltpu.VMEM((GROUP * tq, 1), F32),
                        pltpu.VMEM((GROUP * tq, 1), F32),
                        pltpu.VMEM((GROUP * tq, LANES), F32)],
        compiler_params=_cparams(("parallel", "parallel")),
        name="attn_global",
    )(q, k, v, kc, vc)


def _attn_c_kernel(sink_ref, q_ref, k_ref, v_ref, kc_ref, vc_ref, o_ref, *, tq):
    win = tq + 2 * WINDOW
    i = pl.program_id(1)
    ws = pl.multiple_of(jnp.clip(i * tq - WINDOW, 0, SEQ - win), WINDOW)
    kw = k_ref[0, pl.ds(ws, win), :]
    vw = v_ref[0, pl.ds(ws, win), :]
    qpos = i * tq + lax.broadcasted_iota(jnp.int32, (GROUP * tq, win), 0) % tq
    kpos = ws + lax.broadcasted_iota(jnp.int32, (GROUP * tq, win), 1)
    valid = jnp.abs(qpos - kpos) <= WINDOW
    for g in range(N_KV):
        qs = _stack_group_q(q_ref[0], g)
        s_win = jnp.where(valid, _dot_nt(qs, kw), NEG)
        s_ctx = _dot_nt(qs, kc_ref[0])
        o = _softmax_pv([s_win, s_ctx], [vw, vc_ref[0]], _sink_column(sink_ref, g, tq))
        chunks = _unstack_group_out(o, g, tq)
        for cl in range(2):
            c = 2 * g + cl
            o_ref[0, :, c * LANES:(c + 1) * LANES] = chunks[cl].astype(BF16)


def _attn_c(sink, q, k, v, kc, vc, tq=256):
    b = q.shape[0]
    res = lambda n: pl.BlockSpec((1, n, KV_W), lambda bi, i: (bi, 0, 0))
    return pl.pallas_call(
        functools.partial(_attn_c_kernel, tq=tq),
        grid=(b, SEQ // tq),
        in_specs=[pl.BlockSpec(memory_space=pltpu.SMEM),
                  pl.BlockSpec((1, tq, Q_W), lambda bi, i: (bi, i, 0)),
                  res(SEQ), res(SEQ), res(CTX_LEN), res(CTX_LEN)],
        out_specs=pl.BlockSpec((1, tq, Q_W), lambda bi, i: (bi, i, 0)),
        out_shape=jax.ShapeDtypeStruct((b, SEQ, Q_W), BF16),
        compiler_params=_cparams(("parallel", "parallel")),
        name="attn_window",
    )(sink, q, k, v, kc, vc)


def _nb_window_row(r0):
    return min(max(r0 - NB_ROWS // 2, 0), GRID_H - NB_WIN_ROWS)


def _nb_bias_kernel(rpb_ref, o_ref, tt_ref):
    base = (pl.program_id(0) * N_HEADS + pl.program_id(1)) * N_REL
    cq = lax.broadcasted_iota(jnp.int32, (GRID_W, LANES), 0)
    lane = lax.broadcasted_iota(jnp.int32, (GRID_W, LANES), 1)
    ck = lane & (GRID_W - 1)
    d = ck - cq + (NB_COLS - 1)
    cs = jnp.clip(cq - NB_COLS // 2, 0, GRID_W - NB_COLS)
    col_ok = (ck >= cs) & (ck < cs + NB_COLS)
    n_dc = 2 * NB_COLS - 1
    for a in range(2 * NB_ROWS - 1):
        t = jnp.full((GRID_W, LANES), NEG, F32)
        for dd in range(n_dc):
            t = jnp.where(d == dd, rpb_ref[base + a * n_dc + dd], t)
        tt_ref[a] = jnp.where(col_ok, t, NEG)
    neg_tile = jnp.full((GRID_W, LANES), NEG, F32)
    left = lane < GRID_W
    for case, r0 in enumerate(NB_CASE_R0):
        w0 = _nb_window_row(r0)
        for qr in range(2):
            rq = r0 + qr
            rs = min(max(rq - NB_ROWS // 2, 0), GRID_H - NB_ROWS)
            for j in range(NB_WIN_ROWS // 2):
                halves = []
                for kr in (2 * j, 2 * j + 1):
                    rk = w0 + kr
                    halves.append(tt_ref[rk - rq + NB_ROWS - 1] if rs <= rk < rs + NB_ROWS else neg_tile)
                o_ref[0, case, 0, qr * GRID_W:(qr + 1) * GRID_W, j * LANES:(j + 1) * LANES] = (
                    jnp.where(left, halves[0], halves[1]))


def _nb_bias(rpb):
    n_case = len(NB_CASE_R0)
    return pl.pallas_call(
        _nb_bias_kernel,
        grid=(DEPTH, N_HEADS),
        in_specs=[pl.BlockSpec(memory_space=pltpu.SMEM)],
        out_specs=pl.BlockSpec((1, n_case, 1, NB_TQ, NB_WIN), lambda l, h: (l, 0, h, 0, 0)),
        out_shape=jax.ShapeDtypeStruct((DEPTH, n_case, N_HEADS, NB_TQ, NB_WIN), F32),
        scratch_shapes=[pltpu.VMEM((2 * NB_ROWS - 1, GRID_W, LANES), F32)],
        compiler_params=_cparams(("arbitrary", "arbitrary")),
        name="nb_bias",
    )(rpb.reshape(-1))


def _pair_attention(qp, s_extra_fn, k_list, v_list):
    t = qp.shape[0]
    lane = lax.broadcasted_iota(jnp.int32, (t, LANES), 1)
    qf = qp.astype(F32)
    outs = []
    for h in range(2):
        qpad = jnp.where((lane // HEAD_DIM) == h, qf, 0.0).astype(BF16)
        s_list = [_dot_nt(qpad, k) for k in k_list]
        s_list = s_extra_fn(h, s_list)
        outs.append(_softmax_pv(s_list, v_list))
    return jnp.where(lane < HEAD_DIM, outs[0], outs[1])


def _attn_b_kernel(q_ref, k_ref, v_ref, kc_ref, vc_ref, bias_ref, o_ref):
    i = pl.program_id(1)
    w0 = jnp.clip(2 * i - NB_ROWS // 2, 0, GRID_H - NB_WIN_ROWS)
    off = pl.multiple_of(w0 * GRID_W, LANES)
    for c in range(N_HEADS // 2):
        cols = slice(c * LANES, (c + 1) * LANES)
        add_bias = lambda h, s_list, c=c: [s_list[0] + bias_ref[0, 2 * c + h], s_list[1]]
        o = _pair_attention(
            q_ref[0, :, cols], add_bias,
            [k_ref[0, pl.ds(off, NB_WIN), cols], kc_ref[0, :, cols]],
            [v_ref[0, pl.ds(off, NB_WIN), cols], vc_ref[0, :, cols]])
        o_ref[0, :, cols] = o.astype(BF16)


def _attn_b(q, k, v, kc, vc, bias):
    b = q.shape[0]
    n_tiles = SEQ // NB_TQ
    res = lambda n: pl.BlockSpec((1, n, Q_W), lambda bi, i: (bi, 0, 0), pipeline_mode=pl.Buffered(1))

    def bias_idx(bi, i):
        return (jnp.minimum(i, 2) + jnp.maximum(i - (n_tiles - 3), 0), 0, 0, 0)

    return pl.pallas_call(
        _attn_b_kernel,
        grid=(b, n_tiles),
        in_specs=[pl.BlockSpec((1, NB_TQ, Q_W), lambda bi, i: (bi, i, 0)),
                  res(SEQ), res(SEQ), res(CTX_LEN), res(CTX_LEN),
                  pl.BlockSpec((1, N_HEADS, NB_TQ, NB_WIN), bias_idx)],
        out_specs=pl.BlockSpec((1, NB_TQ, Q_W), lambda bi, i: (bi, i, 0)),
        out_shape=jax.ShapeDtypeStruct((b, SEQ, Q_W), BF16),
        compiler_params=_cparams(("parallel", "arbitrary")),
        name="attn_neighbourhood",
    )(q, k, v, kc, vc, bias)


def _attn_ctx_kernel(sink_ref, qa_ref, ka_ref, va_ref, qb_ref, kb_ref, vb_ref, qc_ref, kc_ref, vc_ref,
                     oa_ref, ob_ref, oc_ref):
    t = CTX_LEN
    for g in range(N_KV):
        for q_ref, k_ref, v_ref, o_ref, sink in ((qa_ref, ka_ref, va_ref, oa_ref, False),
                                                 (qc_ref, kc_ref, vc_ref, oc_ref, True)):
            qs = _stack_group_q(q_ref[0], g)
            extra = _sink_column(sink_ref, g, t) if sink else None
            o = _softmax_pv([_dot_nt(qs, k_ref[0])], [v_ref[0]], extra)
            chunks = _unstack_group_out(o, g, t)
            for cl in range(2):
                c = 2 * g + cl
                o_ref[0, :, c * LANES:(c + 1) * LANES] = chunks[cl].astype(BF16)
    for c in range(N_HEADS // 2):
        cols = slice(c * LANES, (c + 1) * LANES)
        o = _pair_attention(qb_ref[0, :, cols], lambda h, s_list: s_list,
                            [kb_ref[0, :, cols]], [vb_ref[0, :, cols]])
        ob_ref[0, :, cols] = o.astype(BF16)


def _attn_ctx(sink, qa, ka, va, qb, kb, vb, qc, kc, vc):
    b = qa.shape[0]
    spec = lambda w: pl.BlockSpec((1, CTX_LEN, w), lambda bi: (bi, 0, 0))
    return pl.pallas_call(
        _attn_ctx_kernel,
        grid=(b,),
        in_specs=[pl.BlockSpec(memory_space=pltpu.SMEM),
                  spec(Q_W), spec(KV_W), spec(KV_W), spec(Q_W), spec(Q_W), spec(Q_W),
                  spec(Q_W), spec(KV_W), spec(KV_W)],
        out_specs=[spec(Q_W)] * 3,
        out_shape=[jax.ShapeDtypeStruct((b, CTX_LEN, Q_W), BF16)] * 3,
        compiler_params=_cparams(("parallel",)),
        name="attn_ctx",
    )(sink, qa, ka, va, qb, kb, vb, qc, kc, vc)


def _out_proj_kernel(*refs, final):
    if final:
        ya_ref, yb_ref, yc_ref, sg_ref, x_ref, mod_ref, w_ref, fw_ref, o_ref = refs
    else:
        ya_ref, yb_ref, yc_ref, sg_ref, x_ref, mod_ref, w_ref, o_ref = refs
    acc = None
    for idx, y_ref in enumerate((ya_ref, yb_ref, yc_ref)):
        cols = slice(idx * Q_W, (idx + 1) * Q_W)
        u = (y_ref[0].astype(F32) * sg_ref[0, :, cols].astype(F32)).astype(BF16)
        part = _dot(u, w_ref[cols, :])
        acc = part if acc is None else acc + part
    xn = x_ref[0] + mod_ref[0, 2:3, :] * acc
    if final:
        xn = (xn * lax.rsqrt(jnp.mean(xn * xn, axis=-1, keepdims=True) + EPS)) * fw_ref[...]
    o_ref[0] = xn


def _out_proj(ya, yb, yc, sg, x, mod, w_bf, final_w, tm):
    b, n, _ = x.shape
    final = final_w is not None
    mod_idx = (lambda bi, i: (bi, 0, 0)) if mod.shape[0] > 1 else (lambda bi, i: (0, 0, 0))
    row = lambda w: pl.BlockSpec((1, tm, w), lambda bi, i: (bi, i, 0))
    in_specs = [row(Q_W), row(Q_W), row(Q_W), row(MIX_WIDTH), row(D_MODEL),
                pl.BlockSpec((1, 3, D_MODEL), mod_idx),
                pl.BlockSpec((MIX_WIDTH, D_MODEL), lambda bi, i: (0, 0))]
    args = [ya, yb, yc, sg, x, mod, w_bf]
    if final:
        in_specs.append(pl.BlockSpec((1, D_MODEL), lambda bi, i: (0, 0)))
        args.append(final_w)
    return pl.pallas_call(
        functools.partial(_out_proj_kernel, final=final),
        grid=(b, n // tm),
        in_specs=in_specs,
        out_specs=row(D_MODEL),
        out_shape=jax.ShapeDtypeStruct((b, n, D_MODEL), F32),
        compiler_params=_cparams(("parallel", "parallel")),
        name="out_proj_final" if final else "out_proj",
    )(*args)


def _rope_tables():
    t = jnp.arange(SEQ, dtype=jnp.int32)
    rows = (t // GRID_W).astype(F32)
    cols = (t % GRID_W).astype(F32)
    n_freq = HEAD_DIM // 4
    freq = ROPE_THETA ** (-jnp.arange(n_freq, dtype=F32) / n_freq)
    ang = jnp.concatenate([rows[:, None] * freq, cols[:, None] * freq], axis=-1)
    cos, sin = jnp.cos(ang), jnp.sin(ang)
    cos_t = jnp.tile(jnp.concatenate([cos, cos], axis=-1), (1, LANES // HEAD_DIM))
    sin_t = jnp.tile(jnp.concatenate([-sin, sin], axis=-1), (1, LANES // HEAD_DIM))
    return cos_t, sin_t


def kernel(x, c, ctx, c_ctx, norm_w, ada_w, ada_b, w_in, w_out, q_norm_a, k_norm_a, rpb_b, sink_c,
           final_norm_w):
    bsz = x.shape[0]
    cvecs = jnp.concatenate([c, c_ctx[None], jnp.zeros((8 - bsz - 1, D_MODEL), F32)], axis=0)
    mod = _modulation(cvecs, ada_w, ada_b).reshape(DEPTH, 8, 3, D_MODEL)
    bias = _nb_bias(rpb_b)
    rope_tabs = _rope_tables()
    w_in_bf = w_in.astype(BF16)
    w_out_bf = w_out.astype(BF16)
    cx = ctx
    for l in range(DEPTH):
        need_ctx = l < DEPTH - 1
        mod_x, mod_c = mod[l, :bsz], mod[l, bsz:bsz + 1]
        nw = norm_w[l][None]
        qn_t = jnp.tile(q_norm_a[l], N_HEADS)[None]
        kn_t = jnp.tile(k_norm_a[l], N_KV)[None]
        qa, ka, va, qb, kb, vb, qc, kc, vc, sg = _in_proj(
            x, mod_x, nw, w_in_bf[l], qn_t, kn_t, rope_tabs, tm=512)
        qa_c, ka_c, va_c, qb_c, kb_c, vb_c, qc_c, kc_c, vc_c, sg_c = _in_proj(
            cx, mod_c, nw, w_in_bf[l], qn_t, kn_t, None, tm=CTX_LEN)
        ya = _attn_a(qa, ka, va, ka_c, va_c)
        yb = _attn_b(qb, kb, vb, kb_c, vb_c, bias[l])
        yc = _attn_c(sink_c[l], qc, kc, vc, kc_c, vc_c)
        if need_ctx:
            ya_c, yb_c, yc_c = _attn_ctx(sink_c[l], qa_c, ka_c, va_c, qb_c, kb_c, vb_c, qc_c, kc_c, vc_c)
            cx = _out_proj(ya_c, yb_c, yc_c, sg_c, cx, mod_c, w_out_bf[l], None, tm=CTX_LEN)
        x = _out_proj(ya, yb, yc, sg, x, mod_x, w_out_bf[l],
                      None if need_ctx else final_norm_w[None], tm=512)
    return x
```

```python
import functools

import jax
import jax.numpy as jnp
from jax import lax
from jax.experimental import pallas as pl
from jax.experimental.pallas import tpu as pltpu

D_MODEL = 1024
SEQ = 8192
DEPTH = 2
GRID_W = 64
GRID_H = SEQ // GRID_W
CTX_LEN = 256
HEAD_DIM = 64
N_HEADS = 8
N_KV = 2
GROUP = N_HEADS // N_KV
Q_W = N_HEADS * HEAD_DIM
KV_W = N_KV * HEAD_DIM
MIX_WIDTH = 3 * Q_W
IN_WIDTH = 3 * Q_W + 2 * KV_W + 2 * Q_W + 2 * KV_W + MIX_WIDTH
NB_ROWS = 8
NB_COLS = 16
WINDOW = 128
ROPE_THETA = 10000.0
EPS = 1e-6
SCALE = HEAD_DIM ** -0.5
NEG = -1e30

LANES = 128
VMEM_LIMIT = 56 * 1024 * 1024

OFF_AQ, OFF_AK, OFF_AV = 0, 512, 640
OFF_BQ, OFF_BK, OFF_BV = 768, 1280, 1792
OFF_CQ, OFF_CK, OFF_CV = 2304, 2816, 2944
OFF_G = 3072

NB_TQ = 2 * GRID_W
NB_WIN_ROWS = 10
NB_WIN = NB_WIN_ROWS * GRID_W
NB_CASE_R0 = (0, 2, 4, GRID_H - 4, GRID_H - 2)
N_REL = (2 * NB_ROWS - 1) * (2 * NB_COLS - 1)

F32 = jnp.float32
BF16 = jnp.bfloat16


def _cparams(sem):
    return pltpu.CompilerParams(dimension_semantics=sem, vmem_limit_bytes=VMEM_LIMIT)


def _dot(a, b):
    return jnp.dot(a, b, preferred_element_type=F32)


def _dot_nt(a, b):
    return lax.dot_general(a, b, (((1,), (1,)), ((), ())), preferred_element_type=F32)


def _silu(x):
    return x * (1.0 / (1.0 + jnp.exp(-x)))


def _mod_kernel(c_ref, w_ref, b_ref, o_ref):
    cs = _silu(c_ref[...])
    o_ref[0] = jnp.dot(cs, w_ref[0], preferred_element_type=F32,
                       precision=lax.Precision.HIGHEST) + b_ref[0]


def _modulation(cvecs, ada_w, ada_b):
    tn = 1024
    return pl.pallas_call(
        _mod_kernel,
        grid=(DEPTH, 3 * D_MODEL // tn),
        in_specs=[
            pl.BlockSpec((8, D_MODEL), lambda l, j: (0, 0)),
            pl.BlockSpec((1, D_MODEL, tn), lambda l, j: (l, 0, j)),
            pl.BlockSpec((1, 1, tn), lambda l, j: (l, 0, j)),
        ],
        out_specs=pl.BlockSpec((1, 8, tn), lambda l, j: (l, 0, j)),
        out_shape=jax.ShapeDtypeStruct((DEPTH, 8, 3 * D_MODEL), F32),
        compiler_params=_cparams(("arbitrary", "arbitrary")),
        name="adaln_mod",
    )(cvecs, ada_w, ada_b.reshape(DEPTH, 1, 3 * D_MODEL))


def _head_rms(z, wn):
    w = z.shape[1]
    r = lax.broadcasted_iota(jnp.int32, (w, w), 0) // HEAD_DIM
    c = lax.broadcasted_iota(jnp.int32, (w, w), 1) // HEAD_DIM
    ones = (r == c).astype(BF16)
    z2 = z * z
    hi = z2.astype(BF16)
    lo = (z2 - hi.astype(F32)).astype(BF16)
    ss = _dot(hi, ones) + _dot(lo, ones)
    return (z * lax.rsqrt(ss * (1.0 / HEAD_DIM) + EPS)) * wn


def _rope(z, cos_t, sin_t):
    w = z.shape[1]
    reps = w // LANES
    if reps > 1:
        cos_t = jnp.concatenate([cos_t] * reps, axis=1)
        sin_t = jnp.concatenate([sin_t] * reps, axis=1)
    lane = lax.broadcasted_iota(jnp.int32, z.shape, 1)
    partner = jnp.where((lane & (HEAD_DIM // 2)) == 0,
                        pltpu.roll(z, w - HEAD_DIM // 2, 1), pltpu.roll(z, HEAD_DIM // 2, 1))
    return z * cos_t + partner * sin_t


def _in_proj_kernel(*refs, rope):
    if rope:
        (x_ref, mod_ref, nw_ref, w_ref, qn_ref, kn_ref, cos_ref, sin_ref,
         qa_ref, ka_ref, va_ref, qb_ref, kb_ref, vb_ref, qc_ref, kc_ref, vc_ref, sg_ref) = refs
        cos_t, sin_t = cos_ref[...], sin_ref[...]
        rot = lambda z: _rope(z, cos_t, sin_t)
    else:
        (x_ref, mod_ref, nw_ref, w_ref, qn_ref, kn_ref,
         qa_ref, ka_ref, va_ref, qb_ref, kb_ref, vb_ref, qc_ref, kc_ref, vc_ref, sg_ref) = refs
        rot = lambda z: z
    x = x_ref[0]
    y = (x * lax.rsqrt(jnp.mean(x * x, axis=-1, keepdims=True) + EPS)) * nw_ref[...]
    h = (y * (1.0 + mod_ref[0, 1:2, :]) + mod_ref[0, 0:1, :]).astype(BF16)

    def proj(off, width):
        return _dot(h, w_ref[:, off:off + width])

    qa_ref[0] = (rot(_head_rms(proj(OFF_AQ, Q_W), qn_ref[...])) * SCALE).astype(BF16)
    ka_ref[0] = rot(_head_rms(proj(OFF_AK, KV_W), kn_ref[...])).astype(BF16)
    va_ref[0] = proj(OFF_AV, KV_W).astype(BF16)
    qb_ref[0] = (proj(OFF_BQ, Q_W) * SCALE).astype(BF16)
    kb_ref[0] = proj(OFF_BK, Q_W).astype(BF16)
    vb_ref[0] = proj(OFF_BV, Q_W).astype(BF16)
    qc_ref[0] = (rot(proj(OFF_CQ, Q_W)) * SCALE).astype(BF16)
    kc_ref[0] = rot(proj(OFF_CK, KV_W)).astype(BF16)
    vc_ref[0] = proj(OFF_CV, KV_W).astype(BF16)
    sg_ref[0] = _silu(proj(OFF_G, MIX_WIDTH)).astype(BF16)


def _in_proj(x, mod, nw, w_bf, qn_t, kn_t, rope_tabs, tm):
    b, n, _ = x.shape
    rope = rope_tabs is not None
    mod_idx = (lambda bi, i: (bi, 0, 0)) if mod.shape[0] > 1 else (lambda bi, i: (0, 0, 0))
    const2 = lambda bi, i: (0, 0)
    in_specs = [
        pl.BlockSpec((1, tm, D_MODEL), lambda bi, i: (bi, i, 0)),
        pl.BlockSpec((1, 3, D_MODEL), mod_idx),
        pl.BlockSpec((1, D_MODEL), const2),
        pl.BlockSpec((D_MODEL, IN_WIDTH), const2, pipeline_mode=pl.Buffered(1)),
        pl.BlockSpec((1, Q_W), const2),
        pl.BlockSpec((1, KV_W), const2),
    ]
    args = [x, mod, nw, w_bf, qn_t, kn_t]
    if rope:
        in_specs += [pl.BlockSpec((tm, LANES), lambda bi, i: (i, 0))] * 2
        args += list(rope_tabs)
    widths = (Q_W, KV_W, KV_W, Q_W, Q_W, Q_W, Q_W, KV_W, KV_W, MIX_WIDTH)
    out_specs = [pl.BlockSpec((1, tm, w), lambda bi, i: (bi, i, 0)) for w in widths]
    out_shape = [jax.ShapeDtypeStruct((b, n, w), BF16) for w in widths]
    return pl.pallas_call(
        functools.partial(_in_proj_kernel, rope=rope),
        grid=(b, n // tm),
        in_specs=in_specs,
        out_specs=out_specs,
        out_shape=out_shape,
        compiler_params=_cparams(("parallel", "parallel")),
        name="in_proj_rope" if rope else "in_proj_ctx",
    )(*args)


def _stack_group_q(q, g):
    t = q.shape[0]
    lane = lax.broadcasted_iota(jnp.int32, (t, LANES), 1)
    keep = (lane // HEAD_DIM) == g
    parts = []
    for j in range(GROUP):
        h = GROUP * g + j
        chunk = q[:, (h // 2) * LANES:(h // 2 + 1) * LANES].astype(F32)
        if h % 2 != g:
            chunk = pltpu.roll(chunk, HEAD_DIM, 1)
        parts.append(jnp.where(keep, chunk, 0.0).astype(BF16))
    return jnp.concatenate(parts, axis=0)


def _unstack_group_out(o, g, t):
    lane = lax.broadcasted_iota(jnp.int32, (t, LANES), 1)
    left = lane < HEAD_DIM
    chunks = []
    for cl in range(2):
        even = o[(2 * cl) * t:(2 * cl + 1) * t]
        odd = o[(2 * cl + 1) * t:(2 * cl + 2) * t]
        if g == 0:
            odd = pltpu.roll(odd, HEAD_DIM, 1)
        else:
            even = pltpu.roll(even, HEAD_DIM, 1)
        chunks.append(jnp.where(left, even, odd))
    return chunks


def _softmax_pv(s_list, v_list, extra_logit=None):
    m = s_list[0].max(axis=1, keepdims=True)
    for s in s_list[1:]:
        m = jnp.maximum(m, s.max(axis=1, keepdims=True))
    if extra_logit is not None:
        m = jnp.maximum(m, extra_logit)
    l = None
    o = None
    for s, v in zip(s_list, v_list):
        p = jnp.exp(s - m)
        ps = p.sum(axis=1, keepdims=True)
        po = _dot(p.astype(BF16), v)
        l = ps if l is None else l + ps
        o = po if o is None else o + po
    if extra_logit is not None:
        l = l + jnp.exp(extra_logit - m)
    return o / l


def _sink_column(sink_ref, g, t):
    row = lax.broadcasted_iota(jnp.int32, (GROUP * t, 1), 0) // t
    col = jnp.full((GROUP * t, 1), sink_ref[GROUP * g], F32)
    for j in range(1, GROUP):
        col = jnp.where(row == j, sink_ref[GROUP * g + j], col)
    return col


ONES_ROWS = 16


def _attn_a_kernel(q_ref, k_ref, v_ref, kc_ref, vc_ref, o_ref, vt_ref, vct_ref, qs_ref, m_ref, acc_ref,
                   *, tq, tk):
    def transpose_bf16(z):
        return z.astype(F32).T.astype(BF16)

    @pl.when(pl.program_id(1) == 0)
    def _():
        for j in range(SEQ // tk):
            vt_ref[j] = transpose_bf16(v_ref[0, j * tk:(j + 1) * tk, :])
        vct_ref[...] = transpose_bf16(vc_ref[0])

    def step(k, vt_g):
        s = _dot(k, qs_ref[...])
        m_prev = m_ref[...]
        m_new = jnp.maximum(m_prev, s.max(axis=0, keepdims=True))
        alpha = jnp.exp(m_prev - m_new)
        p = jnp.exp(s - m_new).astype(BF16)
        v_ext = jnp.concatenate([vt_g, jnp.ones((ONES_ROWS, vt_g.shape[1]), BF16)], axis=0)
        acc_ref[...] = alpha * acc_ref[...] + _dot(v_ext, p)
        m_ref[...] = m_new

    qt = transpose_bf16(q_ref[0])
    for g in range(N_KV):
        rows = slice(g * HEAD_DIM, (g + 1) * HEAD_DIM)
        qs_ref[...] = jnp.zeros(qs_ref.shape, BF16)
        for j in range(GROUP):
            h = GROUP * g + j
            qs_ref[rows, j * tq:(j + 1) * tq] = qt[h * HEAD_DIM:(h + 1) * HEAD_DIM, :]
        m_ref[...] = jnp.full(m_ref.shape, NEG, F32)
        acc_ref[...] = jnp.zeros(acc_ref.shape, F32)

        def body(j, carry):
            off = pl.multiple_of(j * tk, tk)
            step(k_ref[0, pl.ds(off, tk), :], vt_ref[j, rows, :])
            return carry

        lax.fori_loop(0, SEQ // tk, body, 0)
        step(kc_ref[0], vct_ref[rows, :])
        o = acc_ref[0:HEAD_DIM, :] / acc_ref[HEAD_DIM:HEAD_DIM + 1, :]
        for cl in range(GROUP // 2):
            pair = jnp.concatenate([o[:, (2 * cl) * tq:(2 * cl + 1) * tq],
                                    o[:, (2 * cl + 1) * tq:(2 * cl + 2) * tq]], axis=0)
            c = (GROUP // 2) * g + cl
            o_ref[0, :, c * LANES:(c + 1) * LANES] = pair.T.astype(BF16)


def _attn_a(q, k, v, kc, vc, tq=256, tk=512):
    b = q.shape[0]
    res = lambda n: pl.BlockSpec((1, n, KV_W), lambda bi, i: (bi, 0, 0))
    return pl.pallas_call(
        functools.partial(_attn_a_kernel, tq=tq, tk=tk),
        grid=(b, SEQ // tq),
        in_specs=[pl.BlockSpec((1, tq, Q_W), lambda bi, i: (bi, i, 0)),
                  res(SEQ), res(SEQ), res(CTX_LEN), res(CTX_LEN)],
        out_specs=pl.BlockSpec((1, tq, Q_W), lambda bi, i: (bi, i, 0)),
        out_shape=jax.ShapeDtypeStruct((b, SEQ, Q_W), BF16),
        scratch_shapes=[pltpu.VMEM((SEQ // tk, KV_W, tk), BF16),
                        pltpu.VMEM((KV_W, CTX_LEN), BF16),
                        pltpu.VMEM((LANES, GROUP * tq), BF16),
                        pltpu.VMEM((1, GROUP * tq), F32),
                        pltpu.VMEM((HEAD_DIM + ONES_ROWS, GROUP * tq), F32)],
        compiler_params=_cparams(("parallel", "arbitrary")),
        name="attn_global",
    )(q, k, v, kc, vc)


def _attn_c_kernel(sink_ref, q_ref, k_ref, v_ref, kc_ref, vc_ref, o_ref, *, tq):
    win = tq + 2 * WINDOW
    i = pl.program_id(1)
    ws = pl.multiple_of(jnp.clip(i * tq - WINDOW, 0, SEQ - win), WINDOW)
    kw = k_ref[0, pl.ds(ws, win), :]
    vw = v_ref[0, pl.ds(ws, win), :]
    qpos = i * tq + lax.broadcasted_iota(jnp.int32, (GROUP * tq, win), 0) % tq
    kpos = ws + lax.broadcasted_iota(jnp.int32, (GROUP * tq, win), 1)
    valid = jnp.abs(qpos - kpos) <= WINDOW
    for g in range(N_KV):
        qs = _stack_group_q(q_ref[0], g)
        s_win = jnp.where(valid, _dot_nt(qs, kw), NEG)
        s_ctx = _dot_nt(qs, kc_ref[0])
        o = _softmax_pv([s_win, s_ctx], [vw, vc_ref[0]], _sink_column(sink_ref, g, tq))
        chunks = _unstack_group_out(o, g, tq)
        for cl in range(2):
            c = 2 * g + cl
            o_ref[0, :, c * LANES:(c + 1) * LANES] = chunks[cl].astype(BF16)


def _attn_c(sink, q, k, v, kc, vc, tq=256):
    b = q.shape[0]
    res = lambda n: pl.BlockSpec((1, n, KV_W), lambda bi, i: (bi, 0, 0))
    return pl.pallas_call(
        functools.partial(_attn_c_kernel, tq=tq),
        grid=(b, SEQ // tq),
        in_specs=[pl.BlockSpec(memory_space=pltpu.SMEM),
                  pl.BlockSpec((1, tq, Q_W), lambda bi, i: (bi, i, 0)),
                  res(SEQ), res(SEQ), res(CTX_LEN), res(CTX_LEN)],
        out_specs=pl.BlockSpec((1, tq, Q_W), lambda bi, i: (bi, i, 0)),
        out_shape=jax.ShapeDtypeStruct((b, SEQ, Q_W), BF16),
        compiler_params=_cparams(("parallel", "parallel")),
        name="attn_window",
    )(sink, q, k, v, kc, vc)


def _nb_window_row(r0):
    return min(max(r0 - NB_ROWS // 2, 0), GRID_H - NB_WIN_ROWS)


def _nb_bias_kernel(rpb_ref, o_ref, tt_ref):
    base = (pl.program_id(0) * N_HEADS + pl.program_id(1)) * N_REL
    cq = lax.broadcasted_iota(jnp.int32, (GRID_W, LANES), 0)
    lane = lax.broadcasted_iota(jnp.int32, (GRID_W, LANES), 1)
    ck = lane & (GRID_W - 1)
    d = ck - cq + (NB_COLS - 1)
    cs = jnp.clip(cq - NB_COLS // 2, 0, GRID_W - NB_COLS)
    col_ok = (ck >= cs) & (ck < cs + NB_COLS)
    n_dc = 2 * NB_COLS - 1
    for a in range(2 * NB_ROWS - 1):
        t = jnp.full((GRID_W, LANES), NEG, F32)
        for dd in range(n_dc):
            t = jnp.where(d == dd, rpb_ref[base + a * n_dc + dd], t)
        tt_ref[a] = jnp.where(col_ok, t, NEG)
    neg_tile = jnp.full((GRID_W, LANES), NEG, F32)
    left = lane < GRID_W
    for case, r0 in enumerate(NB_CASE_R0):
        w0 = _nb_window_row(r0)
        for qr in range(2):
            rq = r0 + qr
            rs = min(max(rq - NB_ROWS // 2, 0), GRID_H - NB_ROWS)
            for j in range(NB_WIN_ROWS // 2):
                halves = []
                for kr in (2 * j, 2 * j + 1):
                    rk = w0 + kr
                    halves.append(tt_ref[rk - rq + NB_ROWS - 1] if rs <= rk < rs + NB_ROWS else neg_tile)
                o_ref[0, case, 0, qr * GRID_W:(qr + 1) * GRID_W, j * LANES:(j + 1) * LANES] = (
                    jnp.where(left, halves[0], halves[1]))


def _nb_bias(rpb):
    n_case = len(NB_CASE_R0)
    return pl.pallas_call(
        _nb_bias_kernel,
        grid=(DEPTH, N_HEADS),
        in_specs=[pl.BlockSpec(memory_space=pltpu.SMEM)],
        out_specs=pl.BlockSpec((1, n_case, 1, NB_TQ, NB_WIN), lambda l, h: (l, 0, h, 0, 0)),
        out_shape=jax.ShapeDtypeStruct((DEPTH, n_case, N_HEADS, NB_TQ, NB_WIN), F32),
        scratch_shapes=[pltpu.VMEM((2 * NB_ROWS - 1, GRID_W, LANES), F32)],
        compiler_params=_cparams(("arbitrary", "arbitrary")),
        name="nb_bias",
    )(rpb.reshape(-1))


def _pair_attention(qp, s_extra_fn, k_list, v_list):
    t = qp.shape[0]
    lane = lax.broadcasted_iota(jnp.int32, (t, LANES), 1)
    qf = qp.astype(F32)
    outs = []
    for h in range(2):
        qpad = jnp.where((lane // HEAD_DIM) == h, qf, 0.0).astype(BF16)
        s_list = [_dot_nt(qpad, k) for k in k_list]
        s_list = s_extra_fn(h, s_list)
        outs.append(_softmax_pv(s_list, v_list))
    return jnp.where(lane < HEAD_DIM, outs[0], outs[1])


def _attn_b_kernel(q_ref, k_ref, v_ref, kc_ref, vc_ref, bias_ref, o_ref):
    i = pl.program_id(1)
    w0 = jnp.clip(2 * i - NB_ROWS // 2, 0, GRID_H - NB_WIN_ROWS)
    off = pl.multiple_of(w0 * GRID_W, LANES)
    for c in range(N_HEADS // 2):
        cols = slice(c * LANES, (c + 1) * LANES)
        add_bias = lambda h, s_list, c=c: [s_list[0] + bias_ref[0, 2 * c + h], s_list[1]]
        o = _pair_attention(
            q_ref[0, :, cols], add_bias,
            [k_ref[0, pl.ds(off, NB_WIN), cols], kc_ref[0, :, cols]],
            [v_ref[0, pl.ds(off, NB_WIN), cols], vc_ref[0, :, cols]])
        o_ref[0, :, cols] = o.astype(BF16)


def _attn_b(q, k, v, kc, vc, bias):
    b = q.shape[0]
    n_tiles = SEQ // NB_TQ
    res = lambda n: pl.BlockSpec((1, n, Q_W), lambda bi, i: (bi, 0, 0), pipeline_mode=pl.Buffered(1))

    def bias_idx(bi, i):
        return (jnp.minimum(i, 2) + jnp.maximum(i - (n_tiles - 3), 0), 0, 0, 0)

    return pl.pallas_call(
        _attn_b_kernel,
        grid=(b, n_tiles),
        in_specs=[pl.BlockSpec((1, NB_TQ, Q_W), lambda bi, i: (bi, i, 0)),
                  res(SEQ), res(SEQ), res(CTX_LEN), res(CTX_LEN),
                  pl.BlockSpec((1, N_HEADS, NB_TQ, NB_WIN), bias_idx)],
        out_specs=pl.BlockSpec((1, NB_TQ, Q_W), lambda bi, i: (bi, i, 0)),
        out_shape=jax.ShapeDtypeStruct((b, SEQ, Q_W), BF16),
        compiler_params=_cparams(("parallel", "arbitrary")),
        name="attn_neighbourhood",
    )(q, k, v, kc, vc, bias)


def _attn_ctx_kernel(sink_ref, qa_ref, ka_ref, va_ref, qb_ref, kb_ref, vb_ref, qc_ref, kc_ref, vc_ref,
                     oa_ref, ob_ref, oc_ref):
    t = CTX_LEN
    for g in range(N_KV):
        for q_ref, k_ref, v_ref, o_ref, sink in ((qa_ref, ka_ref, va_ref, oa_ref, False),
                                                 (qc_ref, kc_ref, vc_ref, oc_ref, True)):
            qs = _stack_group_q(q_ref[0], g)
            extra = _sink_column(sink_ref, g, t) if sink else None
            o = _softmax_pv([_dot_nt(qs, k_ref[0])], [v_ref[0]], extra)
            chunks = _unstack_group_out(o, g, t)
            for cl in range(2):
                c = 2 * g + cl
                o_ref[0, :, c * LANES:(c + 1) * LANES] = chunks[cl].astype(BF16)
    for c in range(N_HEADS // 2):
        cols = slice(c * LANES, (c + 1) * LANES)
        o = _pair_attention(qb_ref[0, :, cols], lambda h, s_list: s_list,
                            [kb_ref[0, :, cols]], [vb_ref[0, :, cols]])
        ob_ref[0, :, cols] = o.astype(BF16)


def _attn_ctx(sink, qa, ka, va, qb, kb, vb, qc, kc, vc):
    b = qa.shape[0]
    spec = lambda w: pl.BlockSpec((1, CTX_LEN, w), lambda bi: (bi, 0, 0))
    return pl.pallas_call(
        _attn_ctx_kernel,
        grid=(b,),
        in_specs=[pl.BlockSpec(memory_space=pltpu.SMEM),
                  spec(Q_W), spec(KV_W), spec(KV_W), spec(Q_W), spec(Q_W), spec(Q_W),
                  spec(Q_W), spec(KV_W), spec(KV_W)],
        out_specs=[spec(Q_W)] * 3,
        out_shape=[jax.ShapeDtypeStruct((b, CTX_LEN, Q_W), BF16)] * 3,
        compiler_params=_cparams(("parallel",)),
        name="attn_ctx",
    )(sink, qa, ka, va, qb, kb, vb, qc, kc, vc)


def _out_proj_kernel(*refs, final):
    if final:
        ya_ref, yb_ref, yc_ref, sg_ref, x_ref, mod_ref, w_ref, fw_ref, o_ref = refs
    else:
        ya_ref, yb_ref, yc_ref, sg_ref, x_ref, mod_ref, w_ref, o_ref = refs
    acc = None
    for idx, y_ref in enumerate((ya_ref, yb_ref, yc_ref)):
        cols = slice(idx * Q_W, (idx + 1) * Q_W)
        u = (y_ref[0].astype(F32) * sg_ref[0, :, cols].astype(F32)).astype(BF16)
        part = _dot(u, w_ref[cols, :])
        acc = part if acc is None else acc + part
    xn = x_ref[0] + mod_ref[0, 2:3, :] * acc
    if final:
        xn = (xn * lax.rsqrt(jnp.mean(xn * xn, axis=-1, keepdims=True) + EPS)) * fw_ref[...]
    o_ref[0] = xn


def _out_proj(ya, yb, yc, sg, x, mod, w_bf, final_w, tm):
    b, n, _ = x.shape
    final = final_w is not None
    mod_idx = (lambda bi, i: (bi, 0, 0)) if mod.shape[0] > 1 else (lambda bi, i: (0, 0, 0))
    row = lambda w: pl.BlockSpec((1, tm, w), lambda bi, i: (bi, i, 0))
    in_specs = [row(Q_W), row(Q_W), row(Q_W), row(MIX_WIDTH), row(D_MODEL),
                pl.BlockSpec((1, 3, D_MODEL), mod_idx),
                pl.BlockSpec((MIX_WIDTH, D_MODEL), lambda bi, i: (0, 0))]
    args = [ya, yb, yc, sg, x, mod, w_bf]
    if final:
        in_specs.append(pl.BlockSpec((1, D_MODEL), lambda bi, i: (0, 0)))
        args.append(final_w)
    return pl.pallas_call(
        functools.partial(_out_proj_kernel, final=final),
        grid=(b, n // tm),
        in_specs=in_specs,
        out_specs=row(D_MODEL),
        out_shape=jax.ShapeDtypeStruct((b, n, D_MODEL), F32),
        compiler_params=_cparams(("parallel", "parallel")),
        name="out_proj_final" if final else "out_proj",
    )(*args)


def _rope_tables():
    t = jnp.arange(SEQ, dtype=jnp.int32)
    rows = (t // GRID_W).astype(F32)
    cols = (t % GRID_W).astype(F32)
    n_freq = HEAD_DIM // 4
    freq = ROPE_THETA ** (-jnp.arange(n_freq, dtype=F32) / n_freq)
    ang = jnp.concatenate([rows[:, None] * freq, cols[:, None] * freq], axis=-1)
    cos, sin = jnp.cos(ang), jnp.sin(ang)
    cos_t = jnp.tile(jnp.concatenate([cos, cos], axis=-1), (1, LANES // HEAD_DIM))
    sin_t = jnp.tile(jnp.concatenate([-sin, sin], axis=-1), (1, LANES // HEAD_DIM))
    return cos_t, sin_t


def kernel(x, c, ctx, c_ctx, norm_w, ada_w, ada_b, w_in, w_out, q_norm_a, k_norm_a, rpb_b, sink_c,
           final_norm_w):
    bsz = x.shape[0]
    cvecs = jnp.concatenate([c, c_ctx[None], jnp.zeros((8 - bsz - 1, D_MODEL), F32)], axis=0)
    mod = _modulation(cvecs, ada_w, ada_b).reshape(DEPTH, 8, 3, D_MODEL)
    bias = _nb_bias(rpb_b)
    rope_tabs = _rope_tables()
    w_in_bf = w_in.astype(BF16)
    w_out_bf = w_out.astype(BF16)
    cx = ctx
    for l in range(DEPTH):
        need_ctx = l < DEPTH - 1
        mod_x, mod_c = mod[l, :bsz], mod[l, bsz:bsz + 1]
        nw = norm_w[l][None]
        qn_t = jnp.tile(q_norm_a[l], N_HEADS)[None]
        kn_t = jnp.tile(k_norm_a[l], N_KV)[None]
        qa, ka, va, qb, kb, vb, qc, kc, vc, sg = _in_proj(
            x, mod_x, nw, w_in_bf[l], qn_t, kn_t, rope_tabs, tm=512)
        qa_c, ka_c, va_c, qb_c, kb_c, vb_c, qc_c, kc_c, vc_c, sg_c = _in_proj(
            cx, mod_c, nw, w_in_bf[l], qn_t, kn_t, None, tm=CTX_LEN)
        ya = _attn_a(qa, ka, va, ka_c, va_c)
        yb = _attn_b(qb, kb, vb, kb_c, vb_c, bias[l])
        yc = _attn_c(sink_c[l], qc, kc, vc, kc_c, vc_c)
        if need_ctx:
            ya_c, yb_c, yc_c = _attn_ctx(sink_c[l], qa_c, ka_c, va_c, qb_c, kb_c, vb_c, qc_c, kc_c, vc_c)
            cx = _out_proj(ya_c, yb_c, yc_c, sg_c, cx, mod_c, w_out_bf[l], None, tm=CTX_LEN)
        x = _out_proj(ya, yb, yc, sg, x, mod_x, w_out_bf[l],
                      None if need_ctx else final_norm_w[None], tm=512)
    return x
```

```python
import functools

import jax
import jax.numpy as jnp
from jax import lax
from jax.experimental import pallas as pl
from jax.experimental.pallas import tpu as pltpu

D_MODEL = 1024
SEQ = 8192
DEPTH = 2
GRID_W = 64
GRID_H = SEQ // GRID_W
CTX_LEN = 256
HEAD_DIM = 64
N_HEADS = 8
N_KV = 2
GROUP = N_HEADS // N_KV
Q_W = N_HEADS * HEAD_DIM
KV_W = N_KV * HEAD_DIM
MIX_WIDTH = 3 * Q_W
IN_WIDTH = 3 * Q_W + 2 * KV_W + 2 * Q_W + 2 * KV_W + MIX_WIDTH
NB_ROWS = 8
NB_COLS = 16
WINDOW = 128
ROPE_THETA = 10000.0
EPS = 1e-6
SCALE = HEAD_DIM ** -0.5
LOG2E = 1.4426950408889634
NEG = -1e30

LANES = 128
VMEM_LIMIT = 56 * 1024 * 1024

OFF_AQ, OFF_AK, OFF_AV = 0, 512, 640
OFF_BQ, OFF_BK, OFF_BV = 768, 1280, 1792
OFF_CQ, OFF_CK, OFF_CV = 2304, 2816, 2944
OFF_G = 3072

NB_TQ = 2 * GRID_W
NB_WIN_ROWS = 10
NB_WIN = NB_WIN_ROWS * GRID_W
NB_CASE_R0 = (0, 2, 4, GRID_H - 4, GRID_H - 2)
N_REL = (2 * NB_ROWS - 1) * (2 * NB_COLS - 1)

F32 = jnp.float32
BF16 = jnp.bfloat16


def _cparams(sem):
    return pltpu.CompilerParams(dimension_semantics=sem, vmem_limit_bytes=VMEM_LIMIT)


def _dot(a, b):
    return jnp.dot(a, b, preferred_element_type=F32)


def _dot_nt(a, b):
    return lax.dot_general(a, b, (((1,), (1,)), ((), ())), preferred_element_type=F32)


def _silu(x):
    return x * (1.0 / (1.0 + jnp.exp(-x)))


def _mod_kernel(c_ref, w_ref, b_ref, o_ref):
    cs = _silu(c_ref[...])
    o_ref[0] = jnp.dot(cs, w_ref[0], preferred_element_type=F32,
                       precision=lax.Precision.HIGHEST) + b_ref[0]


def _modulation(cvecs, ada_w, ada_b):
    tn = 1024
    return pl.pallas_call(
        _mod_kernel,
        grid=(DEPTH, 3 * D_MODEL // tn),
        in_specs=[
            pl.BlockSpec((8, D_MODEL), lambda l, j: (0, 0)),
            pl.BlockSpec((1, D_MODEL, tn), lambda l, j: (l, 0, j)),
            pl.BlockSpec((1, 1, tn), lambda l, j: (l, 0, j)),
        ],
        out_specs=pl.BlockSpec((1, 8, tn), lambda l, j: (l, 0, j)),
        out_shape=jax.ShapeDtypeStruct((DEPTH, 8, 3 * D_MODEL), F32),
        compiler_params=_cparams(("arbitrary", "arbitrary")),
        name="adaln_mod",
    )(cvecs, ada_w, ada_b.reshape(DEPTH, 1, 3 * D_MODEL))


def _head_rms(z, wn):
    w = z.shape[1]
    r = lax.broadcasted_iota(jnp.int32, (w, w), 0) // HEAD_DIM
    c = lax.broadcasted_iota(jnp.int32, (w, w), 1) // HEAD_DIM
    ones = (r == c).astype(BF16)
    z2 = z * z
    hi = z2.astype(BF16)
    lo = (z2 - hi.astype(F32)).astype(BF16)
    ss = _dot(hi, ones) + _dot(lo, ones)
    return (z * lax.rsqrt(ss * (1.0 / HEAD_DIM) + EPS)) * wn


def _rope(z, cos_t, sin_t):
    w = z.shape[1]
    reps = w // LANES
    if reps > 1:
        cos_t = jnp.concatenate([cos_t] * reps, axis=1)
        sin_t = jnp.concatenate([sin_t] * reps, axis=1)
    lane = lax.broadcasted_iota(jnp.int32, z.shape, 1)
    partner = jnp.where((lane & (HEAD_DIM // 2)) == 0,
                        pltpu.roll(z, w - HEAD_DIM // 2, 1), pltpu.roll(z, HEAD_DIM // 2, 1))
    return z * cos_t + partner * sin_t


def _in_proj_kernel(*refs, rope):
    if rope:
        (x_ref, mod_ref, nw_ref, w_ref, qn_ref, kn_ref, cos_ref, sin_ref,
         qa_ref, ka_ref, va_ref, qb_ref, kb_ref, vb_ref, qc_ref, kc_ref, vc_ref, sg_ref) = refs
        cos_t, sin_t = cos_ref[...], sin_ref[...]
        rot = lambda z: _rope(z, cos_t, sin_t)
    else:
        (x_ref, mod_ref, nw_ref, w_ref, qn_ref, kn_ref,
         qa_ref, ka_ref, va_ref, qb_ref, kb_ref, vb_ref, qc_ref, kc_ref, vc_ref, sg_ref) = refs
        rot = lambda z: z
    x = x_ref[0]
    y = (x * lax.rsqrt(jnp.mean(x * x, axis=-1, keepdims=True) + EPS)) * nw_ref[...]
    h = (y * (1.0 + mod_ref[0, 1:2, :]) + mod_ref[0, 0:1, :]).astype(BF16)

    def proj(off, width):
        return _dot(h, w_ref[:, off:off + width])

    qa_scale = SCALE * LOG2E if rope else SCALE
    qa_ref[0] = (rot(_head_rms(proj(OFF_AQ, Q_W), qn_ref[...])) * qa_scale).astype(BF16)
    ka_ref[0] = rot(_head_rms(proj(OFF_AK, KV_W), kn_ref[...])).astype(BF16)
    va_ref[0] = proj(OFF_AV, KV_W).astype(BF16)
    qb_ref[0] = (proj(OFF_BQ, Q_W) * SCALE).astype(BF16)
    kb_ref[0] = proj(OFF_BK, Q_W).astype(BF16)
    vb_ref[0] = proj(OFF_BV, Q_W).astype(BF16)
    qc_ref[0] = (rot(proj(OFF_CQ, Q_W)) * SCALE).astype(BF16)
    kc_ref[0] = rot(proj(OFF_CK, KV_W)).astype(BF16)
    vc_ref[0] = proj(OFF_CV, KV_W).astype(BF16)
    sg_ref[0] = _silu(proj(OFF_G, MIX_WIDTH)).astype(BF16)


def _in_proj(x, mod, nw, w_bf, qn_t, kn_t, rope_tabs, tm):
    b, n, _ = x.shape
    rope = rope_tabs is not None
    mod_idx = (lambda bi, i: (bi, 0, 0)) if mod.shape[0] > 1 else (lambda bi, i: (0, 0, 0))
    const2 = lambda bi, i: (0, 0)
    in_specs = [
        pl.BlockSpec((1, tm, D_MODEL), lambda bi, i: (bi, i, 0)),
        pl.BlockSpec((1, 3, D_MODEL), mod_idx),
        pl.BlockSpec((1, D_MODEL), const2),
        pl.BlockSpec((D_MODEL, IN_WIDTH), const2, pipeline_mode=pl.Buffered(1)),
        pl.BlockSpec((1, Q_W), const2),
        pl.BlockSpec((1, KV_W), const2),
    ]
    args = [x, mod, nw, w_bf, qn_t, kn_t]
    if rope:
        in_specs += [pl.BlockSpec((tm, LANES), lambda bi, i: (i, 0))] * 2
        args += list(rope_tabs)
    widths = (Q_W, KV_W, KV_W, Q_W, Q_W, Q_W, Q_W, KV_W, KV_W, MIX_WIDTH)
    out_specs = [pl.BlockSpec((1, tm, w), lambda bi, i: (bi, i, 0)) for w in widths]
    out_shape = [jax.ShapeDtypeStruct((b, n, w), BF16) for w in widths]
    return pl.pallas_call(
        functools.partial(_in_proj_kernel, rope=rope),
        grid=(b, n // tm),
        in_specs=in_specs,
        out_specs=out_specs,
        out_shape=out_shape,
        compiler_params=_cparams(("parallel", "parallel")),
        name="in_proj_rope" if rope else "in_proj_ctx",
    )(*args)


def _stack_group_q(q, g):
    t = q.shape[0]
    lane = lax.broadcasted_iota(jnp.int32, (t, LANES), 1)
    keep = (lane // HEAD_DIM) == g
    parts = []
    for j in range(GROUP):
        h = GROUP * g + j
        chunk = q[:, (h // 2) * LANES:(h // 2 + 1) * LANES].astype(F32)
        if h % 2 != g:
            chunk = pltpu.roll(chunk, HEAD_DIM, 1)
        parts.append(jnp.where(keep, chunk, 0.0).astype(BF16))
    return jnp.concatenate(parts, axis=0)


def _unstack_group_out(o, g, t):
    lane = lax.broadcasted_iota(jnp.int32, (t, LANES), 1)
    left = lane < HEAD_DIM
    chunks = []
    for cl in range(2):
        even = o[(2 * cl) * t:(2 * cl + 1) * t]
        odd = o[(2 * cl + 1) * t:(2 * cl + 2) * t]
        if g == 0:
            odd = pltpu.roll(odd, HEAD_DIM, 1)
        else:
            even = pltpu.roll(even, HEAD_DIM, 1)
        chunks.append(jnp.where(left, even, odd))
    return chunks


def _softmax_pv(s_list, v_list, extra_logit=None):
    m = s_list[0].max(axis=1, keepdims=True)
    for s in s_list[1:]:
        m = jnp.maximum(m, s.max(axis=1, keepdims=True))
    if extra_logit is not None:
        m = jnp.maximum(m, extra_logit)
    l = None
    o = None
    for s, v in zip(s_list, v_list):
        p = jnp.exp(s - m)
        ps = p.sum(axis=1, keepdims=True)
        po = _dot(p.astype(BF16), v)
        l = ps if l is None else l + ps
        o = po if o is None else o + po
    if extra_logit is not None:
        l = l + jnp.exp(extra_logit - m)
    return o / l


def _sink_column(sink_ref, g, t):
    row = lax.broadcasted_iota(jnp.int32, (GROUP * t, 1), 0) // t
    col = jnp.full((GROUP * t, 1), sink_ref[GROUP * g], F32)
    for j in range(1, GROUP):
        col = jnp.where(row == j, sink_ref[GROUP * g + j], col)
    return col


ONES_ROWS = 16


def _attn_a_kernel(q_ref, k_ref, v_ref, kc_ref, vc_ref, o_ref, vt_ref, vct_ref, qs_ref, m_ref, acc_ref,
                   s0_ref, s1_ref, s2_ref, mx0_ref, mx1_ref, mx2_ref, *, tq, tk, col_blk):
    n_chunks = SEQ // tk

    def transpose_bf16(z):
        return z.astype(F32).T.astype(BF16)

    @pl.when(pl.program_id(1) == 0)
    def _():
        for j in range(SEQ // tk):
            vt_ref[j] = transpose_bf16(v_ref[0, j * tk:(j + 1) * tk, :])
        vct_ref[...] = transpose_bf16(vc_ref[0])

    col_blocks = [slice(c0, c0 + col_blk) for c0 in range(0, GROUP * tq, col_blk)]

    def score_block(k, s_ref, mx_ref, cols):
        s = _dot(k, qs_ref[:, cols])
        s_ref[0:k.shape[0], cols] = s
        mx_ref[:, cols] = s.max(axis=0, keepdims=True)

    def absorb_block(n, s_ref, mx_ref, v_ext, cols):
        m_prev = m_ref[:, cols]
        m_new = jnp.maximum(m_prev, mx_ref[:, cols])
        alpha = jnp.exp2(m_prev - m_new)
        p = jnp.exp2(s_ref[0:n, cols] - m_new).astype(BF16)
        acc_ref[:, cols] = alpha * acc_ref[:, cols] + _dot(v_ext, p)
        m_ref[:, cols] = m_new

    def with_ones(vt_g):
        return jnp.concatenate([vt_g, jnp.ones((ONES_ROWS, vt_g.shape[1]), BF16)], axis=0)

    def latent_keys(j):
        return k_ref[0, pl.ds(pl.multiple_of(j * tk, tk), tk), :]

    qt = transpose_bf16(q_ref[0])
    for g in range(N_KV):
        rows = slice(g * HEAD_DIM, (g + 1) * HEAD_DIM)
        qs_ref[...] = jnp.zeros(qs_ref.shape, BF16)
        for j in range(GROUP):
            h = GROUP * g + j
            qs_ref[rows, j * tq:(j + 1) * tq] = qt[h * HEAD_DIM:(h + 1) * HEAD_DIM, :]
        m_ref[...] = jnp.full(m_ref.shape, NEG, F32)
        acc_ref[...] = jnp.zeros(acc_ref.shape, F32)

        bufs = ((s0_ref, mx0_ref), (s1_ref, mx1_ref), (s2_ref, mx2_ref))

        def trip(n_cur, cur, vt_g, k_next, nxt):
            v_ext = with_ones(vt_g)
            for cols in col_blocks:
                if k_next is not None:
                    score_block(k_next, *bufs[nxt], cols)
                absorb_block(n_cur, *bufs[cur], v_ext, cols)

        kc = kc_ref[0]
        for cols in col_blocks:
            score_block(kc, *bufs[2], cols)
        trip(CTX_LEN, 2, vct_ref[rows, :], latent_keys(0), 0)

        def body(i, carry):
            j = 3 * i
            trip(tk, 0, vt_ref[j, rows, :], latent_keys(j + 1), 1)
            trip(tk, 1, vt_ref[j + 1, rows, :], latent_keys(j + 2), 2)
            trip(tk, 2, vt_ref[j + 2, rows, :], latent_keys(j + 3), 0)
            return carry

        assert n_chunks % 3 == 1
        lax.fori_loop(0, n_chunks // 3, body, 0)
        trip(tk, 0, vt_ref[n_chunks - 1, rows, :], None, None)
        o = acc_ref[0:HEAD_DIM, :] / acc_ref[HEAD_DIM:HEAD_DIM + 1, :]
        for cl in range(GROUP // 2):
            pair = jnp.concatenate([o[:, (2 * cl) * tq:(2 * cl + 1) * tq],
                                    o[:, (2 * cl + 1) * tq:(2 * cl + 2) * tq]], axis=0)
            c = (GROUP // 2) * g + cl
            o_ref[0, :, c * LANES:(c + 1) * LANES] = pair.T.astype(BF16)


def _attn_a(q, k, v, kc, vc, tq=512, tk=512, col_blk=256):
    b = q.shape[0]
    res = lambda n: pl.BlockSpec((1, n, KV_W), lambda bi, i: (bi, 0, 0))
    return pl.pallas_call(
        functools.partial(_attn_a_kernel, tq=tq, tk=tk, col_blk=col_blk),
        grid=(b, SEQ // tq),
        in_specs=[pl.BlockSpec((1, tq, Q_W), lambda bi, i: (bi, i, 0)),
                  res(SEQ), res(SEQ), res(CTX_LEN), res(CTX_LEN)],
        out_specs=pl.BlockSpec((1, tq, Q_W), lambda bi, i: (bi, i, 0)),
        out_shape=jax.ShapeDtypeStruct((b, SEQ, Q_W), BF16),
        scratch_shapes=[pltpu.VMEM((SEQ // tk, KV_W, tk), BF16),
                        pltpu.VMEM((KV_W, CTX_LEN), BF16),
                        pltpu.VMEM((LANES, GROUP * tq), BF16),
                        pltpu.VMEM((1, GROUP * tq), F32),
                        pltpu.VMEM((HEAD_DIM + ONES_ROWS, GROUP * tq), F32),
                        pltpu.VMEM((tk, GROUP * tq), F32),
                        pltpu.VMEM((tk, GROUP * tq), F32),
                        pltpu.VMEM((tk, GROUP * tq), F32),
                        pltpu.VMEM((1, GROUP * tq), F32),
                        pltpu.VMEM((1, GROUP * tq), F32),
                        pltpu.VMEM((1, GROUP * tq), F32)],
        compiler_params=_cparams(("parallel", "arbitrary")),
        name="attn_global",
    )(q, k, v, kc, vc)


def _attn_c_kernel(sink_ref, q_ref, k_ref, v_ref, kc_ref, vc_ref, o_ref, *, tq):
    win = tq + 2 * WINDOW
    i = pl.program_id(1)
    ws = pl.multiple_of(jnp.clip(i * tq - WINDOW, 0, SEQ - win), WINDOW)
    kw = k_ref[0, pl.ds(ws, win), :]
    vw = v_ref[0, pl.ds(ws, win), :]
    qpos = i * tq + lax.broadcasted_iota(jnp.int32, (GROUP * tq, win), 0) % tq
    kpos = ws + lax.broadcasted_iota(jnp.int32, (GROUP * tq, win), 1)
    valid = jnp.abs(qpos - kpos) <= WINDOW
    for g in range(N_KV):
        qs = _stack_group_q(q_ref[0], g)
        s_win = jnp.where(valid, _dot_nt(qs, kw), NEG)
        s_ctx = _dot_nt(qs, kc_ref[0])
        o = _softmax_pv([s_win, s_ctx], [vw, vc_ref[0]], _sink_column(sink_ref, g, tq))
        chunks = _unstack_group_out(o, g, tq)
        for cl in range(2):
            c = 2 * g + cl
            o_ref[0, :, c * LANES:(c + 1) * LANES] = chunks[cl].astype(BF16)


def _attn_c(sink, q, k, v, kc, vc, tq=256):
    b = q.shape[0]
    res = lambda n: pl.BlockSpec((1, n, KV_W), lambda bi, i: (bi, 0, 0))
    return pl.pallas_call(
        functools.partial(_attn_c_kernel, tq=tq),
        grid=(b, SEQ // tq),
        in_specs=[pl.BlockSpec(memory_space=pltpu.SMEM),
                  pl.BlockSpec((1, tq, Q_W), lambda bi, i: (bi, i, 0)),
                  res(SEQ), res(SEQ), res(CTX_LEN), res(CTX_LEN)],
        out_specs=pl.BlockSpec((1, tq, Q_W), lambda bi, i: (bi, i, 0)),
        out_shape=jax.ShapeDtypeStruct((b, SEQ, Q_W), BF16),
        compiler_params=_cparams(("parallel", "parallel")),
        name="attn_window",
    )(sink, q, k, v, kc, vc)


def _nb_window_row(r0):
    return min(max(r0 - NB_ROWS // 2, 0), GRID_H - NB_WIN_ROWS)


def _nb_bias_kernel(rpb_ref, o_ref, tt_ref):
    base = (pl.program_id(0) * N_HEADS + pl.program_id(1)) * N_REL
    cq = lax.broadcasted_iota(jnp.int32, (GRID_W, LANES), 0)
    lane = lax.broadcasted_iota(jnp.int32, (GRID_W, LANES), 1)
    ck = lane & (GRID_W - 1)
    d = ck - cq + (NB_COLS - 1)
    cs = jnp.clip(cq - NB_COLS // 2, 0, GRID_W - NB_COLS)
    col_ok = (ck >= cs) & (ck < cs + NB_COLS)
    n_dc = 2 * NB_COLS - 1
    for a in range(2 * NB_ROWS - 1):
        t = jnp.full((GRID_W, LANES), NEG, F32)
        for dd in range(n_dc):
            t = jnp.where(d == dd, rpb_ref[base + a * n_dc + dd], t)
        tt_ref[a] = jnp.where(col_ok, t, NEG)
    neg_tile = jnp.full((GRID_W, LANES), NEG, F32)
    left = lane < GRID_W
    for case, r0 in enumerate(NB_CASE_R0):
        w0 = _nb_window_row(r0)
        for qr in range(2):
            rq = r0 + qr
            rs = min(max(rq - NB_ROWS // 2, 0), GRID_H - NB_ROWS)
            for j in range(NB_WIN_ROWS // 2):
                halves = []
                for kr in (2 * j, 2 * j + 1):
                    rk = w0 + kr
                    halves.append(tt_ref[rk - rq + NB_ROWS - 1] if rs <= rk < rs + NB_ROWS else neg_tile)
                o_ref[0, case, 0, qr * GRID_W:(qr + 1) * GRID_W, j * LANES:(j + 1) * LANES] = (
                    jnp.where(left, halves[0], halves[1]))


def _nb_bias(rpb):
    n_case = len(NB_CASE_R0)
    return pl.pallas_call(
        _nb_bias_kernel,
        grid=(DEPTH, N_HEADS),
        in_specs=[pl.BlockSpec(memory_space=pltpu.SMEM)],
        out_specs=pl.BlockSpec((1, n_case, 1, NB_TQ, NB_WIN), lambda l, h: (l, 0, h, 0, 0)),
        out_shape=jax.ShapeDtypeStruct((DEPTH, n_case, N_HEADS, NB_TQ, NB_WIN), F32),
        scratch_shapes=[pltpu.VMEM((2 * NB_ROWS - 1, GRID_W, LANES), F32)],
        compiler_params=_cparams(("arbitrary", "arbitrary")),
        name="nb_bias",
    )(rpb.reshape(-1))


def _pair_attention(qp, s_extra_fn, k_list, v_list):
    t = qp.shape[0]
    lane = lax.broadcasted_iota(jnp.int32, (t, LANES), 1)
    qf = qp.astype(F32)
    outs = []
    for h in range(2):
        qpad = jnp.where((lane // HEAD_DIM) == h, qf, 0.0).astype(BF16)
        s_list = [_dot_nt(qpad, k) for k in k_list]
        s_list = s_extra_fn(h, s_list)
        outs.append(_softmax_pv(s_list, v_list))
    return jnp.where(lane < HEAD_DIM, outs[0], outs[1])


def _attn_b_kernel(q_ref, k_ref, v_ref, kc_ref, vc_ref, bias_ref, o_ref):
    i = pl.program_id(1)
    w0 = jnp.clip(2 * i - NB_ROWS // 2, 0, GRID_H - NB_WIN_ROWS)
    off = pl.multiple_of(w0 * GRID_W, LANES)
    for c in range(N_HEADS // 2):
        cols = slice(c * LANES, (c + 1) * LANES)
        add_bias = lambda h, s_list, c=c: [s_list[0] + bias_ref[0, 2 * c + h], s_list[1]]
        o = _pair_attention(
            q_ref[0, :, cols], add_bias,
            [k_ref[0, pl.ds(off, NB_WIN), cols], kc_ref[0, :, cols]],
            [v_ref[0, pl.ds(off, NB_WIN), cols], vc_ref[0, :, cols]])
        o_ref[0, :, cols] = o.astype(BF16)


def _attn_b(q, k, v, kc, vc, bias):
    b = q.shape[0]
    n_tiles = SEQ // NB_TQ
    res = lambda n: pl.BlockSpec((1, n, Q_W), lambda bi, i: (bi, 0, 0), pipeline_mode=pl.Buffered(1))

    def bias_idx(bi, i):
        return (jnp.minimum(i, 2) + jnp.maximum(i - (n_tiles - 3), 0), 0, 0, 0)

    return pl.pallas_call(
        _attn_b_kernel,
        grid=(b, n_tiles),
        in_specs=[pl.BlockSpec((1, NB_TQ, Q_W), lambda bi, i: (bi, i, 0)),
                  res(SEQ), res(SEQ), res(CTX_LEN), res(CTX_LEN),
                  pl.BlockSpec((1, N_HEADS, NB_TQ, NB_WIN), bias_idx)],
        out_specs=pl.BlockSpec((1, NB_TQ, Q_W), lambda bi, i: (bi, i, 0)),
        out_shape=jax.ShapeDtypeStruct((b, SEQ, Q_W), BF16),
        compiler_params=_cparams(("parallel", "arbitrary")),
        name="attn_neighbourhood",
    )(q, k, v, kc, vc, bias)


def _attn_ctx_kernel(sink_ref, qa_ref, ka_ref, va_ref, qb_ref, kb_ref, vb_ref, qc_ref, kc_ref, vc_ref,
                     oa_ref, ob_ref, oc_ref):
    t = CTX_LEN
    for g in range(N_KV):
        for q_ref, k_ref, v_ref, o_ref, sink in ((qa_ref, ka_ref, va_ref, oa_ref, False),
                                                 (qc_ref, kc_ref, vc_ref, oc_ref, True)):
            qs = _stack_group_q(q_ref[0], g)
            extra = _sink_column(sink_ref, g, t) if sink else None
            o = _softmax_pv([_dot_nt(qs, k_ref[0])], [v_ref[0]], extra)
            chunks = _unstack_group_out(o, g, t)
            for cl in range(2):
                c = 2 * g + cl
                o_ref[0, :, c * LANES:(c + 1) * LANES] = chunks[cl].astype(BF16)
    for c in range(N_HEADS // 2):
        cols = slice(c * LANES, (c + 1) * LANES)
        o = _pair_attention(qb_ref[0, :, cols], lambda h, s_list: s_list,
                            [kb_ref[0, :, cols]], [vb_ref[0, :, cols]])
        ob_ref[0, :, cols] = o.astype(BF16)


def _attn_ctx(sink, qa, ka, va, qb, kb, vb, qc, kc, vc):
    b = qa.shape[0]
    spec = lambda w: pl.BlockSpec((1, CTX_LEN, w), lambda bi: (bi, 0, 0))
    return pl.pallas_call(
        _attn_ctx_kernel,
        grid=(b,),
        in_specs=[pl.BlockSpec(memory_space=pltpu.SMEM),
                  spec(Q_W), spec(KV_W), spec(KV_W), spec(Q_W), spec(Q_W), spec(Q_W),
                  spec(Q_W), spec(KV_W), spec(KV_W)],
        out_specs=[spec(Q_W)] * 3,
        out_shape=[jax.ShapeDtypeStruct((b, CTX_LEN, Q_W), BF16)] * 3,
        compiler_params=_cparams(("parallel",)),
        name="attn_ctx",
    )(sink, qa, ka, va, qb, kb, vb, qc, kc, vc)


def _out_proj_kernel(*refs, final):
    if final:
        ya_ref, yb_ref, yc_ref, sg_ref, x_ref, mod_ref, w_ref, fw_ref, o_ref = refs
    else:
        ya_ref, yb_ref, yc_ref, sg_ref, x_ref, mod_ref, w_ref, o_ref = refs
    acc = None
    for idx, y_ref in enumerate((ya_ref, yb_ref, yc_ref)):
        cols = slice(idx * Q_W, (idx + 1) * Q_W)
        u = (y_ref[0].astype(F32) * sg_ref[0, :, cols].astype(F32)).astype(BF16)
        part = _dot(u, w_ref[cols, :])
        acc = part if acc is None else acc + part
    xn = x_ref[0] + mod_ref[0, 2:3, :] * acc
    if final:
        xn = (xn * lax.rsqrt(jnp.mean(xn * xn, axis=-1, keepdims=True) + EPS)) * fw_ref[...]
    o_ref[0] = xn


def _out_proj(ya, yb, yc, sg, x, mod, w_bf, final_w, tm):
    b, n, _ = x.shape
    final = final_w is not None
    mod_idx = (lambda bi, i: (bi, 0, 0)) if mod.shape[0] > 1 else (lambda bi, i: (0, 0, 0))
    row = lambda w: pl.BlockSpec((1, tm, w), lambda bi, i: (bi, i, 0))
    in_specs = [row(Q_W), row(Q_W), row(Q_W), row(MIX_WIDTH), row(D_MODEL),
                pl.BlockSpec((1, 3, D_MODEL), mod_idx),
                pl.BlockSpec((MIX_WIDTH, D_MODEL), lambda bi, i: (0, 0))]
    args = [ya, yb, yc, sg, x, mod, w_bf]
    if final:
        in_specs.append(pl.BlockSpec((1, D_MODEL), lambda bi, i: (0, 0)))
        args.append(final_w)
    return pl.pallas_call(
        functools.partial(_out_proj_kernel, final=final),
        grid=(b, n // tm),
        in_specs=in_specs,
        out_specs=row(D_MODEL),
        out_shape=jax.ShapeDtypeStruct((b, n, D_MODEL), F32),
        compiler_params=_cparams(("parallel", "parallel")),
        name="out_proj_final" if final else "out_proj",
    )(*args)


def _rope_tables():
    t = jnp.arange(SEQ, dtype=jnp.int32)
    rows = (t // GRID_W).astype(F32)
    cols = (t % GRID_W).astype(F32)
    n_freq = HEAD_DIM // 4
    freq = ROPE_THETA ** (-jnp.arange(n_freq, dtype=F32) / n_freq)
    ang = jnp.concatenate([rows[:, None] * freq, cols[:, None] * freq], axis=-1)
    cos, sin = jnp.cos(ang), jnp.sin(ang)
    cos_t = jnp.tile(jnp.concatenate([cos, cos], axis=-1), (1, LANES // HEAD_DIM))
    sin_t = jnp.tile(jnp.concatenate([-sin, sin], axis=-1), (1, LANES // HEAD_DIM))
    return cos_t, sin_t


def kernel(x, c, ctx, c_ctx, norm_w, ada_w, ada_b, w_in, w_out, q_norm_a, k_norm_a, rpb_b, sink_c,
           final_norm_w):
    bsz = x.shape[0]
    cvecs = jnp.concatenate([c, c_ctx[None], jnp.zeros((8 - bsz - 1, D_MODEL), F32)], axis=0)
    mod = _modulation(cvecs, ada_w, ada_b).reshape(DEPTH, 8, 3, D_MODEL)
    bias = _nb_bias(rpb_b)
    rope_tabs = _rope_tables()
    w_in_bf = w_in.astype(BF16)
    w_out_bf = w_out.astype(BF16)
    cx = ctx
    for l in range(DEPTH):
        need_ctx = l < DEPTH - 1
        mod_x, mod_c = mod[l, :bsz], mod[l, bsz:bsz + 1]
        nw = norm_w[l][None]
        qn_t = jnp.tile(q_norm_a[l], N_HEADS)[None]
        kn_t = jnp.tile(k_norm_a[l], N_KV)[None]
        qa, ka, va, qb, kb, vb, qc, kc, vc, sg = _in_proj(
            x, mod_x, nw, w_in_bf[l], qn_t, kn_t, rope_tabs, tm=512)
        qa_c, ka_c, va_c, qb_c, kb_c, vb_c, qc_c, kc_c, vc_c, sg_c = _in_proj(
            cx, mod_c, nw, w_in_bf[l], qn_t, kn_t, None, tm=CTX_LEN)
        ya = _attn_a(qa, ka, va, ka_c, va_c)
        yb = _attn_b(qb, kb, vb, kb_c, vb_c, bias[l])
        yc = _attn_c(sink_c[l], qc, kc, vc, kc_c, vc_c)
        if need_ctx:
            ya_c, yb_c, yc_c = _attn_ctx(sink_c[l], qa_c, ka_c, va_c, qb_c, kb_c, vb_c, qc_c, kc_c, vc_c)
            cx = _out_proj(ya_c, yb_c, yc_c, sg_c, cx, mod_c, w_out_bf[l], None, tm=CTX_LEN)
        x = _out_proj(ya, yb, yc, sg, x, mod_x, w_out_bf[l],
                      None if need_ctx else final_norm_w[None], tm=512)
    return x
```

```python
import functools

import jax
import jax.numpy as jnp
from jax import lax
from jax.experimental import pallas as pl
from jax.experimental.pallas import tpu as pltpu

D_MODEL = 1024
SEQ = 8192
DEPTH = 2
GRID_W = 64
GRID_H = SEQ // GRID_W
CTX_LEN = 256
HEAD_DIM = 64
N_HEADS = 8
N_KV = 2
GROUP = N_HEADS // N_KV
Q_W = N_HEADS * HEAD_DIM
KV_W = N_KV * HEAD_DIM
MIX_WIDTH = 3 * Q_W
IN_WIDTH = 3 * Q_W + 2 * KV_W + 2 * Q_W + 2 * KV_W + MIX_WIDTH
NB_ROWS = 8
NB_COLS = 16
WINDOW = 128
ROPE_THETA = 10000.0
EPS = 1e-6
SCALE = HEAD_DIM ** -0.5
LOG2E = 1.4426950408889634
NEG = -1e30

LANES = 128
VMEM_LIMIT = 56 * 1024 * 1024

OFF_AQ, OFF_AK, OFF_AV = 0, 512, 640
OFF_BQ, OFF_BK, OFF_BV = 768, 1280, 1792
OFF_CQ, OFF_CK, OFF_CV = 2304, 2816, 2944
OFF_G = 3072

NB_TQ = 2 * GRID_W
NB_WIN_ROWS = 10
NB_WIN = NB_WIN_ROWS * GRID_W
NB_CASE_R0 = (0, 2, 4, GRID_H - 4, GRID_H - 2)
N_REL = (2 * NB_ROWS - 1) * (2 * NB_COLS - 1)

F32 = jnp.float32
BF16 = jnp.bfloat16


def _cparams(sem):
    return pltpu.CompilerParams(dimension_semantics=sem, vmem_limit_bytes=VMEM_LIMIT)


def _dot(a, b):
    return jnp.dot(a, b, preferred_element_type=F32)


def _dot_nt(a, b):
    return lax.dot_general(a, b, (((1,), (1,)), ((), ())), preferred_element_type=F32)


def _transpose_bf16(z):
    return z.astype(F32).T.astype(BF16)


def _silu(x):
    return x * (1.0 / (1.0 + jnp.exp(-x)))


def _mod_kernel(c_ref, w_ref, b_ref, o_ref):
    cs = _silu(c_ref[...])
    o_ref[0] = jnp.dot(cs, w_ref[0], preferred_element_type=F32,
                       precision=lax.Precision.HIGHEST) + b_ref[0]


def _modulation(cvecs, ada_w, ada_b):
    tn = 1024
    return pl.pallas_call(
        _mod_kernel,
        grid=(DEPTH, 3 * D_MODEL // tn),
        in_specs=[
            pl.BlockSpec((8, D_MODEL), lambda l, j: (0, 0)),
            pl.BlockSpec((1, D_MODEL, tn), lambda l, j: (l, 0, j)),
            pl.BlockSpec((1, 1, tn), lambda l, j: (l, 0, j)),
        ],
        out_specs=pl.BlockSpec((1, 8, tn), lambda l, j: (l, 0, j)),
        out_shape=jax.ShapeDtypeStruct((DEPTH, 8, 3 * D_MODEL), F32),
        compiler_params=_cparams(("arbitrary", "arbitrary")),
        name="adaln_mod",
    )(cvecs, ada_w, ada_b.reshape(DEPTH, 1, 3 * D_MODEL))


def _head_rms(z, wn):
    w = z.shape[1]
    r = lax.broadcasted_iota(jnp.int32, (w, w), 0) // HEAD_DIM
    c = lax.broadcasted_iota(jnp.int32, (w, w), 1) // HEAD_DIM
    ones = (r == c).astype(BF16)
    z2 = z * z
    hi = z2.astype(BF16)
    lo = (z2 - hi.astype(F32)).astype(BF16)
    ss = _dot(hi, ones) + _dot(lo, ones)
    return (z * lax.rsqrt(ss * (1.0 / HEAD_DIM) + EPS)) * wn


def _rope(z, cos_t, sin_t):
    w = z.shape[1]
    reps = w // LANES
    if reps > 1:
        cos_t = jnp.concatenate([cos_t] * reps, axis=1)
        sin_t = jnp.concatenate([sin_t] * reps, axis=1)
    lane = lax.broadcasted_iota(jnp.int32, z.shape, 1)
    partner = jnp.where((lane & (HEAD_DIM // 2)) == 0,
                        pltpu.roll(z, w - HEAD_DIM // 2, 1), pltpu.roll(z, HEAD_DIM // 2, 1))
    return z * cos_t + partner * sin_t


def _in_proj_kernel(*refs, rope):
    if rope:
        (x_ref, mod_ref, nw_ref, w_ref, qn_ref, kn_ref, cos_ref, sin_ref,
         qa_ref, ka_ref, va_ref, qb_ref, kb_ref, vb_ref, qc_ref, kc_ref, vc_ref, sg_ref) = refs
        cos_t, sin_t = cos_ref[...], sin_ref[...]
        rot = lambda z: _rope(z, cos_t, sin_t)
    else:
        (x_ref, mod_ref, nw_ref, w_ref, qn_ref, kn_ref,
         qa_ref, ka_ref, va_ref, qb_ref, kb_ref, vb_ref, qc_ref, kc_ref, vc_ref, sg_ref) = refs
        rot = lambda z: z
    x = x_ref[0]
    y = (x * lax.rsqrt(jnp.mean(x * x, axis=-1, keepdims=True) + EPS)) * nw_ref[...]
    h = (y * (1.0 + mod_ref[0, 1:2, :]) + mod_ref[0, 0:1, :]).astype(BF16)

    def proj(off, width):
        return _dot(h, w_ref[:, off:off + width])

    q_scale = SCALE * LOG2E if rope else SCALE
    qa_ref[0] = (rot(_head_rms(proj(OFF_AQ, Q_W), qn_ref[...])) * q_scale).astype(BF16)
    ka_ref[0] = rot(_head_rms(proj(OFF_AK, KV_W), kn_ref[...])).astype(BF16)
    va_ref[0] = proj(OFF_AV, KV_W).astype(BF16)
    qb_ref[0] = (proj(OFF_BQ, Q_W) * q_scale).astype(BF16)
    kb_ref[0] = proj(OFF_BK, Q_W).astype(BF16)
    vb_ref[0] = proj(OFF_BV, Q_W).astype(BF16)
    qc_ref[0] = (rot(proj(OFF_CQ, Q_W)) * q_scale).astype(BF16)
    kc_ref[0] = rot(proj(OFF_CK, KV_W)).astype(BF16)
    vc_ref[0] = proj(OFF_CV, KV_W).astype(BF16)
    sg_ref[0] = _silu(proj(OFF_G, MIX_WIDTH)).astype(BF16)


def _in_proj(x, mod, nw, w_bf, qn_t, kn_t, rope_tabs, tm):
    b, n, _ = x.shape
    rope = rope_tabs is not None
    mod_idx = (lambda bi, i: (bi, 0, 0)) if mod.shape[0] > 1 else (lambda bi, i: (0, 0, 0))
    const2 = lambda bi, i: (0, 0)
    in_specs = [
        pl.BlockSpec((1, tm, D_MODEL), lambda bi, i: (bi, i, 0)),
        pl.BlockSpec((1, 3, D_MODEL), mod_idx),
        pl.BlockSpec((1, D_MODEL), const2),
        pl.BlockSpec((D_MODEL, IN_WIDTH), const2, pipeline_mode=pl.Buffered(1)),
        pl.BlockSpec((1, Q_W), const2),
        pl.BlockSpec((1, KV_W), const2),
    ]
    args = [x, mod, nw, w_bf, qn_t, kn_t]
    if rope:
        in_specs += [pl.BlockSpec((tm, LANES), lambda bi, i: (i, 0))] * 2
        args += list(rope_tabs)
    widths = (Q_W, KV_W, KV_W, Q_W, Q_W, Q_W, Q_W, KV_W, KV_W, MIX_WIDTH)
    out_specs = [pl.BlockSpec((1, tm, w), lambda bi, i: (bi, i, 0)) for w in widths]
    out_shape = [jax.ShapeDtypeStruct((b, n, w), BF16) for w in widths]
    return pl.pallas_call(
        functools.partial(_in_proj_kernel, rope=rope),
        grid=(b, n // tm),
        in_specs=in_specs,
        out_specs=out_specs,
        out_shape=out_shape,
        compiler_params=_cparams(("parallel", "parallel")),
        name="in_proj_rope" if rope else "in_proj_ctx",
    )(*args)


def _stack_group_q(q, g):
    t = q.shape[0]
    lane = lax.broadcasted_iota(jnp.int32, (t, LANES), 1)
    keep = (lane // HEAD_DIM) == g
    parts = []
    for j in range(GROUP):
        h = GROUP * g + j
        chunk = q[:, (h // 2) * LANES:(h // 2 + 1) * LANES].astype(F32)
        if h % 2 != g:
            chunk = pltpu.roll(chunk, HEAD_DIM, 1)
        parts.append(jnp.where(keep, chunk, 0.0).astype(BF16))
    return jnp.concatenate(parts, axis=0)


def _unstack_group_out(o, g, t):
    lane = lax.broadcasted_iota(jnp.int32, (t, LANES), 1)
    left = lane < HEAD_DIM
    chunks = []
    for cl in range(2):
        even = o[(2 * cl) * t:(2 * cl + 1) * t]
        odd = o[(2 * cl + 1) * t:(2 * cl + 2) * t]
        if g == 0:
            odd = pltpu.roll(odd, HEAD_DIM, 1)
        else:
            even = pltpu.roll(even, HEAD_DIM, 1)
        chunks.append(jnp.where(left, even, odd))
    return chunks


def _softmax_pv(s_list, v_list, extra_logit=None):
    m = s_list[0].max(axis=1, keepdims=True)
    for s in s_list[1:]:
        m = jnp.maximum(m, s.max(axis=1, keepdims=True))
    if extra_logit is not None:
        m = jnp.maximum(m, extra_logit)
    l = None
    o = None
    for s, v in zip(s_list, v_list):
        p = jnp.exp(s - m)
        ps = p.sum(axis=1, keepdims=True)
        po = _dot(p.astype(BF16), v)
        l = ps if l is None else l + ps
        o = po if o is None else o + po
    if extra_logit is not None:
        l = l + jnp.exp(extra_logit - m)
    return o / l


def _sink_column(sink_ref, g, t):
    row = lax.broadcasted_iota(jnp.int32, (GROUP * t, 1), 0) // t
    col = jnp.full((GROUP * t, 1), sink_ref[GROUP * g], F32)
    for j in range(1, GROUP):
        col = jnp.where(row == j, sink_ref[GROUP * g + j], col)
    return col


ONES_ROWS = 16


def _attn_a_kernel(q_ref, k_ref, v_ref, kc_ref, vc_ref, o_ref, vt_ref, vct_ref, qs_ref, m_ref, acc_ref,
                   s0_ref, s1_ref, s2_ref, mx0_ref, mx1_ref, mx2_ref, *, tq, tk, col_blk):
    n_chunks = SEQ // tk

    @pl.when(pl.program_id(1) == 0)
    def _():
        for j in range(SEQ // tk):
            vt_ref[j] = _transpose_bf16(v_ref[0, j * tk:(j + 1) * tk, :])
        vct_ref[...] = _transpose_bf16(vc_ref[0])

    col_blocks = [slice(c0, c0 + col_blk) for c0 in range(0, GROUP * tq, col_blk)]

    def score_block(k, s_ref, mx_ref, cols):
        s = _dot(k, qs_ref[:, cols])
        s_ref[0:k.shape[0], cols] = s
        mx_ref[:, cols] = s.max(axis=0, keepdims=True)

    def absorb_block(n, s_ref, mx_ref, v_ext, cols):
        m_prev = m_ref[:, cols]
        m_new = jnp.maximum(m_prev, mx_ref[:, cols])
        alpha = jnp.exp2(m_prev - m_new)
        p = jnp.exp2(s_ref[0:n, cols] - m_new).astype(BF16)
        acc_ref[:, cols] = alpha * acc_ref[:, cols] + _dot(v_ext, p)
        m_ref[:, cols] = m_new

    def with_ones(vt_g):
        return jnp.concatenate([vt_g, jnp.ones((ONES_ROWS, vt_g.shape[1]), BF16)], axis=0)

    def latent_keys(j):
        return k_ref[0, pl.ds(pl.multiple_of(j * tk, tk), tk), :]

    qt = _transpose_bf16(q_ref[0])
    for g in range(N_KV):
        rows = slice(g * HEAD_DIM, (g + 1) * HEAD_DIM)
        qs_ref[...] = jnp.zeros(qs_ref.shape, BF16)
        for j in range(GROUP):
            h = GROUP * g + j
            qs_ref[rows, j * tq:(j + 1) * tq] = qt[h * HEAD_DIM:(h + 1) * HEAD_DIM, :]
        m_ref[...] = jnp.full(m_ref.shape, NEG, F32)
        acc_ref[...] = jnp.zeros(acc_ref.shape, F32)

        bufs = ((s0_ref, mx0_ref), (s1_ref, mx1_ref), (s2_ref, mx2_ref))

        def trip(n_cur, cur, vt_g, k_next, nxt):
            v_ext = with_ones(vt_g)
            for cols in col_blocks:
                if k_next is not None:
                    score_block(k_next, *bufs[nxt], cols)
                absorb_block(n_cur, *bufs[cur], v_ext, cols)

        kc = kc_ref[0]
        for cols in col_blocks:
            score_block(kc, *bufs[2], cols)
        trip(CTX_LEN, 2, vct_ref[rows, :], latent_keys(0), 0)

        def body(i, carry):
            j = 3 * i
            trip(tk, 0, vt_ref[j, rows, :], latent_keys(j + 1), 1)
            trip(tk, 1, vt_ref[j + 1, rows, :], latent_keys(j + 2), 2)
            trip(tk, 2, vt_ref[j + 2, rows, :], latent_keys(j + 3), 0)
            return carry

        assert n_chunks % 3 == 1
        lax.fori_loop(0, n_chunks // 3, body, 0)
        trip(tk, 0, vt_ref[n_chunks - 1, rows, :], None, None)
        o = acc_ref[0:HEAD_DIM, :] / acc_ref[HEAD_DIM:HEAD_DIM + 1, :]
        for cl in range(GROUP // 2):
            pair = jnp.concatenate([o[:, (2 * cl) * tq:(2 * cl + 1) * tq],
                                    o[:, (2 * cl + 1) * tq:(2 * cl + 2) * tq]], axis=0)
            c = (GROUP // 2) * g + cl
            o_ref[0, :, c * LANES:(c + 1) * LANES] = _transpose_bf16(pair)


def _attn_a(q, k, v, kc, vc, tq=512, tk=512, col_blk=256):
    b = q.shape[0]
    res = lambda n: pl.BlockSpec((1, n, KV_W), lambda bi, i: (bi, 0, 0))
    return pl.pallas_call(
        functools.partial(_attn_a_kernel, tq=tq, tk=tk, col_blk=col_blk),
        grid=(b, SEQ // tq),
        in_specs=[pl.BlockSpec((1, tq, Q_W), lambda bi, i: (bi, i, 0)),
                  res(SEQ), res(SEQ), res(CTX_LEN), res(CTX_LEN)],
        out_specs=pl.BlockSpec((1, tq, Q_W), lambda bi, i: (bi, i, 0)),
        out_shape=jax.ShapeDtypeStruct((b, SEQ, Q_W), BF16),
        scratch_shapes=[pltpu.VMEM((SEQ // tk, KV_W, tk), BF16),
                        pltpu.VMEM((KV_W, CTX_LEN), BF16),
                        pltpu.VMEM((LANES, GROUP * tq), BF16),
                        pltpu.VMEM((1, GROUP * tq), F32),
                        pltpu.VMEM((HEAD_DIM + ONES_ROWS, GROUP * tq), F32),
                        pltpu.VMEM((tk, GROUP * tq), F32),
                        pltpu.VMEM((tk, GROUP * tq), F32),
                        pltpu.VMEM((tk, GROUP * tq), F32),
                        pltpu.VMEM((1, GROUP * tq), F32),
                        pltpu.VMEM((1, GROUP * tq), F32),
                        pltpu.VMEM((1, GROUP * tq), F32)],
        compiler_params=_cparams(("parallel", "arbitrary")),
        name="attn_global",
    )(q, k, v, kc, vc)


def _fill_feature_major(v_ref, vt_ref, blk):
    def fill(j, carry):
        off = pl.multiple_of(j * blk, blk)
        vt_ref[j] = _transpose_bf16(v_ref[0, pl.ds(off, blk), :])
        return carry

    lax.fori_loop(0, v_ref.shape[1] // blk, fill, 0)


def _with_ones_rows(vt):
    return jnp.concatenate([vt, jnp.ones((ONES_ROWS, vt.shape[1]), BF16)], axis=0)


def _attn_c_kernel(sink_ref, q_ref, k_ref, v_ref, kc_ref, vc_ref, o_ref, vt_ref, vct_ref,
                   qs0_ref, qs1_ref, s0_ref, s1_ref, mx0_ref, mx1_ref, *, tq):
    win = tq + 2 * WINDOW
    n_keys = win + CTX_LEN
    i = pl.program_id(1)

    @pl.when(i == 0)
    def _():
        _fill_feature_major(v_ref, vt_ref, LANES)
        vct_ref[...] = _transpose_bf16(vc_ref[0])

    ws = pl.multiple_of(jnp.clip(i * tq - WINDOW, 0, SEQ - win), LANES)
    blk0 = ws // LANES
    kw = k_ref[0, pl.ds(ws, win), :]
    kc = kc_ref[0]
    kpos = ws + lax.broadcasted_iota(jnp.int32, (win, tq), 0)
    qpos = i * tq + lax.broadcasted_iota(jnp.int32, (win, tq), 1)
    valid = jnp.abs(qpos - kpos) <= WINDOW
    qt = _transpose_bf16(q_ref[0])
    bufs = ((qs0_ref, s0_ref, mx0_ref), (qs1_ref, s1_ref, mx1_ref))
    col_blocks = [slice(j * tq, (j + 1) * tq) for j in range(GROUP)]

    def stack_queries(g):
        qs_ref = bufs[g][0]
        qs_ref[...] = jnp.zeros(qs_ref.shape, BF16)
        for j in range(GROUP):
            h = GROUP * g + j
            qs_ref[g * HEAD_DIM:(g + 1) * HEAD_DIM, j * tq:(j + 1) * tq] = (
                qt[h * HEAD_DIM:(h + 1) * HEAD_DIM, :])

    def score_block(g, j):
        qs_ref, s_ref, mx_ref = bufs[g]
        cols = col_blocks[j]
        s_win = jnp.where(valid, _dot(kw, qs_ref[:, cols]), NEG)
        s_ctx = _dot(kc, qs_ref[:, cols])
        s_ref[0:win, cols] = s_win
        s_ref[win:n_keys, cols] = s_ctx
        sink2 = sink_ref[GROUP * g + j] * LOG2E
        mx_ref[:, cols] = jnp.maximum(
            jnp.maximum(s_win.max(axis=0, keepdims=True), s_ctx.max(axis=0, keepdims=True)), sink2)

    def values(g):
        rows = slice(g * HEAD_DIM, (g + 1) * HEAD_DIM)
        blocks = [vt_ref[blk0 + t, rows, :] for t in range(win // LANES)] + [vct_ref[rows, :]]
        return _with_ones_rows(jnp.concatenate(blocks, axis=1))

    def out_block(g, j, v_ext):
        _, s_ref, mx_ref = bufs[g]
        cols = col_blocks[j]
        m = mx_ref[:, cols]
        p = jnp.exp2(s_ref[:, cols] - m).astype(BF16)
        acc = _dot(v_ext, p)
        l = acc[HEAD_DIM:HEAD_DIM + 1, :] + jnp.exp2(sink_ref[GROUP * g + j] * LOG2E - m)
        return acc[0:HEAD_DIM, :] / l

    def store_pair(g, cl, o_even, o_odd):
        c = (GROUP // 2) * g + cl
        o_ref[0, :, c * LANES:(c + 1) * LANES] = _transpose_bf16(
            jnp.concatenate([o_even, o_odd], axis=0))

    stack_queries(0)
    stack_queries(1)
    for j in range(GROUP):
        score_block(0, j)
    v_ext = values(0)
    outs = []
    for j in range(GROUP):
        score_block(1, j)
        outs.append(out_block(0, j, v_ext))
        if j % 2 == 1:
            store_pair(0, j // 2, outs[j - 1], outs[j])
    v_ext = values(1)
    outs = []
    for j in range(GROUP):
        outs.append(out_block(1, j, v_ext))
        if j % 2 == 1:
            store_pair(1, j // 2, outs[j - 1], outs[j])


def _attn_c(sink, q, k, v, kc, vc, tq=256):
    b = q.shape[0]
    n_keys = tq + 2 * WINDOW + CTX_LEN
    res = lambda n: pl.BlockSpec((1, n, KV_W), lambda bi, i: (bi, 0, 0))
    return pl.pallas_call(
        functools.partial(_attn_c_kernel, tq=tq),
        grid=(b, SEQ // tq),
        in_specs=[pl.BlockSpec(memory_space=pltpu.SMEM),
                  pl.BlockSpec((1, tq, Q_W), lambda bi, i: (bi, i, 0)),
                  res(SEQ), res(SEQ), res(CTX_LEN), res(CTX_LEN)],
        out_specs=pl.BlockSpec((1, tq, Q_W), lambda bi, i: (bi, i, 0)),
        out_shape=jax.ShapeDtypeStruct((b, SEQ, Q_W), BF16),
        scratch_shapes=[pltpu.VMEM((SEQ // LANES, KV_W, LANES), BF16),
                        pltpu.VMEM((KV_W, CTX_LEN), BF16),
                        pltpu.VMEM((LANES, GROUP * tq), BF16),
                        pltpu.VMEM((LANES, GROUP * tq), BF16),
                        pltpu.VMEM((n_keys, GROUP * tq), F32),
                        pltpu.VMEM((n_keys, GROUP * tq), F32),
                        pltpu.VMEM((1, GROUP * tq), F32),
                        pltpu.VMEM((1, GROUP * tq), F32)],
        compiler_params=_cparams(("parallel", "arbitrary")),
        name="attn_window",
    )(sink, q, k, v, kc, vc)


def _nb_window_row(r0):
    return min(max(r0 - NB_ROWS // 2, 0), GRID_H - NB_WIN_ROWS)


def _nb_bias_kernel(rpb_ref, o_ref, tt_ref):
    ck = lax.broadcasted_iota(jnp.int32, (GRID_W, LANES), 0)
    lane = lax.broadcasted_iota(jnp.int32, (GRID_W, LANES), 1)
    cq = lane & (GRID_W - 1)
    d = ck - cq + (NB_COLS - 1)
    cs = jnp.clip(cq - NB_COLS // 2, 0, GRID_W - NB_COLS)
    col_ok = (ck >= cs) & (ck < cs + NB_COLS)
    n_dc = 2 * NB_COLS - 1
    neg_tile = jnp.full((GRID_W, LANES), NEG, F32)
    first_row = lane < GRID_W
    for h in range(2):
        base = ((pl.program_id(0) * (N_HEADS // 2) + pl.program_id(1)) * 2 + h) * N_REL
        for a in range(2 * NB_ROWS - 1):
            t = neg_tile
            for dd in range(n_dc):
                t = jnp.where(d == dd, rpb_ref[base + a * n_dc + dd] * LOG2E, t)
            tt_ref[a] = jnp.where(col_ok, t, NEG)
        for case, r0 in enumerate(NB_CASE_R0):
            w0 = _nb_window_row(r0)
            for kr in range(NB_WIN_ROWS):
                rk = w0 + kr
                halves = []
                for qr in range(2):
                    rq = r0 + qr
                    rs = min(max(rq - NB_ROWS // 2, 0), GRID_H - NB_ROWS)
                    halves.append(tt_ref[rk - rq + NB_ROWS - 1] if rs <= rk < rs + NB_ROWS else neg_tile)
                o_ref[0, case, 0, kr * GRID_W:(kr + 1) * GRID_W, h * LANES:(h + 1) * LANES] = (
                    jnp.where(first_row, halves[0], halves[1]))


def _nb_bias(rpb):
    n_case = len(NB_CASE_R0)
    return pl.pallas_call(
        _nb_bias_kernel,
        grid=(DEPTH, N_HEADS // 2),
        in_specs=[pl.BlockSpec(memory_space=pltpu.SMEM)],
        out_specs=pl.BlockSpec((1, n_case, 1, NB_WIN, 2 * NB_TQ), lambda l, c: (l, 0, c, 0, 0)),
        out_shape=jax.ShapeDtypeStruct((DEPTH, n_case, N_HEADS // 2, NB_WIN, 2 * NB_TQ), F32),
        scratch_shapes=[pltpu.VMEM((2 * NB_ROWS - 1, GRID_W, LANES), F32)],
        compiler_params=_cparams(("arbitrary", "arbitrary")),
        name="nb_bias",
    )(rpb.reshape(-1))


def _pair_attention(qp, s_extra_fn, k_list, v_list):
    t = qp.shape[0]
    lane = lax.broadcasted_iota(jnp.int32, (t, LANES), 1)
    qf = qp.astype(F32)
    outs = []
    for h in range(2):
        qpad = jnp.where((lane // HEAD_DIM) == h, qf, 0.0).astype(BF16)
        s_list = [_dot_nt(qpad, k) for k in k_list]
        s_list = s_extra_fn(h, s_list)
        outs.append(_softmax_pv(s_list, v_list))
    return jnp.where(lane < HEAD_DIM, outs[0], outs[1])


def _attn_b_kernel(q_ref, k_ref, v_ref, kc_ref, vc_ref, bias_ref, o_ref, vt_ref, vct_ref,
                   s0_ref, s1_ref, mx0_ref, mx1_ref):
    i = pl.program_id(1)

    @pl.when(i == 0)
    def _():
        _fill_feature_major(v_ref, vt_ref, NB_TQ)
        vct_ref[...] = _transpose_bf16(vc_ref[0])

    w0 = jnp.clip(2 * i - NB_ROWS // 2, 0, GRID_H - NB_WIN_ROWS)
    off = pl.multiple_of(w0 * GRID_W, LANES)
    blk0 = w0 // 2
    bufs = ((s0_ref, mx0_ref), (s1_ref, mx1_ref))
    zeros = jnp.zeros((NB_TQ, LANES), BF16)
    head0_lanes = lax.broadcasted_iota(jnp.int32, (NB_TQ, LANES), 1) < HEAD_DIM

    def score_stage(c):
        s_ref, mx_ref = bufs[c % 2]
        cols = slice(c * LANES, (c + 1) * LANES)
        qp = q_ref[0, :, cols]
        qpad = jnp.concatenate([jnp.where(head0_lanes, qp, zeros),
                                jnp.where(head0_lanes, zeros, qp)], axis=0)
        s_win = _dot_nt(k_ref[0, pl.ds(off, NB_WIN), cols], qpad) + bias_ref[0, c]
        s_ctx = _dot_nt(kc_ref[0, :, cols], qpad)
        s_ref[0:NB_WIN, :] = s_win
        s_ref[NB_WIN:NB_WIN + CTX_LEN, :] = s_ctx
        mx_ref[...] = jnp.maximum(s_win.max(axis=0, keepdims=True), s_ctx.max(axis=0, keepdims=True))

    def out_stage(c):
        s_ref, mx_ref = bufs[c % 2]
        cols = slice(c * LANES, (c + 1) * LANES)
        p = jnp.exp2(s_ref[...] - mx_ref[...]).astype(BF16)
        blocks = [vt_ref[blk0 + t, cols, :] for t in range(NB_WIN // NB_TQ)] + [vct_ref[cols, :]]
        acc = _dot(_with_ones_rows(jnp.concatenate(blocks, axis=1)), p)
        l = acc[2 * HEAD_DIM:2 * HEAD_DIM + 1, :]
        o0 = acc[0:HEAD_DIM, 0:NB_TQ] / l[:, 0:NB_TQ]
        o1 = acc[HEAD_DIM:2 * HEAD_DIM, NB_TQ:2 * NB_TQ] / l[:, NB_TQ:2 * NB_TQ]
        o_ref[0, :, cols] = _transpose_bf16(jnp.concatenate([o0, o1], axis=0))

    n_pairs = N_HEADS // 2
    score_stage(0)
    for c in range(n_pairs):
        if c + 1 < n_pairs:
            score_stage(c + 1)
        out_stage(c)


def _attn_b(q, k, v, kc, vc, bias):
    b = q.shape[0]
    n_tiles = SEQ // NB_TQ
    res = lambda n: pl.BlockSpec((1, n, Q_W), lambda bi, i: (bi, 0, 0), pipeline_mode=pl.Buffered(1))

    def bias_idx(bi, i):
        return (jnp.minimum(i, 2) + jnp.maximum(i - (n_tiles - 3), 0), 0, 0, 0)

    return pl.pallas_call(
        _attn_b_kernel,
        grid=(b, n_tiles),
        in_specs=[pl.BlockSpec((1, NB_TQ, Q_W), lambda bi, i: (bi, i, 0)),
                  res(SEQ), res(SEQ), res(CTX_LEN), res(CTX_LEN),
                  pl.BlockSpec((1, N_HEADS // 2, NB_WIN, 2 * NB_TQ), bias_idx)],
        out_specs=pl.BlockSpec((1, NB_TQ, Q_W), lambda bi, i: (bi, i, 0)),
        out_shape=jax.ShapeDtypeStruct((b, SEQ, Q_W), BF16),
        scratch_shapes=[pltpu.VMEM((n_tiles, Q_W, NB_TQ), BF16),
                        pltpu.VMEM((Q_W, CTX_LEN), BF16),
                        pltpu.VMEM((NB_WIN + CTX_LEN, 2 * NB_TQ), F32),
                        pltpu.VMEM((NB_WIN + CTX_LEN, 2 * NB_TQ), F32),
                        pltpu.VMEM((1, 2 * NB_TQ), F32),
                        pltpu.VMEM((1, 2 * NB_TQ), F32)],
        compiler_params=_cparams(("parallel", "arbitrary")),
        name="attn_neighbourhood",
    )(q, k, v, kc, vc, bias)


def _attn_ctx_kernel(sink_ref, qa_ref, ka_ref, va_ref, qb_ref, kb_ref, vb_ref, qc_ref, kc_ref, vc_ref,
                     oa_ref, ob_ref, oc_ref):
    t = CTX_LEN
    for g in range(N_KV):
        for q_ref, k_ref, v_ref, o_ref, sink in ((qa_ref, ka_ref, va_ref, oa_ref, False),
                                                 (qc_ref, kc_ref, vc_ref, oc_ref, True)):
            qs = _stack_group_q(q_ref[0], g)
            extra = _sink_column(sink_ref, g, t) if sink else None
            o = _softmax_pv([_dot_nt(qs, k_ref[0])], [v_ref[0]], extra)
            chunks = _unstack_group_out(o, g, t)
            for cl in range(2):
                c = 2 * g + cl
                o_ref[0, :, c * LANES:(c + 1) * LANES] = chunks[cl].astype(BF16)
    for c in range(N_HEADS // 2):
        cols = slice(c * LANES, (c + 1) * LANES)
        o = _pair_attention(qb_ref[0, :, cols], lambda h, s_list: s_list,
                            [kb_ref[0, :, cols]], [vb_ref[0, :, cols]])
        ob_ref[0, :, cols] = o.astype(BF16)


def _attn_ctx(sink, qa, ka, va, qb, kb, vb, qc, kc, vc):
    b = qa.shape[0]
    spec = lambda w: pl.BlockSpec((1, CTX_LEN, w), lambda bi: (bi, 0, 0))
    return pl.pallas_call(
        _attn_ctx_kernel,
        grid=(b,),
        in_specs=[pl.BlockSpec(memory_space=pltpu.SMEM),
                  spec(Q_W), spec(KV_W), spec(KV_W), spec(Q_W), spec(Q_W), spec(Q_W),
                  spec(Q_W), spec(KV_W), spec(KV_W)],
        out_specs=[spec(Q_W)] * 3,
        out_shape=[jax.ShapeDtypeStruct((b, CTX_LEN, Q_W), BF16)] * 3,
        compiler_params=_cparams(("parallel",)),
        name="attn_ctx",
    )(sink, qa, ka, va, qb, kb, vb, qc, kc, vc)


def _out_proj_kernel(*refs, final):
    if final:
        ya_ref, yb_ref, yc_ref, sg_ref, x_ref, mod_ref, w_ref, fw_ref, o_ref = refs
    else:
        ya_ref, yb_ref, yc_ref, sg_ref, x_ref, mod_ref, w_ref, o_ref = refs
    acc = None
    for idx, y_ref in enumerate((ya_ref, yb_ref, yc_ref)):
        cols = slice(idx * Q_W, (idx + 1) * Q_W)
        u = (y_ref[0].astype(F32) * sg_ref[0, :, cols].astype(F32)).astype(BF16)
        part = _dot(u, w_ref[cols, :])
        acc = part if acc is None else acc + part
    xn = x_ref[0] + mod_ref[0, 2:3, :] * acc
    if final:
        xn = (xn * lax.rsqrt(jnp.mean(xn * xn, axis=-1, keepdims=True) + EPS)) * fw_ref[...]
    o_ref[0] = xn


def _out_proj(ya, yb, yc, sg, x, mod, w_bf, final_w, tm):
    b, n, _ = x.shape
    final = final_w is not None
    mod_idx = (lambda bi, i: (bi, 0, 0)) if mod.shape[0] > 1 else (lambda bi, i: (0, 0, 0))
    row = lambda w: pl.BlockSpec((1, tm, w), lambda bi, i: (bi, i, 0))
    in_specs = [row(Q_W), row(Q_W), row(Q_W), row(MIX_WIDTH), row(D_MODEL),
                pl.BlockSpec((1, 3, D_MODEL), mod_idx),
                pl.BlockSpec((MIX_WIDTH, D_MODEL), lambda bi, i: (0, 0))]
    args = [ya, yb, yc, sg, x, mod, w_bf]
    if final:
        in_specs.append(pl.BlockSpec((1, D_MODEL), lambda bi, i: (0, 0)))
        args.append(final_w)
    return pl.pallas_call(
        functools.partial(_out_proj_kernel, final=final),
        grid=(b, n // tm),
        in_specs=in_specs,
        out_specs=row(D_MODEL),
        out_shape=jax.ShapeDtypeStruct((b, n, D_MODEL), F32),
        compiler_params=_cparams(("parallel", "parallel")),
        name="out_proj_final" if final else "out_proj",
    )(*args)


def _rope_tables():
    t = jnp.arange(SEQ, dtype=jnp.int32)
    rows = (t // GRID_W).astype(F32)
    cols = (t % GRID_W).astype(F32)
    n_freq = HEAD_DIM // 4
    freq = ROPE_THETA ** (-jnp.arange(n_freq, dtype=F32) / n_freq)
    ang = jnp.concatenate([rows[:, None] * freq, cols[:, None] * freq], axis=-1)
    cos, sin = jnp.cos(ang), jnp.sin(ang)
    cos_t = jnp.tile(jnp.concatenate([cos, cos], axis=-1), (1, LANES // HEAD_DIM))
    sin_t = jnp.tile(jnp.concatenate([-sin, sin], axis=-1), (1, LANES // HEAD_DIM))
    return cos_t, sin_t


def kernel(x, c, ctx, c_ctx, norm_w, ada_w, ada_b, w_in, w_out, q_norm_a, k_norm_a, rpb_b, sink_c,
           final_norm_w):
    bsz = x.shape[0]
    cvecs = jnp.concatenate([c, c_ctx[None], jnp.zeros((8 - bsz - 1, D_MODEL), F32)], axis=0)
    mod = _modulation(cvecs, ada_w, ada_b).reshape(DEPTH, 8, 3, D_MODEL)
    bias = _nb_bias(rpb_b)
    rope_tabs = _rope_tables()
    w_in_bf = w_in.astype(BF16)
    w_out_bf = w_out.astype(BF16)
    cx = ctx
    for l in range(DEPTH):
        need_ctx = l < DEPTH - 1
        mod_x, mod_c = mod[l, :bsz], mod[l, bsz:bsz + 1]
        nw = norm_w[l][None]
        qn_t = jnp.tile(q_norm_a[l], N_HEADS)[None]
        kn_t = jnp.tile(k_norm_a[l], N_KV)[None]
        qa, ka, va, qb, kb, vb, qc, kc, vc, sg = _in_proj(
            x, mod_x, nw, w_in_bf[l], qn_t, kn_t, rope_tabs, tm=512)
        qa_c, ka_c, va_c, qb_c, kb_c, vb_c, qc_c, kc_c, vc_c, sg_c = _in_proj(
            cx, mod_c, nw, w_in_bf[l], qn_t, kn_t, None, tm=CTX_LEN)
        ya = _attn_a(qa, ka, va, ka_c, va_c)
        yb = _attn_b(qb, kb, vb, kb_c, vb_c, bias[l])
        yc = _attn_c(sink_c[l], qc, kc, vc, kc_c, vc_c)
        if need_ctx:
            ya_c, yb_c, yc_c = _attn_ctx(sink_c[l], qa_c, ka_c, va_c, qb_c, kb_c, vb_c, qc_c, kc_c, vc_c)
            cx = _out_proj(ya_c, yb_c, yc_c, sg_c, cx, mod_c, w_out_bf[l], None, tm=CTX_LEN)
        x = _out_proj(ya, yb, yc, sg, x, mod_x, w_out_bf[l],
                      None if need_ctx else final_norm_w[None], tm=512)
    return x
```

```python
import functools

import jax
import jax.numpy as jnp
from jax import lax
from jax.experimental import pallas as pl
from jax.experimental.pallas import tpu as pltpu

D_MODEL = 1024
SEQ = 8192
DEPTH = 2
GRID_W = 64
GRID_H = SEQ // GRID_W
CTX_LEN = 256
HEAD_DIM = 64
N_HEADS = 8
N_KV = 2
GROUP = N_HEADS // N_KV
Q_W = N_HEADS * HEAD_DIM
KV_W = N_KV * HEAD_DIM
MIX_WIDTH = 3 * Q_W
IN_WIDTH = 3 * Q_W + 2 * KV_W + 2 * Q_W + 2 * KV_W + MIX_WIDTH
NB_ROWS = 8
NB_COLS = 16
WINDOW = 128
ROPE_THETA = 10000.0
EPS = 1e-6
SCALE = HEAD_DIM ** -0.5
LOG2E = 1.4426950408889634
NEG = -1e30

LANES = 128
VMEM_LIMIT = 56 * 1024 * 1024

OFF_AQ, OFF_AK, OFF_AV = 0, 512, 640
OFF_BQ, OFF_BK, OFF_BV = 768, 1280, 1792
OFF_CQ, OFF_CK, OFF_CV = 2304, 2816, 2944
OFF_G = 3072

NB_Q_ROWS = 4
NB_TQ = NB_Q_ROWS * GRID_W
NB_WIN_ROWS = NB_Q_ROWS + NB_ROWS
NB_WIN = NB_WIN_ROWS * GRID_W
NB_KEY_BLK = 2 * GRID_W
N_REL = (2 * NB_ROWS - 1) * (2 * NB_COLS - 1)


def _nb_window_row(r0):
    return min(max(r0 - NB_ROWS // 2, 0), GRID_H - NB_WIN_ROWS)


def _nb_first_key_row(r):
    return min(max(r - NB_ROWS // 2, 0), GRID_H - NB_ROWS)


def _nb_bias_pattern(r0):
    w0 = _nb_window_row(r0)
    pat = []
    for qr in range(NB_Q_ROWS):
        rs = _nb_first_key_row(r0 + qr)
        pat.append(tuple((w0 + kr) - (r0 + qr) + NB_ROWS - 1 if rs <= w0 + kr < rs + NB_ROWS else None
                         for kr in range(NB_WIN_ROWS)))
    return tuple(pat)


def _nb_cases():
    patterns, tile_case = [], []
    for r0 in range(0, GRID_H, NB_Q_ROWS):
        pat = _nb_bias_pattern(r0)
        if pat not in patterns:
            patterns.append(pat)
        tile_case.append(patterns.index(pat))
    return patterns, tile_case


NB_PATTERNS, NB_TILE_CASE = _nb_cases()

F32 = jnp.float32
BF16 = jnp.bfloat16


def _cparams(sem):
    return pltpu.CompilerParams(dimension_semantics=sem, vmem_limit_bytes=VMEM_LIMIT)


def _dot(a, b):
    return jnp.dot(a, b, preferred_element_type=F32)


def _dot_nt(a, b):
    return lax.dot_general(a, b, (((1,), (1,)), ((), ())), preferred_element_type=F32)


def _transpose_bf16(z):
    return z.astype(F32).T.astype(BF16)


def _silu(x):
    return x * (1.0 / (1.0 + jnp.exp(-x)))


def _mod_kernel(c_ref, w_ref, b_ref, o_ref):
    cs = _silu(c_ref[...])
    o_ref[0] = jnp.dot(cs, w_ref[0], preferred_element_type=F32,
                       precision=lax.Precision.HIGHEST) + b_ref[0]


def _modulation(cvecs, ada_w, ada_b):
    tn = 1024
    return pl.pallas_call(
        _mod_kernel,
        grid=(DEPTH, 3 * D_MODEL // tn),
        in_specs=[
            pl.BlockSpec((8, D_MODEL), lambda l, j: (0, 0)),
            pl.BlockSpec((1, D_MODEL, tn), lambda l, j: (l, 0, j)),
            pl.BlockSpec((1, 1, tn), lambda l, j: (l, 0, j)),
        ],
        out_specs=pl.BlockSpec((1, 8, tn), lambda l, j: (l, 0, j)),
        out_shape=jax.ShapeDtypeStruct((DEPTH, 8, 3 * D_MODEL), F32),
        compiler_params=_cparams(("arbitrary", "arbitrary")),
        name="adaln_mod",
    )(cvecs, ada_w, ada_b.reshape(DEPTH, 1, 3 * D_MODEL))


def _head_rms(z, wn):
    w = z.shape[1]
    r = lax.broadcasted_iota(jnp.int32, (w, w), 0) // HEAD_DIM
    c = lax.broadcasted_iota(jnp.int32, (w, w), 1) // HEAD_DIM
    ones = (r == c).astype(BF16)
    z2 = z * z
    hi = z2.astype(BF16)
    lo = (z2 - hi.astype(F32)).astype(BF16)
    ss = _dot(hi, ones) + _dot(lo, ones)
    return (z * lax.rsqrt(ss * (1.0 / HEAD_DIM) + EPS)) * wn


def _rope(z, cos_t, sin_t):
    w = z.shape[1]
    reps = w // LANES
    if reps > 1:
        cos_t = jnp.concatenate([cos_t] * reps, axis=1)
        sin_t = jnp.concatenate([sin_t] * reps, axis=1)
    lane = lax.broadcasted_iota(jnp.int32, z.shape, 1)
    partner = jnp.where((lane & (HEAD_DIM // 2)) == 0,
                        pltpu.roll(z, w - HEAD_DIM // 2, 1), pltpu.roll(z, HEAD_DIM // 2, 1))
    return z * cos_t + partner * sin_t


def _in_proj_kernel(*refs, rope):
    if rope:
        (x_ref, mod_ref, nw_ref, w_ref, qn_ref, kn_ref, cos_ref, sin_ref,
         qa_ref, ka_ref, va_ref, qb_ref, kb_ref, vb_ref, qc_ref, kc_ref, vc_ref, sg_ref) = refs
        cos_t, sin_t = cos_ref[...], sin_ref[...]
        rot = lambda z: _rope(z, cos_t, sin_t)
    else:
        (x_ref, mod_ref, nw_ref, w_ref, qn_ref, kn_ref,
         qa_ref, ka_ref, va_ref, qb_ref, kb_ref, vb_ref, qc_ref, kc_ref, vc_ref, sg_ref) = refs
        rot = lambda z: z
    x = x_ref[0]
    y = (x * lax.rsqrt(jnp.mean(x * x, axis=-1, keepdims=True) + EPS)) * nw_ref[...]
    h = (y * (1.0 + mod_ref[0, 1:2, :]) + mod_ref[0, 0:1, :]).astype(BF16)

    def proj(off, width):
        return _dot(h, w_ref[:, off:off + width])

    q_scale = SCALE * LOG2E if rope else SCALE
    gate_w = MIX_WIDTH // 3
    groups = [
        (OFF_AQ, Q_W, qa_ref, lambda z: rot(_head_rms(z, qn_ref[...])) * q_scale),
        (OFF_AK, KV_W, ka_ref, lambda z: rot(_head_rms(z, kn_ref[...]))),
        (OFF_AV, KV_W, va_ref, lambda z: z),
        (OFF_BQ, Q_W, qb_ref, lambda z: z * q_scale),
        (OFF_BK, Q_W, kb_ref, lambda z: z),
        (OFF_BV, Q_W, vb_ref, lambda z: z),
        (OFF_CQ, Q_W, qc_ref, lambda z: rot(z) * q_scale),
        (OFF_CK, KV_W, kc_ref, lambda z: rot(z)),
        (OFF_CV, KV_W, vc_ref, lambda z: z),
    ] + [(OFF_G + t * gate_w, gate_w, (sg_ref, t * gate_w), _silu) for t in range(3)]
    z = proj(groups[0][0], groups[0][1])
    for gi, (off, width, dst, epilogue) in enumerate(groups):
        if gi + 1 < len(groups):
            z_next = proj(groups[gi + 1][0], groups[gi + 1][1])
        out = epilogue(z).astype(BF16)
        if isinstance(dst, tuple):
            dst[0][0, :, dst[1]:dst[1] + width] = out
        else:
            dst[0] = out
        z = z_next


def _in_proj(x, mod, nw, w_bf, qn_t, kn_t, rope_tabs, tm):
    b, n, _ = x.shape
    rope = rope_tabs is not None
    mod_idx = (lambda bi, i: (bi, 0, 0)) if mod.shape[0] > 1 else (lambda bi, i: (0, 0, 0))
    const2 = lambda bi, i: (0, 0)
    in_specs = [
        pl.BlockSpec((1, tm, D_MODEL), lambda bi, i: (bi, i, 0)),
        pl.BlockSpec((1, 3, D_MODEL), mod_idx),
        pl.BlockSpec((1, D_MODEL), const2),
        pl.BlockSpec((D_MODEL, IN_WIDTH), const2, pipeline_mode=pl.Buffered(1)),
        pl.BlockSpec((1, Q_W), const2),
        pl.BlockSpec((1, KV_W), const2),
    ]
    args = [x, mod, nw, w_bf, qn_t, kn_t]
    if rope:
        in_specs += [pl.BlockSpec((tm, LANES), lambda bi, i: (i, 0))] * 2
        args += list(rope_tabs)
    widths = (Q_W, KV_W, KV_W, Q_W, Q_W, Q_W, Q_W, KV_W, KV_W, MIX_WIDTH)
    out_specs = [pl.BlockSpec((1, tm, w), lambda bi, i: (bi, i, 0)) for w in widths]
    out_shape = [jax.ShapeDtypeStruct((b, n, w), BF16) for w in widths]
    return pl.pallas_call(
        functools.partial(_in_proj_kernel, rope=rope),
        grid=(b, n // tm),
        in_specs=in_specs,
        out_specs=out_specs,
        out_shape=out_shape,
        compiler_params=_cparams(("parallel", "parallel")),
        name="in_proj_rope" if rope else "in_proj_ctx",
    )(*args)


def _stack_group_q(q, g):
    t = q.shape[0]
    lane = lax.broadcasted_iota(jnp.int32, (t, LANES), 1)
    keep = (lane // HEAD_DIM) == g
    parts = []
    for j in range(GROUP):
        h = GROUP * g + j
        chunk = q[:, (h // 2) * LANES:(h // 2 + 1) * LANES].astype(F32)
        if h % 2 != g:
            chunk = pltpu.roll(chunk, HEAD_DIM, 1)
        parts.append(jnp.where(keep, chunk, 0.0).astype(BF16))
    return jnp.concatenate(parts, axis=0)


def _unstack_group_out(o, g, t):
    lane = lax.broadcasted_iota(jnp.int32, (t, LANES), 1)
    left = lane < HEAD_DIM
    chunks = []
    for cl in range(2):
        even = o[(2 * cl) * t:(2 * cl + 1) * t]
        odd = o[(2 * cl + 1) * t:(2 * cl + 2) * t]
        if g == 0:
            odd = pltpu.roll(odd, HEAD_DIM, 1)
        else:
            even = pltpu.roll(even, HEAD_DIM, 1)
        chunks.append(jnp.where(left, even, odd))
    return chunks


def _softmax_pv(s_list, v_list, extra_logit=None):
    m = s_list[0].max(axis=1, keepdims=True)
    for s in s_list[1:]:
        m = jnp.maximum(m, s.max(axis=1, keepdims=True))
    if extra_logit is not None:
        m = jnp.maximum(m, extra_logit)
    l = None
    o = None
    for s, v in zip(s_list, v_list):
        p = jnp.exp(s - m)
        ps = p.sum(axis=1, keepdims=True)
        po = _dot(p.astype(BF16), v)
        l = ps if l is None else l + ps
        o = po if o is None else o + po
    if extra_logit is not None:
        l = l + jnp.exp(extra_logit - m)
    return o / l


def _sink_column(sink_ref, g, t):
    row = lax.broadcasted_iota(jnp.int32, (GROUP * t, 1), 0) // t
    col = jnp.full((GROUP * t, 1), sink_ref[GROUP * g], F32)
    for j in range(1, GROUP):
        col = jnp.where(row == j, sink_ref[GROUP * g + j], col)
    return col


ONES_ROWS = 16


def _attn_a_kernel(q_ref, k_ref, v_ref, kc_ref, vc_ref, o_ref, vt_ref, vct_ref, qs_ref, m_ref, acc_ref,
                   s0_ref, s1_ref, s2_ref, mx0_ref, mx1_ref, mx2_ref, *, tq, tk, col_blk):
    n_chunks = SEQ // tk

    @pl.when(pl.program_id(1) == 0)
    def _():
        for j in range(SEQ // tk):
            vt_ref[j] = _transpose_bf16(v_ref[0, j * tk:(j + 1) * tk, :])
        vct_ref[...] = _transpose_bf16(vc_ref[0])

    col_blocks = [slice(c0, c0 + col_blk) for c0 in range(0, GROUP * tq, col_blk)]

    def score_block(k, s_ref, mx_ref, cols):
        s = _dot(k, qs_ref[:, cols])
        s_ref[0:k.shape[0], cols] = s
        mx_ref[:, cols] = s.max(axis=0, keepdims=True)

    def absorb_block(n, s_ref, mx_ref, v_ext, cols):
        m_prev = m_ref[:, cols]
        m_new = jnp.maximum(m_prev, mx_ref[:, cols])
        alpha = jnp.exp2(m_prev - m_new)
        p = jnp.exp2(s_ref[0:n, cols] - m_new).astype(BF16)
        acc_ref[:, cols] = alpha * acc_ref[:, cols] + _dot(v_ext, p)
        m_ref[:, cols] = m_new

    def with_ones(vt_g):
        return jnp.concatenate([vt_g, jnp.ones((ONES_ROWS, vt_g.shape[1]), BF16)], axis=0)

    def latent_keys(j):
        return k_ref[0, pl.ds(pl.multiple_of(j * tk, tk), tk), :]

    qt = _transpose_bf16(q_ref[0])
    for g in range(N_KV):
        rows = slice(g * HEAD_DIM, (g + 1) * HEAD_DIM)
        qs_ref[...] = jnp.zeros(qs_ref.shape, BF16)
        for j in range(GROUP):
            h = GROUP * g + j
            qs_ref[rows, j * tq:(j + 1) * tq] = qt[h * HEAD_DIM:(h + 1) * HEAD_DIM, :]
        m_ref[...] = jnp.full(m_ref.shape, NEG, F32)
        acc_ref[...] = jnp.zeros(acc_ref.shape, F32)

        bufs = ((s0_ref, mx0_ref), (s1_ref, mx1_ref), (s2_ref, mx2_ref))

        def trip(n_cur, cur, vt_g, k_next, nxt):
            v_ext = with_ones(vt_g)
            for cols in col_blocks:
                if k_next is not None:
                    score_block(k_next, *bufs[nxt], cols)
                absorb_block(n_cur, *bufs[cur], v_ext, cols)

        kc = kc_ref[0]
        for cols in col_blocks:
            score_block(kc, *bufs[2], cols)
        trip(CTX_LEN, 2, vct_ref[rows, :], latent_keys(0), 0)

        def body(i, carry):
            j = 3 * i
            trip(tk, 0, vt_ref[j, rows, :], latent_keys(j + 1), 1)
            trip(tk, 1, vt_ref[j + 1, rows, :], latent_keys(j + 2), 2)
            trip(tk, 2, vt_ref[j + 2, rows, :], latent_keys(j + 3), 0)
            return carry

        assert n_chunks % 3 == 1
        lax.fori_loop(0, n_chunks // 3, body, 0)
        trip(tk, 0, vt_ref[n_chunks - 1, rows, :], None, None)
        o = acc_ref[0:HEAD_DIM, :] / acc_ref[HEAD_DIM:HEAD_DIM + 1, :]
        for cl in range(GROUP // 2):
            pair = jnp.concatenate([o[:, (2 * cl) * tq:(2 * cl + 1) * tq],
                                    o[:, (2 * cl + 1) * tq:(2 * cl + 2) * tq]], axis=0)
            c = (GROUP // 2) * g + cl
            o_ref[0, :, c * LANES:(c + 1) * LANES] = _transpose_bf16(pair)


def _attn_a(q, k, v, kc, vc, tq=512, tk=512, col_blk=256):
    b = q.shape[0]
    res = lambda n: pl.BlockSpec((1, n, KV_W), lambda bi, i: (bi, 0, 0))
    return pl.pallas_call(
        functools.partial(_attn_a_kernel, tq=tq, tk=tk, col_blk=col_blk),
        grid=(b, SEQ // tq),
        in_specs=[pl.BlockSpec((1, tq, Q_W), lambda bi, i: (bi, i, 0)),
                  res(SEQ), res(SEQ), res(CTX_LEN), res(CTX_LEN)],
        out_specs=pl.BlockSpec((1, tq, Q_W), lambda bi, i: (bi, i, 0)),
        out_shape=jax.ShapeDtypeStruct((b, SEQ, Q_W), BF16),
        scratch_shapes=[pltpu.VMEM((SEQ // tk, KV_W, tk), BF16),
                        pltpu.VMEM((KV_W, CTX_LEN), BF16),
                        pltpu.VMEM((LANES, GROUP * tq), BF16),
                        pltpu.VMEM((1, GROUP * tq), F32),
                        pltpu.VMEM((HEAD_DIM + ONES_ROWS, GROUP * tq), F32),
                        pltpu.VMEM((tk, GROUP * tq), F32),
                        pltpu.VMEM((tk, GROUP * tq), F32),
                        pltpu.VMEM((tk, GROUP * tq), F32),
                        pltpu.VMEM((1, GROUP * tq), F32),
                        pltpu.VMEM((1, GROUP * tq), F32),
                        pltpu.VMEM((1, GROUP * tq), F32)],
        compiler_params=_cparams(("parallel", "arbitrary")),
        name="attn_global",
    )(q, k, v, kc, vc)


def _fill_feature_major(v_ref, vt_ref, blk):
    def fill(j, carry):
        off = pl.multiple_of(j * blk, blk)
        vt_ref[j] = _transpose_bf16(v_ref[0, pl.ds(off, blk), :])
        return carry

    lax.fori_loop(0, v_ref.shape[1] // blk, fill, 0)


def _with_ones_rows(vt):
    return jnp.concatenate([vt, jnp.ones((ONES_ROWS, vt.shape[1]), BF16)], axis=0)


def _attn_c_kernel(sink_ref, q_ref, k_ref, v_ref, kc_ref, vc_ref, o_ref, vt_ref, vct_ref,
                   qs0_ref, qs1_ref, s0_ref, s1_ref, mx0_ref, mx1_ref, *, tq):
    win = tq + 2 * WINDOW
    n_keys = win + CTX_LEN
    i = pl.program_id(1)

    @pl.when(i == 0)
    def _():
        _fill_feature_major(v_ref, vt_ref, LANES)
        vct_ref[...] = _transpose_bf16(vc_ref[0])

    ws = pl.multiple_of(jnp.clip(i * tq - WINDOW, 0, SEQ - win), LANES)
    blk0 = ws // LANES
    kw = k_ref[0, pl.ds(ws, win), :]
    kc = kc_ref[0]
    kpos = ws + lax.broadcasted_iota(jnp.int32, (win, tq), 0)
    qpos = i * tq + lax.broadcasted_iota(jnp.int32, (win, tq), 1)
    valid = jnp.abs(qpos - kpos) <= WINDOW
    qt = _transpose_bf16(q_ref[0])
    bufs = ((qs0_ref, s0_ref, mx0_ref), (qs1_ref, s1_ref, mx1_ref))
    col_blocks = [slice(j * tq, (j + 1) * tq) for j in range(GROUP)]

    def stack_queries(g):
        qs_ref = bufs[g][0]
        qs_ref[...] = jnp.zeros(qs_ref.shape, BF16)
        for j in range(GROUP):
            h = GROUP * g + j
            qs_ref[g * HEAD_DIM:(g + 1) * HEAD_DIM, j * tq:(j + 1) * tq] = (
                qt[h * HEAD_DIM:(h + 1) * HEAD_DIM, :])

    def score_block(g, j):
        qs_ref, s_ref, mx_ref = bufs[g]
        cols = col_blocks[j]
        s_win = jnp.where(valid, _dot(kw, qs_ref[:, cols]), NEG)
        s_ctx = _dot(kc, qs_ref[:, cols])
        s_ref[0:win, cols] = s_win
        s_ref[win:n_keys, cols] = s_ctx
        sink2 = sink_ref[GROUP * g + j] * LOG2E
        mx_ref[:, cols] = jnp.maximum(
            jnp.maximum(s_win.max(axis=0, keepdims=True), s_ctx.max(axis=0, keepdims=True)), sink2)

    def values(g):
        rows = slice(g * HEAD_DIM, (g + 1) * HEAD_DIM)
        blocks = [vt_ref[blk0 + t, rows, :] for t in range(win // LANES)] + [vct_ref[rows, :]]
        return _with_ones_rows(jnp.concatenate(blocks, axis=1))

    def out_block(g, j, v_ext):
        _, s_ref, mx_ref = bufs[g]
        cols = col_blocks[j]
        m = mx_ref[:, cols]
        p = jnp.exp2(s_ref[:, cols] - m).astype(BF16)
        acc = _dot(v_ext, p)
        l = acc[HEAD_DIM:HEAD_DIM + 1, :] + jnp.exp2(sink_ref[GROUP * g + j] * LOG2E - m)
        return acc[0:HEAD_DIM, :] / l

    def store_pair(g, cl, o_even, o_odd):
        c = (GROUP // 2) * g + cl
        o_ref[0, :, c * LANES:(c + 1) * LANES] = _transpose_bf16(
            jnp.concatenate([o_even, o_odd], axis=0))

    stack_queries(0)
    stack_queries(1)
    for j in range(GROUP):
        score_block(0, j)
    v_ext = values(0)
    outs = []
    for j in range(GROUP):
        score_block(1, j)
        outs.append(out_block(0, j, v_ext))
        if j % 2 == 1:
            store_pair(0, j // 2, outs[j - 1], outs[j])
    v_ext = values(1)
    outs = []
    for j in range(GROUP):
        outs.append(out_block(1, j, v_ext))
        if j % 2 == 1:
            store_pair(1, j // 2, outs[j - 1], outs[j])


def _attn_c(sink, q, k, v, kc, vc, tq=256):
    b = q.shape[0]
    n_keys = tq + 2 * WINDOW + CTX_LEN
    res = lambda n: pl.BlockSpec((1, n, KV_W), lambda bi, i: (bi, 0, 0))
    return pl.pallas_call(
        functools.partial(_attn_c_kernel, tq=tq),
        grid=(b, SEQ // tq),
        in_specs=[pl.BlockSpec(memory_space=pltpu.SMEM),
                  pl.BlockSpec((1, tq, Q_W), lambda bi, i: (bi, i, 0)),
                  res(SEQ), res(SEQ), res(CTX_LEN), res(CTX_LEN)],
        out_specs=pl.BlockSpec((1, tq, Q_W), lambda bi, i: (bi, i, 0)),
        out_shape=jax.ShapeDtypeStruct((b, SEQ, Q_W), BF16),
        scratch_shapes=[pltpu.VMEM((SEQ // LANES, KV_W, LANES), BF16),
                        pltpu.VMEM((KV_W, CTX_LEN), BF16),
                        pltpu.VMEM((LANES, GROUP * tq), BF16),
                        pltpu.VMEM((LANES, GROUP * tq), BF16),
                        pltpu.VMEM((n_keys, GROUP * tq), F32),
                        pltpu.VMEM((n_keys, GROUP * tq), F32),
                        pltpu.VMEM((1, GROUP * tq), F32),
                        pltpu.VMEM((1, GROUP * tq), F32)],
        compiler_params=_cparams(("parallel", "arbitrary")),
        name="attn_window",
    )(sink, q, k, v, kc, vc)


def _nb_bias_kernel(rpb_ref, o_ref, tt_ref):
    ck = lax.broadcasted_iota(jnp.int32, (GRID_W, LANES), 0)
    lane = lax.broadcasted_iota(jnp.int32, (GRID_W, LANES), 1)
    cq = lane & (GRID_W - 1)
    d = ck - cq + (NB_COLS - 1)
    cs = jnp.clip(cq - NB_COLS // 2, 0, GRID_W - NB_COLS)
    col_ok = (ck >= cs) & (ck < cs + NB_COLS)
    n_dc = 2 * NB_COLS - 1
    neg_tile = jnp.full((GRID_W, LANES), NEG, F32)
    even_row = lane < GRID_W
    for h in range(2):
        base = ((pl.program_id(0) * (N_HEADS // 2) + pl.program_id(1)) * 2 + h) * N_REL
        for a in range(2 * NB_ROWS - 1):
            t = neg_tile
            for dd in range(n_dc):
                t = jnp.where(d == dd, rpb_ref[base + a * n_dc + dd] * LOG2E, t)
            tt_ref[a] = jnp.where(col_ok, t, NEG)
        for case, pat in enumerate(NB_PATTERNS):
            for kr in range(NB_WIN_ROWS):
                for qv in range(NB_Q_ROWS // 2):
                    a0, a1 = pat[2 * qv][kr], pat[2 * qv + 1][kr]
                    lo = neg_tile if a0 is None else tt_ref[a0]
                    hi = neg_tile if a1 is None else tt_ref[a1]
                    c0 = h * NB_TQ + qv * LANES
                    o_ref[0, case, 0, kr * GRID_W:(kr + 1) * GRID_W, c0:c0 + LANES] = (
                        jnp.where(even_row, lo, hi))


def _nb_bias(rpb):
    n_case = len(NB_PATTERNS)
    return pl.pallas_call(
        _nb_bias_kernel,
        grid=(DEPTH, N_HEADS // 2),
        in_specs=[pl.BlockSpec(memory_space=pltpu.SMEM)],
        out_specs=pl.BlockSpec((1, n_case, 1, NB_WIN, 2 * NB_TQ), lambda l, c: (l, 0, c, 0, 0)),
        out_shape=jax.ShapeDtypeStruct((DEPTH, n_case, N_HEADS // 2, NB_WIN, 2 * NB_TQ), F32),
        scratch_shapes=[pltpu.VMEM((2 * NB_ROWS - 1, GRID_W, LANES), F32)],
        compiler_params=_cparams(("arbitrary", "arbitrary")),
        name="nb_bias",
    )(rpb.reshape(-1))


def _pair_attention(qp, s_extra_fn, k_list, v_list):
    t = qp.shape[0]
    lane = lax.broadcasted_iota(jnp.int32, (t, LANES), 1)
    qf = qp.astype(F32)
    outs = []
    for h in range(2):
        qpad = jnp.where((lane // HEAD_DIM) == h, qf, 0.0).astype(BF16)
        s_list = [_dot_nt(qpad, k) for k in k_list]
        s_list = s_extra_fn(h, s_list)
        outs.append(_softmax_pv(s_list, v_list))
    return jnp.where(lane < HEAD_DIM, outs[0], outs[1])


def _attn_b_kernel(q_ref, k_ref, v_ref, kc_ref, vc_ref, bias_ref, o_ref, vt_ref, vct_ref,
                   s0_ref, s1_ref, mx0_ref, mx1_ref):
    i = pl.program_id(1)

    @pl.when(i == 0)
    def _():
        _fill_feature_major(v_ref, vt_ref, NB_KEY_BLK)
        vct_ref[...] = _transpose_bf16(vc_ref[0])

    w0 = jnp.clip(NB_Q_ROWS * i - NB_ROWS // 2, 0, GRID_H - NB_WIN_ROWS)
    off = pl.multiple_of(w0 * GRID_W, NB_KEY_BLK)
    blk0 = w0 // 2
    bufs = ((s0_ref, mx0_ref), (s1_ref, mx1_ref))
    zeros = jnp.zeros((NB_TQ, LANES), BF16)
    head0_lanes = lax.broadcasted_iota(jnp.int32, (NB_TQ, LANES), 1) < HEAD_DIM

    def score_stage(c):
        s_ref, mx_ref = bufs[c % 2]
        cols = slice(c * LANES, (c + 1) * LANES)
        qp = q_ref[0, :, cols]
        qpad = jnp.concatenate([jnp.where(head0_lanes, qp, zeros),
                                jnp.where(head0_lanes, zeros, qp)], axis=0)
        s_win = _dot_nt(k_ref[0, pl.ds(off, NB_WIN), cols], qpad) + bias_ref[0, 0, c]
        s_ctx = _dot_nt(kc_ref[0, :, cols], qpad)
        s_ref[0:NB_WIN, :] = s_win
        s_ref[NB_WIN:NB_WIN + CTX_LEN, :] = s_ctx
        mx_ref[...] = jnp.maximum(s_win.max(axis=0, keepdims=True), s_ctx.max(axis=0, keepdims=True))

    def pv_stage(c):
        s_ref, mx_ref = bufs[c % 2]
        cols = slice(c * LANES, (c + 1) * LANES)
        p = jnp.exp2(s_ref[...] - mx_ref[...]).astype(BF16)
        blocks = [vt_ref[blk0 + t, cols, :] for t in range(NB_WIN // NB_KEY_BLK)] + [vct_ref[cols, :]]
        return _dot(_with_ones_rows(jnp.concatenate(blocks, axis=1)), p)

    def finish(c, acc):
        l = acc[2 * HEAD_DIM:2 * HEAD_DIM + 1, :]
        o0 = acc[0:HEAD_DIM, 0:NB_TQ] / l[:, 0:NB_TQ]
        o1 = acc[HEAD_DIM:2 * HEAD_DIM, NB_TQ:2 * NB_TQ] / l[:, NB_TQ:2 * NB_TQ]
        o_ref[0, :, c * LANES:(c + 1) * LANES] = _transpose_bf16(jnp.concatenate([o0, o1], axis=0))

    n_pairs = N_HEADS // 2
    score_stage(0)
    pending = None
    for c in range(n_pairs):
        if c + 1 < n_pairs:
            score_stage(c + 1)
        acc = pv_stage(c)
        if pending is not None:
            finish(*pending)
        pending = (c, acc)
    finish(*pending)


def _attn_b(q, k, v, kc, vc, bias, layer):
    b = q.shape[0]
    n_tiles = SEQ // NB_TQ
    res = lambda n: pl.BlockSpec((1, n, Q_W), lambda bi, i: (bi, 0, 0), pipeline_mode=pl.Buffered(1))

    n_case = len(NB_PATTERNS)
    interior = max(set(NB_TILE_CASE), key=NB_TILE_CASE.count)
    n_lo, n_hi = interior, n_case - 1 - interior
    case_of = lambda t: min(t, n_lo) + max(t - (n_tiles - 1 - n_hi), 0)
    assert [case_of(t) for t in range(n_tiles)] == NB_TILE_CASE

    def bias_idx(bi, i):
        return (layer, jnp.minimum(i, n_lo) + jnp.maximum(i - (n_tiles - 1 - n_hi), 0), 0, 0, 0)

    return pl.pallas_call(
        _attn_b_kernel,
        grid=(b, n_tiles),
        in_specs=[pl.BlockSpec((1, NB_TQ, Q_W), lambda bi, i: (bi, i, 0)),
                  res(SEQ), res(SEQ), res(CTX_LEN), res(CTX_LEN),
                  pl.BlockSpec((1, 1, N_HEADS // 2, NB_WIN, 2 * NB_TQ), bias_idx)],
        out_specs=pl.BlockSpec((1, NB_TQ, Q_W), lambda bi, i: (bi, i, 0)),
        out_shape=jax.ShapeDtypeStruct((b, SEQ, Q_W), BF16),
        scratch_shapes=[pltpu.VMEM((SEQ // NB_KEY_BLK, Q_W, NB_KEY_BLK), BF16),
                        pltpu.VMEM((Q_W, CTX_LEN), BF16),
                        pltpu.VMEM((NB_WIN + CTX_LEN, 2 * NB_TQ), F32),
                        pltpu.VMEM((NB_WIN + CTX_LEN, 2 * NB_TQ), F32),
                        pltpu.VMEM((1, 2 * NB_TQ), F32),
                        pltpu.VMEM((1, 2 * NB_TQ), F32)],
        compiler_params=_cparams(("parallel", "arbitrary")),
        name="attn_neighbourhood",
    )(q, k, v, kc, vc, bias)


def _attn_ctx_kernel(sink_ref, qa_ref, ka_ref, va_ref, qb_ref, kb_ref, vb_ref, qc_ref, kc_ref, vc_ref,
                     oa_ref, ob_ref, oc_ref):
    t = CTX_LEN
    for g in range(N_KV):
        for q_ref, k_ref, v_ref, o_ref, sink in ((qa_ref, ka_ref, va_ref, oa_ref, False),
                                                 (qc_ref, kc_ref, vc_ref, oc_ref, True)):
            qs = _stack_group_q(q_ref[0], g)
            extra = _sink_column(sink_ref, g, t) if sink else None
            o = _softmax_pv([_dot_nt(qs, k_ref[0])], [v_ref[0]], extra)
            chunks = _unstack_group_out(o, g, t)
            for cl in range(2):
                c = 2 * g + cl
                o_ref[0, :, c * LANES:(c + 1) * LANES] = chunks[cl].astype(BF16)
    for c in range(N_HEADS // 2):
        cols = slice(c * LANES, (c + 1) * LANES)
        o = _pair_attention(qb_ref[0, :, cols], lambda h, s_list: s_list,
                            [kb_ref[0, :, cols]], [vb_ref[0, :, cols]])
        ob_ref[0, :, cols] = o.astype(BF16)


def _attn_ctx(sink, qa, ka, va, qb, kb, vb, qc, kc, vc):
    b = qa.shape[0]
    spec = lambda w: pl.BlockSpec((1, CTX_LEN, w), lambda bi: (bi, 0, 0))
    return pl.pallas_call(
        _attn_ctx_kernel,
        grid=(b,),
        in_specs=[pl.BlockSpec(memory_space=pltpu.SMEM),
                  spec(Q_W), spec(KV_W), spec(KV_W), spec(Q_W), spec(Q_W), spec(Q_W),
                  spec(Q_W), spec(KV_W), spec(KV_W)],
        out_specs=[spec(Q_W)] * 3,
        out_shape=[jax.ShapeDtypeStruct((b, CTX_LEN, Q_W), BF16)] * 3,
        compiler_params=_cparams(("parallel",)),
        name="attn_ctx",
    )(sink, qa, ka, va, qb, kb, vb, qc, kc, vc)


def _out_proj_kernel(*refs, final):
    if final:
        ya_ref, yb_ref, yc_ref, sg_ref, x_ref, mod_ref, w_ref, fw_ref, o_ref = refs
    else:
        ya_ref, yb_ref, yc_ref, sg_ref, x_ref, mod_ref, w_ref, o_ref = refs
    acc = None
    for idx, y_ref in enumerate((ya_ref, yb_ref, yc_ref)):
        cols = slice(idx * Q_W, (idx + 1) * Q_W)
        u = (y_ref[0].astype(F32) * sg_ref[0, :, cols].astype(F32)).astype(BF16)
        part = _dot(u, w_ref[cols, :])
        acc = part if acc is None else acc + part
    xn = x_ref[0] + mod_ref[0, 2:3, :] * acc
    if final:
        xn = (xn * lax.rsqrt(jnp.mean(xn * xn, axis=-1, keepdims=True) + EPS)) * fw_ref[...]
    o_ref[0] = xn


def _out_proj(ya, yb, yc, sg, x, mod, w_bf, final_w, tm):
    b, n, _ = x.shape
    final = final_w is not None
    mod_idx = (lambda bi, i: (bi, 0, 0)) if mod.shape[0] > 1 else (lambda bi, i: (0, 0, 0))
    row = lambda w: pl.BlockSpec((1, tm, w), lambda bi, i: (bi, i, 0))
    in_specs = [row(Q_W), row(Q_W), row(Q_W), row(MIX_WIDTH), row(D_MODEL),
                pl.BlockSpec((1, 3, D_MODEL), mod_idx),
                pl.BlockSpec((MIX_WIDTH, D_MODEL), lambda bi, i: (0, 0))]
    args = [ya, yb, yc, sg, x, mod, w_bf]
    if final:
        in_specs.append(pl.BlockSpec((1, D_MODEL), lambda bi, i: (0, 0)))
        args.append(final_w)
    return pl.pallas_call(
        functools.partial(_out_proj_kernel, final=final),
        grid=(b, n // tm),
        in_specs=in_specs,
        out_specs=row(D_MODEL),
        out_shape=jax.ShapeDtypeStruct((b, n, D_MODEL), F32),
        compiler_params=_cparams(("parallel", "parallel")),
        name="out_proj_final" if final else "out_proj",
    )(*args)


def _rope_tables():
    t = jnp.arange(SEQ, dtype=jnp.int32)
    rows = (t // GRID_W).astype(F32)
    cols = (t % GRID_W).astype(F32)
    n_freq = HEAD_DIM // 4
    freq = ROPE_THETA ** (-jnp.arange(n_freq, dtype=F32) / n_freq)
    ang = jnp.concatenate([rows[:, None] * freq, cols[:, None] * freq], axis=-1)
    cos, sin = jnp.cos(ang), jnp.sin(ang)
    cos_t = jnp.tile(jnp.concatenate([cos, cos], axis=-1), (1, LANES // HEAD_DIM))
    sin_t = jnp.tile(jnp.concatenate([-sin, sin], axis=-1), (1, LANES // HEAD_DIM))
    return cos_t, sin_t


def kernel(x, c, ctx, c_ctx, norm_w, ada_w, ada_b, w_in, w_out, q_norm_a, k_norm_a, rpb_b, sink_c,
           final_norm_w):
    bsz = x.shape[0]
    cvecs = jnp.concatenate([c, c_ctx[None], jnp.zeros((8 - bsz - 1, D_MODEL), F32)], axis=0)
    mod = _modulation(cvecs, ada_w, ada_b).reshape(DEPTH, 8, 3, D_MODEL)
    bias = _nb_bias(rpb_b)
    rope_tabs = _rope_tables()
    w_in_bf = w_in.astype(BF16)
    w_out_bf = w_out.astype(BF16)
    cx = ctx
    for l in range(DEPTH):
        need_ctx = l < DEPTH - 1
        mod_x, mod_c = mod[l, :bsz], mod[l, bsz:bsz + 1]
        nw = norm_w[l][None]
        qn_t = jnp.tile(q_norm_a[l], N_HEADS)[None]
        kn_t = jnp.tile(k_norm_a[l], N_KV)[None]
        qa, ka, va, qb, kb, vb, qc, kc, vc, sg = _in_proj(
            x, mod_x, nw, w_in_bf[l], qn_t, kn_t, rope_tabs, tm=512)
        qa_c, ka_c, va_c, qb_c, kb_c, vb_c, qc_c, kc_c, vc_c, sg_c = _in_proj(
            cx, mod_c, nw, w_in_bf[l], qn_t, kn_t, None, tm=CTX_LEN)
        ya = _attn_a(qa, ka, va, ka_c, va_c)
        yb = _attn_b(qb, kb, vb, kb_c, vb_c, bias, l)
        yc = _attn_c(sink_c[l], qc, kc, vc, kc_c, vc_c)
        if need_ctx:
            ya_c, yb_c, yc_c = _attn_ctx(sink_c[l], qa_c, ka_c, va_c, qb_c, kb_c, vb_c, qc_c, kc_c, vc_c)
            cx = _out_proj(ya_c, yb_c, yc_c, sg_c, cx, mod_c, w_out_bf[l], None, tm=CTX_LEN)
        x = _out_proj(ya, yb, yc, sg, x, mod_x, w_out_bf[l],
                      None if need_ctx else final_norm_w[None], tm=512)
    return x
```

```python
import functools

import jax
import jax.numpy as jnp
from jax import lax
from jax.experimental import pallas as pl
from jax.experimental.pallas import tpu as pltpu

D_MODEL = 1024
SEQ = 8192
DEPTH = 2
GRID_W = 64
GRID_H = SEQ // GRID_W
CTX_LEN = 256
HEAD_DIM = 64
N_HEADS = 8
N_KV = 2
GROUP = N_HEADS // N_KV
Q_W = N_HEADS * HEAD_DIM
KV_W = N_KV * HEAD_DIM
MIX_WIDTH = 3 * Q_W
IN_WIDTH = 3 * Q_W + 2 * KV_W + 2 * Q_W + 2 * KV_W + MIX_WIDTH
NB_ROWS = 8
NB_COLS = 16
WINDOW = 128
ROPE_THETA = 10000.0
EPS = 1e-6
SCALE = HEAD_DIM ** -0.5
LOG2E = 1.4426950408889634
NEG = -1e30

LANES = 128
VMEM_LIMIT = 56 * 1024 * 1024

OFF_AQ, OFF_AK, OFF_AV = 0, 512, 640
OFF_BQ, OFF_BK, OFF_BV = 768, 1280, 1792
OFF_CQ, OFF_CK, OFF_CV = 2304, 2816, 2944
OFF_G = 3072

NB_Q_ROWS = 4
NB_TQ = NB_Q_ROWS * GRID_W
NB_WIN_ROWS = NB_Q_ROWS + NB_ROWS
NB_WIN = NB_WIN_ROWS * GRID_W
NB_KEY_BLK = 2 * GRID_W
N_REL = (2 * NB_ROWS - 1) * (2 * NB_COLS - 1)


def _nb_window_row(r0):
    return min(max(r0 - NB_ROWS // 2, 0), GRID_H - NB_WIN_ROWS)


def _nb_first_key_row(r):
    return min(max(r - NB_ROWS // 2, 0), GRID_H - NB_ROWS)


def _nb_bias_pattern(r0):
    w0 = _nb_window_row(r0)
    pat = []
    for qr in range(NB_Q_ROWS):
        rs = _nb_first_key_row(r0 + qr)
        pat.append(tuple((w0 + kr) - (r0 + qr) + NB_ROWS - 1 if rs <= w0 + kr < rs + NB_ROWS else None
                         for kr in range(NB_WIN_ROWS)))
    return tuple(pat)


def _nb_cases():
    patterns, tile_case = [], []
    for r0 in range(0, GRID_H, NB_Q_ROWS):
        pat = _nb_bias_pattern(r0)
        if pat not in patterns:
            patterns.append(pat)
        tile_case.append(patterns.index(pat))
    return patterns, tile_case


NB_PATTERNS, NB_TILE_CASE = _nb_cases()

F32 = jnp.float32
BF16 = jnp.bfloat16


def _cparams(sem):
    return pltpu.CompilerParams(dimension_semantics=sem, vmem_limit_bytes=VMEM_LIMIT)


def _dot(a, b):
    return jnp.dot(a, b, preferred_element_type=F32)


def _dot_nt(a, b):
    return lax.dot_general(a, b, (((1,), (1,)), ((), ())), preferred_element_type=F32)


def _transpose_bf16(z):
    return z.astype(F32).T.astype(BF16)


def _silu(x):
    return x * (1.0 / (1.0 + jnp.exp(-x)))


def _mod_kernel(c_ref, w_ref, b_ref, o_ref):
    cs = _silu(c_ref[...])
    o_ref[0] = jnp.dot(cs, w_ref[0], preferred_element_type=F32,
                       precision=lax.Precision.HIGHEST) + b_ref[0]


def _modulation(cvecs, ada_w, ada_b):
    tn = 1024
    return pl.pallas_call(
        _mod_kernel,
        grid=(DEPTH, 3 * D_MODEL // tn),
        in_specs=[
            pl.BlockSpec((8, D_MODEL), lambda l, j: (0, 0)),
            pl.BlockSpec((1, D_MODEL, tn), lambda l, j: (l, 0, j)),
            pl.BlockSpec((1, 1, tn), lambda l, j: (l, 0, j)),
        ],
        out_specs=pl.BlockSpec((1, 8, tn), lambda l, j: (l, 0, j)),
        out_shape=jax.ShapeDtypeStruct((DEPTH, 8, 3 * D_MODEL), F32),
        compiler_params=_cparams(("arbitrary", "arbitrary")),
        name="adaln_mod",
    )(cvecs, ada_w, ada_b.reshape(DEPTH, 1, 3 * D_MODEL))


def _head_rms(z, wn):
    w = z.shape[1]
    r = lax.broadcasted_iota(jnp.int32, (w, w), 0) // HEAD_DIM
    c = lax.broadcasted_iota(jnp.int32, (w, w), 1) // HEAD_DIM
    ones = (r == c).astype(BF16)
    ss = _dot((z * z).astype(BF16), ones)
    return (z * lax.rsqrt(ss * (1.0 / HEAD_DIM) + EPS)) * wn


def _rope(z, cos_t, sin_t):
    w = z.shape[1]
    reps = w // LANES
    if reps > 1:
        cos_t = jnp.concatenate([cos_t] * reps, axis=1)
        sin_t = jnp.concatenate([sin_t] * reps, axis=1)
    lane = lax.broadcasted_iota(jnp.int32, z.shape, 1)
    partner = jnp.where((lane & (HEAD_DIM // 2)) == 0,
                        pltpu.roll(z, w - HEAD_DIM // 2, 1), pltpu.roll(z, HEAD_DIM // 2, 1))
    return z * cos_t + partner * sin_t


def _in_proj_kernel(*refs, rope):
    if rope:
        (x_ref, mod_ref, nw_ref, w_ref, qn_ref, kn_ref, cos_ref, sin_ref,
         qa_ref, ka_ref, va_ref, qb_ref, kb_ref, vb_ref, qc_ref, kc_ref, vc_ref, sg_ref) = refs
        cos_t, sin_t = cos_ref[...], sin_ref[...]
        rot = lambda z: _rope(z, cos_t, sin_t)
    else:
        (x_ref, mod_ref, nw_ref, w_ref, qn_ref, kn_ref,
         qa_ref, ka_ref, va_ref, qb_ref, kb_ref, vb_ref, qc_ref, kc_ref, vc_ref, sg_ref) = refs
        rot = lambda z: z
    x = x_ref[0]
    y = (x * lax.rsqrt(jnp.mean(x * x, axis=-1, keepdims=True) + EPS)) * nw_ref[...]
    h = (y * (1.0 + mod_ref[0, 1:2, :]) + mod_ref[0, 0:1, :]).astype(BF16)

    def proj(off, width):
        return _dot(h, w_ref[:, off:off + width])

    q_scale = SCALE * LOG2E if rope else SCALE
    gate_w = MIX_WIDTH // 3
    groups = [
        (OFF_AQ, Q_W, qa_ref, lambda z: rot(_head_rms(z, qn_ref[...])) * q_scale),
        (OFF_AK, KV_W, ka_ref, lambda z: rot(_head_rms(z, kn_ref[...]))),
        (OFF_AV, KV_W, va_ref, lambda z: z),
        (OFF_BQ, Q_W, qb_ref, lambda z: z * q_scale),
        (OFF_BK, Q_W, kb_ref, lambda z: z),
        (OFF_BV, Q_W, vb_ref, lambda z: z),
        (OFF_CQ, Q_W, qc_ref, lambda z: rot(z) * q_scale),
        (OFF_CK, KV_W, kc_ref, lambda z: rot(z)),
        (OFF_CV, KV_W, vc_ref, lambda z: z),
    ] + [(OFF_G + t * gate_w, gate_w, (sg_ref, t * gate_w), _silu) for t in range(3)]
    z = proj(groups[0][0], groups[0][1])
    for gi, (off, width, dst, epilogue) in enumerate(groups):
        if gi + 1 < len(groups):
            z_next = proj(groups[gi + 1][0], groups[gi + 1][1])
        out = epilogue(z).astype(BF16)
        if isinstance(dst, tuple):
            dst[0][0, :, dst[1]:dst[1] + width] = out
        else:
            dst[0] = out
        z = z_next


def _in_proj(x, mod, nw, w_bf, qn_t, kn_t, rope_tabs, tm):
    b, n, _ = x.shape
    rope = rope_tabs is not None
    mod_idx = (lambda bi, i: (bi, 0, 0)) if mod.shape[0] > 1 else (lambda bi, i: (0, 0, 0))
    const2 = lambda bi, i: (0, 0)
    in_specs = [
        pl.BlockSpec((1, tm, D_MODEL), lambda bi, i: (bi, i, 0)),
        pl.BlockSpec((1, 3, D_MODEL), mod_idx),
        pl.BlockSpec((1, D_MODEL), const2),
        pl.BlockSpec((D_MODEL, IN_WIDTH), const2, pipeline_mode=pl.Buffered(1)),
        pl.BlockSpec((1, Q_W), const2),
        pl.BlockSpec((1, KV_W), const2),
    ]
    args = [x, mod, nw, w_bf, qn_t, kn_t]
    if rope:
        in_specs += [pl.BlockSpec((tm, LANES), lambda bi, i: (i, 0))] * 2
        args += list(rope_tabs)
    widths = (Q_W, KV_W, KV_W, Q_W, Q_W, Q_W, Q_W, KV_W, KV_W, MIX_WIDTH)
    out_specs = [pl.BlockSpec((1, tm, w), lambda bi, i: (bi, i, 0)) for w in widths]
    out_shape = [jax.ShapeDtypeStruct((b, n, w), BF16) for w in widths]
    return pl.pallas_call(
        functools.partial(_in_proj_kernel, rope=rope),
        grid=(b, n // tm),
        in_specs=in_specs,
        out_specs=out_specs,
        out_shape=out_shape,
        compiler_params=_cparams(("parallel", "parallel")),
        name="in_proj_rope" if rope else "in_proj_ctx",
    )(*args)


def _stack_group_q(q, g):
    t = q.shape[0]
    lane = lax.broadcasted_iota(jnp.int32, (t, LANES), 1)
    keep = (lane // HEAD_DIM) == g
    parts = []
    for j in range(GROUP):
        h = GROUP * g + j
        chunk = q[:, (h // 2) * LANES:(h // 2 + 1) * LANES].astype(F32)
        if h % 2 != g:
            chunk = pltpu.roll(chunk, HEAD_DIM, 1)
        parts.append(jnp.where(keep, chunk, 0.0).astype(BF16))
    return jnp.concatenate(parts, axis=0)


def _unstack_group_out(o, g, t):
    lane = lax.broadcasted_iota(jnp.int32, (t, LANES), 1)
    left = lane < HEAD_DIM
    chunks = []
    for cl in range(2):
        even = o[(2 * cl) * t:(2 * cl + 1) * t]
        odd = o[(2 * cl + 1) * t:(2 * cl + 2) * t]
        if g == 0:
            odd = pltpu.roll(odd, HEAD_DIM, 1)
        else:
            even = pltpu.roll(even, HEAD_DIM, 1)
        chunks.append(jnp.where(left, even, odd))
    return chunks


def _softmax_pv(s_list, v_list, extra_logit=None):
    m = s_list[0].max(axis=1, keepdims=True)
    for s in s_list[1:]:
        m = jnp.maximum(m, s.max(axis=1, keepdims=True))
    if extra_logit is not None:
        m = jnp.maximum(m, extra_logit)
    l = None
    o = None
    for s, v in zip(s_list, v_list):
        p = jnp.exp(s - m)
        ps = p.sum(axis=1, keepdims=True)
        po = _dot(p.astype(BF16), v)
        l = ps if l is None else l + ps
        o = po if o is None else o + po
    if extra_logit is not None:
        l = l + jnp.exp(extra_logit - m)
    return o / l


def _sink_column(sink_ref, g, t):
    row = lax.broadcasted_iota(jnp.int32, (GROUP * t, 1), 0) // t
    col = jnp.full((GROUP * t, 1), sink_ref[GROUP * g], F32)
    for j in range(1, GROUP):
        col = jnp.where(row == j, sink_ref[GROUP * g + j], col)
    return col


ONES_ROWS = 16


def _attn_a_kernel(q_ref, k_ref, v_ref, kc_ref, vc_ref, o_ref, vt_ref, vct_ref,
                   qs0_ref, qs1_ref, m0_ref, m1_ref, acc0_ref, acc1_ref,
                   s0_ref, s1_ref, s2_ref, mx0_ref, mx1_ref, mx2_ref, *, tq, tk, col_blk):
    n_chunks = SEQ // tk

    @pl.when(pl.program_id(1) == 0)
    def _():
        for j in range(SEQ // tk):
            vt_ref[j] = _transpose_bf16(v_ref[0, j * tk:(j + 1) * tk, :])
        vct_ref[...] = _transpose_bf16(vc_ref[0])

    col_blocks = [slice(c0, c0 + col_blk) for c0 in range(0, GROUP * tq, col_blk)]
    groups = ((qs0_ref, m0_ref, acc0_ref), (qs1_ref, m1_ref, acc1_ref))
    bufs = ((s0_ref, mx0_ref), (s1_ref, mx1_ref), (s2_ref, mx2_ref))

    def score_block(g, k, buf, cols):
        s_ref, mx_ref = bufs[buf]
        s = _dot(k, groups[g][0][:, cols])
        s_ref[0:k.shape[0], cols] = s
        mx_ref[:, cols] = s.max(axis=0, keepdims=True)

    def absorb_block(g, n, buf, v_ext, cols):
        s_ref, mx_ref = bufs[buf]
        _, m_ref, acc_ref = groups[g]
        m_prev = m_ref[:, cols]
        m_new = jnp.maximum(m_prev, mx_ref[:, cols])
        alpha = jnp.exp2(m_prev - m_new)
        p = jnp.exp2(s_ref[0:n, cols] - m_new).astype(BF16)
        acc_ref[:, cols] = alpha * acc_ref[:, cols] + _dot(v_ext, p)
        m_ref[:, cols] = m_new

    def latent_keys(j):
        return k_ref[0, pl.ds(pl.multiple_of(j * tk, tk), tk), :]

    def trip(g, n_cur, cur, vt_g, nxt=None):
        v_ext = _with_ones_rows(vt_g)
        for cols in col_blocks:
            if nxt is not None:
                score_block(nxt[0], nxt[1], nxt[2], cols)
            absorb_block(g, n_cur, cur, v_ext, cols)

    qt = _transpose_bf16(q_ref[0])
    for g in range(N_KV):
        qs_ref, m_ref, acc_ref = groups[g]
        qs_ref[...] = jnp.zeros(qs_ref.shape, BF16)
        for j in range(GROUP):
            h = GROUP * g + j
            qs_ref[g * HEAD_DIM:(g + 1) * HEAD_DIM, j * tq:(j + 1) * tq] = (
                qt[h * HEAD_DIM:(h + 1) * HEAD_DIM, :])
        m_ref[...] = jnp.full(m_ref.shape, NEG, F32)
        acc_ref[...] = jnp.zeros(acc_ref.shape, F32)

    assert N_KV == 2 and n_chunks % 3 == 1
    kc = kc_ref[0]
    for cols in col_blocks:
        score_block(0, kc, 2, cols)
    for g in range(N_KV):
        rows = slice(g * HEAD_DIM, (g + 1) * HEAD_DIM)
        a, b, c = (0, 1, 2) if g == 0 else (1, 2, 0)
        if g == 0:
            trip(0, CTX_LEN, 2, vct_ref[rows, :], (0, latent_keys(0), a))

        def body(i, carry, g=g, rows=rows, a=a, b=b, c=c):
            j = 3 * i
            trip(g, tk, a, vt_ref[j, rows, :], (g, latent_keys(j + 1), b))
            trip(g, tk, b, vt_ref[j + 1, rows, :], (g, latent_keys(j + 2), c))
            trip(g, tk, c, vt_ref[j + 2, rows, :], (g, latent_keys(j + 3), a))
            return carry

        lax.fori_loop(0, n_chunks // 3, body, 0)
        if g == 0:
            trip(0, tk, a, vt_ref[n_chunks - 1, rows, :], (1, latent_keys(0), b))
        else:
            trip(1, tk, a, vt_ref[n_chunks - 1, rows, :], (1, kc, b))
            trip(1, CTX_LEN, b, vct_ref[rows, :])

        acc_ref = groups[g][2]
        o = acc_ref[0:HEAD_DIM, :] / acc_ref[HEAD_DIM:HEAD_DIM + 1, :]
        for cl in range(GROUP // 2):
            pair = jnp.concatenate([o[:, (2 * cl) * tq:(2 * cl + 1) * tq],
                                    o[:, (2 * cl + 1) * tq:(2 * cl + 2) * tq]], axis=0)
            cc = (GROUP // 2) * g + cl
            o_ref[0, :, cc * LANES:(cc + 1) * LANES] = _transpose_bf16(pair)


def _attn_a(q, k, v, kc, vc, tq=512, tk=512, col_blk=256):
    b = q.shape[0]
    res = lambda n: pl.BlockSpec((1, n, KV_W), lambda bi, i: (bi, 0, 0))
    return pl.pallas_call(
        functools.partial(_attn_a_kernel, tq=tq, tk=tk, col_blk=col_blk),
        grid=(b, SEQ // tq),
        in_specs=[pl.BlockSpec((1, tq, Q_W), lambda bi, i: (bi, i, 0)),
                  res(SEQ), res(SEQ), res(CTX_LEN), res(CTX_LEN)],
        out_specs=pl.BlockSpec((1, tq, Q_W), lambda bi, i: (bi, i, 0)),
        out_shape=jax.ShapeDtypeStruct((b, SEQ, Q_W), BF16),
        scratch_shapes=[pltpu.VMEM((SEQ // tk, KV_W, tk), BF16),
                        pltpu.VMEM((KV_W, CTX_LEN), BF16),
                        pltpu.VMEM((LANES, GROUP * tq), BF16),
                        pltpu.VMEM((LANES, GROUP * tq), BF16),
                        pltpu.VMEM((1, GROUP * tq), F32),
                        pltpu.VMEM((1, GROUP * tq), F32),
                        pltpu.VMEM((HEAD_DIM + ONES_ROWS, GROUP * tq), F32),
                        pltpu.VMEM((HEAD_DIM + ONES_ROWS, GROUP * tq), F32),
                        pltpu.VMEM((tk, GROUP * tq), F32),
                        pltpu.VMEM((tk, GROUP * tq), F32),
                        pltpu.VMEM((tk, GROUP * tq), F32),
                        pltpu.VMEM((1, GROUP * tq), F32),
                        pltpu.VMEM((1, GROUP * tq), F32),
                        pltpu.VMEM((1, GROUP * tq), F32)],
        compiler_params=_cparams(("parallel", "arbitrary")),
        name="attn_global",
    )(q, k, v, kc, vc)


def _fill_feature_major(v_ref, vt_ref, blk):
    def fill(j, carry):
        off = pl.multiple_of(j * blk, blk)
        vt_ref[j] = _transpose_bf16(v_ref[0, pl.ds(off, blk), :])
        return carry

    lax.fori_loop(0, v_ref.shape[1] // blk, fill, 0)


def _with_ones_rows(vt):
    return jnp.concatenate([vt, jnp.ones((ONES_ROWS, vt.shape[1]), BF16)], axis=0)


def _attn_c_kernel(sink_ref, q_ref, k_ref, v_ref, kc_ref, vc_ref, o_ref, vt_ref, vct_ref,
                   qs0_ref, qs1_ref, s0_ref, s1_ref, mx0_ref, mx1_ref, band_ref, *, tq):
    win = tq + 2 * WINDOW
    n_keys = win + CTX_LEN
    i = pl.program_id(1)

    @pl.when(i == 0)
    def _():
        _fill_feature_major(v_ref, vt_ref, LANES)
        vct_ref[...] = _transpose_bf16(vc_ref[0])

    ws = pl.multiple_of(jnp.clip(i * tq - WINDOW, 0, SEQ - win), LANES)
    blk0 = ws // LANES
    kw = k_ref[0, pl.ds(ws, win), :]
    kc = kc_ref[0]
    kpos = ws + lax.broadcasted_iota(jnp.int32, (win, tq), 0)
    qpos = i * tq + lax.broadcasted_iota(jnp.int32, (win, tq), 1)
    band_ref[...] = jnp.where(jnp.abs(qpos - kpos) <= WINDOW, 0.0, NEG)
    qt = _transpose_bf16(q_ref[0])
    bufs = ((qs0_ref, s0_ref, mx0_ref), (qs1_ref, s1_ref, mx1_ref))
    col_blocks = [slice(j * tq, (j + 1) * tq) for j in range(GROUP)]

    def stack_queries(g):
        qs_ref = bufs[g][0]
        qs_ref[...] = jnp.zeros(qs_ref.shape, BF16)
        for j in range(GROUP):
            h = GROUP * g + j
            qs_ref[g * HEAD_DIM:(g + 1) * HEAD_DIM, j * tq:(j + 1) * tq] = (
                qt[h * HEAD_DIM:(h + 1) * HEAD_DIM, :])

    def score_block(g, j):
        qs_ref, s_ref, mx_ref = bufs[g]
        cols = col_blocks[j]
        s_win = _dot(kw, qs_ref[:, cols]) + band_ref[...]
        s_ctx = _dot(kc, qs_ref[:, cols])
        s_ref[0:win, cols] = s_win
        s_ref[win:n_keys, cols] = s_ctx
        sink2 = sink_ref[GROUP * g + j] * LOG2E
        mx_ref[:, cols] = jnp.maximum(
            jnp.maximum(s_win.max(axis=0, keepdims=True), s_ctx.max(axis=0, keepdims=True)), sink2)

    def values(g):
        rows = slice(g * HEAD_DIM, (g + 1) * HEAD_DIM)
        blocks = [vt_ref[blk0 + t, rows, :] for t in range(win // LANES)] + [vct_ref[rows, :]]
        return _with_ones_rows(jnp.concatenate(blocks, axis=1))

    def out_block(g, j, v_ext):
        _, s_ref, mx_ref = bufs[g]
        cols = col_blocks[j]
        m = mx_ref[:, cols]
        p = jnp.exp2(s_ref[:, cols] - m).astype(BF16)
        acc = _dot(v_ext, p)
        l = acc[HEAD_DIM:HEAD_DIM + 1, :] + jnp.exp2(sink_ref[GROUP * g + j] * LOG2E - m)
        return acc[0:HEAD_DIM, :] / l

    def store_pair(g, cl, o_even, o_odd):
        c = (GROUP // 2) * g + cl
        o_ref[0, :, c * LANES:(c + 1) * LANES] = _transpose_bf16(
            jnp.concatenate([o_even, o_odd], axis=0))

    stack_queries(0)
    stack_queries(1)
    for j in range(GROUP):
        score_block(0, j)
    v_ext = values(0)
    outs = []
    for j in range(GROUP):
        score_block(1, j)
        outs.append(out_block(0, j, v_ext))
        if j % 2 == 1:
            store_pair(0, j // 2, outs[j - 1], outs[j])
    v_ext = values(1)
    outs = []
    for j in range(GROUP):
        outs.append(out_block(1, j, v_ext))
        if j % 2 == 1:
            store_pair(1, j // 2, outs[j - 1], outs[j])


def _attn_c(sink, q, k, v, kc, vc, tq=256):
    b = q.shape[0]
    n_keys = tq + 2 * WINDOW + CTX_LEN
    res = lambda n: pl.BlockSpec((1, n, KV_W), lambda bi, i: (bi, 0, 0))
    return pl.pallas_call(
        functools.partial(_attn_c_kernel, tq=tq),
        grid=(b, SEQ // tq),
        in_specs=[pl.BlockSpec(memory_space=pltpu.SMEM),
                  pl.BlockSpec((1, tq, Q_W), lambda bi, i: (bi, i, 0)),
                  res(SEQ), res(SEQ), res(CTX_LEN), res(CTX_LEN)],
        out_specs=pl.BlockSpec((1, tq, Q_W), lambda bi, i: (bi, i, 0)),
        out_shape=jax.ShapeDtypeStruct((b, SEQ, Q_W), BF16),
        scratch_shapes=[pltpu.VMEM((SEQ // LANES, KV_W, LANES), BF16),
                        pltpu.VMEM((KV_W, CTX_LEN), BF16),
                        pltpu.VMEM((LANES, GROUP * tq), BF16),
                        pltpu.VMEM((LANES, GROUP * tq), BF16),
                        pltpu.VMEM((n_keys, GROUP * tq), F32),
                        pltpu.VMEM((n_keys, GROUP * tq), F32),
                        pltpu.VMEM((1, GROUP * tq), F32),
                        pltpu.VMEM((1, GROUP * tq), F32),
                        pltpu.VMEM((tq + 2 * WINDOW, tq), F32)],
        compiler_params=_cparams(("parallel", "arbitrary")),
        name="attn_window",
    )(sink, q, k, v, kc, vc)


def _nb_bias_kernel(rpb_ref, o_ref, tt_ref):
    ck = lax.broadcasted_iota(jnp.int32, (GRID_W, LANES), 0)
    lane = lax.broadcasted_iota(jnp.int32, (GRID_W, LANES), 1)
    cq = lane & (GRID_W - 1)
    d = ck - cq + (NB_COLS - 1)
    cs = jnp.clip(cq - NB_COLS // 2, 0, GRID_W - NB_COLS)
    col_ok = (ck >= cs) & (ck < cs + NB_COLS)
    n_dc = 2 * NB_COLS - 1
    neg_tile = jnp.full((GRID_W, LANES), NEG, F32)
    even_row = lane < GRID_W
    for h in range(2):
        base = ((pl.program_id(0) * (N_HEADS // 2) + pl.program_id(1)) * 2 + h) * N_REL
        for a in range(2 * NB_ROWS - 1):
            t = neg_tile
            for dd in range(n_dc):
                t = jnp.where(d == dd, rpb_ref[base + a * n_dc + dd] * LOG2E, t)
            tt_ref[a] = jnp.where(col_ok, t, NEG)
        for case, pat in enumerate(NB_PATTERNS):
            for kr in range(NB_WIN_ROWS):
                for qv in range(NB_Q_ROWS // 2):
                    a0, a1 = pat[2 * qv][kr], pat[2 * qv + 1][kr]
                    lo = neg_tile if a0 is None else tt_ref[a0]
                    hi = neg_tile if a1 is None else tt_ref[a1]
                    c0 = h * NB_TQ + qv * LANES
                    o_ref[0, case, 0, kr * GRID_W:(kr + 1) * GRID_W, c0:c0 + LANES] = (
                        jnp.where(even_row, lo, hi))


def _nb_bias(rpb):
    n_case = len(NB_PATTERNS)
    return pl.pallas_call(
        _nb_bias_kernel,
        grid=(DEPTH, N_HEADS // 2),
        in_specs=[pl.BlockSpec(memory_space=pltpu.SMEM)],
        out_specs=pl.BlockSpec((1, n_case, 1, NB_WIN, 2 * NB_TQ), lambda l, c: (l, 0, c, 0, 0)),
        out_shape=jax.ShapeDtypeStruct((DEPTH, n_case, N_HEADS // 2, NB_WIN, 2 * NB_TQ), F32),
        scratch_shapes=[pltpu.VMEM((2 * NB_ROWS - 1, GRID_W, LANES), F32)],
        compiler_params=_cparams(("arbitrary", "arbitrary")),
        name="nb_bias",
    )(rpb.reshape(-1))


def _pair_attention(qp, s_extra_fn, k_list, v_list):
    t = qp.shape[0]
    lane = lax.broadcasted_iota(jnp.int32, (t, LANES), 1)
    qf = qp.astype(F32)
    outs = []
    for h in range(2):
        qpad = jnp.where((lane // HEAD_DIM) == h, qf, 0.0).astype(BF16)
        s_list = [_dot_nt(qpad, k) for k in k_list]
        s_list = s_extra_fn(h, s_list)
        outs.append(_softmax_pv(s_list, v_list))
    return jnp.where(lane < HEAD_DIM, outs[0], outs[1])


def _attn_b_kernel(q_ref, k_ref, v_ref, kc_ref, vc_ref, bias_ref, o_ref, vt_ref, vct_ref,
                   s0_ref, s1_ref, mx0_ref, mx1_ref):
    i = pl.program_id(1)

    @pl.when(i == 0)
    def _():
        _fill_feature_major(v_ref, vt_ref, NB_KEY_BLK)
        vct_ref[...] = _transpose_bf16(vc_ref[0])

    w0 = jnp.clip(NB_Q_ROWS * i - NB_ROWS // 2, 0, GRID_H - NB_WIN_ROWS)
    off = pl.multiple_of(w0 * GRID_W, NB_KEY_BLK)
    blk0 = w0 // 2
    bufs = ((s0_ref, mx0_ref), (s1_ref, mx1_ref))
    zeros = jnp.zeros((NB_TQ, LANES), BF16)
    head0_lanes = lax.broadcasted_iota(jnp.int32, (NB_TQ, LANES), 1) < HEAD_DIM

    def score_stage(h):
        s_ref, mx_ref = bufs[h % 2]
        c, t = divmod(h, 2)
        cols = slice(c * LANES, (c + 1) * LANES)
        qp = q_ref[0, :, cols]
        qpad = jnp.where(head0_lanes, qp, zeros) if t == 0 else jnp.where(head0_lanes, zeros, qp)
        s_win = (_dot_nt(k_ref[0, pl.ds(off, NB_WIN), cols], qpad)
                 + bias_ref[0, 0, c, :, t * NB_TQ:(t + 1) * NB_TQ])
        s_ctx = _dot_nt(kc_ref[0, :, cols], qpad)
        s_ref[0:NB_WIN, :] = s_win
        s_ref[NB_WIN:NB_WIN + CTX_LEN, :] = s_ctx
        mx_ref[...] = jnp.maximum(s_win.max(axis=0, keepdims=True), s_ctx.max(axis=0, keepdims=True))

    def pv_stage(h):
        s_ref, mx_ref = bufs[h % 2]
        rows = slice(h * HEAD_DIM, (h + 1) * HEAD_DIM)
        p = jnp.exp2(s_ref[...] - mx_ref[...]).astype(BF16)
        blocks = [vt_ref[blk0 + t, rows, :] for t in range(NB_WIN // NB_KEY_BLK)] + [vct_ref[rows, :]]
        acc = _dot(_with_ones_rows(jnp.concatenate(blocks, axis=1)), p)
        return acc[0:HEAD_DIM, :] / acc[HEAD_DIM:HEAD_DIM + 1, :]

    score_stage(0)
    outs = []
    for h in range(N_HEADS):
        if h + 1 < N_HEADS:
            score_stage(h + 1)
        outs.append(pv_stage(h))
        if h % 2 == 1:
            o_ref[0, :, (h // 2) * LANES:(h // 2 + 1) * LANES] = _transpose_bf16(
                jnp.concatenate([outs[h - 1], outs[h]], axis=0))


def _attn_b(q, k, v, kc, vc, bias, layer):
    b = q.shape[0]
    n_tiles = SEQ // NB_TQ
    res = lambda n: pl.BlockSpec((1, n, Q_W), lambda bi, i: (bi, 0, 0), pipeline_mode=pl.Buffered(1))

    n_case = len(NB_PATTERNS)
    interior = max(set(NB_TILE_CASE), key=NB_TILE_CASE.count)
    n_lo, n_hi = interior, n_case - 1 - interior
    case_of = lambda t: min(t, n_lo) + max(t - (n_tiles - 1 - n_hi), 0)
    assert [case_of(t) for t in range(n_tiles)] == NB_TILE_CASE

    def bias_idx(bi, i):
        return (layer, jnp.minimum(i, n_lo) + jnp.maximum(i - (n_tiles - 1 - n_hi), 0), 0, 0, 0)

    return pl.pallas_call(
        _attn_b_kernel,
        grid=(b, n_tiles),
        in_specs=[pl.BlockSpec((1, NB_TQ, Q_W), lambda bi, i: (bi, i, 0)),
                  res(SEQ), res(SEQ), res(CTX_LEN), res(CTX_LEN),
                  pl.BlockSpec((1, 1, N_HEADS // 2, NB_WIN, 2 * NB_TQ), bias_idx)],
        out_specs=pl.BlockSpec((1, NB_TQ, Q_W), lambda bi, i: (bi, i, 0)),
        out_shape=jax.ShapeDtypeStruct((b, SEQ, Q_W), BF16),
        scratch_shapes=[pltpu.VMEM((SEQ // NB_KEY_BLK, Q_W, NB_KEY_BLK), BF16),
                        pltpu.VMEM((Q_W, CTX_LEN), BF16),
                        pltpu.VMEM((NB_WIN + CTX_LEN, NB_TQ), F32),
                        pltpu.VMEM((NB_WIN + CTX_LEN, NB_TQ), F32),
                        pltpu.VMEM((1, NB_TQ), F32),
                        pltpu.VMEM((1, NB_TQ), F32)],
        compiler_params=_cparams(("parallel", "arbitrary")),
        name="attn_neighbourhood",
    )(q, k, v, kc, vc, bias)


def _attn_ctx_kernel(sink_ref, qa_ref, ka_ref, va_ref, qb_ref, kb_ref, vb_ref, qc_ref, kc_ref, vc_ref,
                     oa_ref, ob_ref, oc_ref):
    t = CTX_LEN
    for g in range(N_KV):
        for q_ref, k_ref, v_ref, o_ref, sink in ((qa_ref, ka_ref, va_ref, oa_ref, False),
                                                 (qc_ref, kc_ref, vc_ref, oc_ref, True)):
            qs = _stack_group_q(q_ref[0], g)
            extra = _sink_column(sink_ref, g, t) if sink else None
            o = _softmax_pv([_dot_nt(qs, k_ref[0])], [v_ref[0]], extra)
            chunks = _unstack_group_out(o, g, t)
            for cl in range(2):
                c = 2 * g + cl
                o_ref[0, :, c * LANES:(c + 1) * LANES] = chunks[cl].astype(BF16)
    for c in range(N_HEADS // 2):
        cols = slice(c * LANES, (c + 1) * LANES)
        o = _pair_attention(qb_ref[0, :, cols], lambda h, s_list: s_list,
                            [kb_ref[0, :, cols]], [vb_ref[0, :, cols]])
        ob_ref[0, :, cols] = o.astype(BF16)


def _attn_ctx(sink, qa, ka, va, qb, kb, vb, qc, kc, vc):
    b = qa.shape[0]
    spec = lambda w: pl.BlockSpec((1, CTX_LEN, w), lambda bi: (bi, 0, 0))
    return pl.pallas_call(
        _attn_ctx_kernel,
        grid=(b,),
        in_specs=[pl.BlockSpec(memory_space=pltpu.SMEM),
                  spec(Q_W), spec(KV_W), spec(KV_W), spec(Q_W), spec(Q_W), spec(Q_W),
                  spec(Q_W), spec(KV_W), spec(KV_W)],
        out_specs=[spec(Q_W)] * 3,
        out_shape=[jax.ShapeDtypeStruct((b, CTX_LEN, Q_W), BF16)] * 3,
        compiler_params=_cparams(("parallel",)),
        name="attn_ctx",
    )(sink, qa, ka, va, qb, kb, vb, qc, kc, vc)


def _out_proj_kernel(*refs, final):
    if final:
        ya_ref, yb_ref, yc_ref, sg_ref, x_ref, mod_ref, w_ref, fw_ref, o_ref = refs
    else:
        ya_ref, yb_ref, yc_ref, sg_ref, x_ref, mod_ref, w_ref, o_ref = refs
    acc = None
    for idx, y_ref in enumerate((ya_ref, yb_ref, yc_ref)):
        cols = slice(idx * Q_W, (idx + 1) * Q_W)
        u = (y_ref[0].astype(F32) * sg_ref[0, :, cols].astype(F32)).astype(BF16)
        part = _dot(u, w_ref[cols, :])
        acc = part if acc is None else acc + part
    xn = x_ref[0] + mod_ref[0, 2:3, :] * acc
    if final:
        xn = (xn * lax.rsqrt(jnp.mean(xn * xn, axis=-1, keepdims=True) + EPS)) * fw_ref[...]
    o_ref[0] = xn


def _out_proj(ya, yb, yc, sg, x, mod, w_bf, final_w, tm):
    b, n, _ = x.shape
    final = final_w is not None
    mod_idx = (lambda bi, i: (bi, 0, 0)) if mod.shape[0] > 1 else (lambda bi, i: (0, 0, 0))
    row = lambda w: pl.BlockSpec((1, tm, w), lambda bi, i: (bi, i, 0))
    in_specs = [row(Q_W), row(Q_W), row(Q_W), row(MIX_WIDTH), row(D_MODEL),
                pl.BlockSpec((1, 3, D_MODEL), mod_idx),
                pl.BlockSpec((MIX_WIDTH, D_MODEL), lambda bi, i: (0, 0))]
    args = [ya, yb, yc, sg, x, mod, w_bf]
    if final:
        in_specs.append(pl.BlockSpec((1, D_MODEL), lambda bi, i: (0, 0)))
        args.append(final_w)
    return pl.pallas_call(
        functools.partial(_out_proj_kernel, final=final),
        grid=(b, n // tm),
        in_specs=in_specs,
        out_specs=row(D_MODEL),
        out_shape=jax.ShapeDtypeStruct((b, n, D_MODEL), F32),
        compiler_params=_cparams(("parallel", "parallel")),
        name="out_proj_final" if final else "out_proj",
    )(*args)


def _rope_tables():
    t = jnp.arange(SEQ, dtype=jnp.int32)
    rows = (t // GRID_W).astype(F32)
    cols = (t % GRID_W).astype(F32)
    n_freq = HEAD_DIM // 4
    freq = ROPE_THETA ** (-jnp.arange(n_freq, dtype=F32) / n_freq)
    ang = jnp.concatenate([rows[:, None] * freq, cols[:, None] * freq], axis=-1)
    cos, sin = jnp.cos(ang), jnp.sin(ang)
    cos_t = jnp.tile(jnp.concatenate([cos, cos], axis=-1), (1, LANES // HEAD_DIM))
    sin_t = jnp.tile(jnp.concatenate([-sin, sin], axis=-1), (1, LANES // HEAD_DIM))
    return cos_t, sin_t


def kernel(x, c, ctx, c_ctx, norm_w, ada_w, ada_b, w_in, w_out, q_norm_a, k_norm_a, rpb_b, sink_c,
           final_norm_w):
    bsz = x.shape[0]
    cvecs = jnp.concatenate([c, c_ctx[None], jnp.zeros((8 - bsz - 1, D_MODEL), F32)], axis=0)
    mod = _modulation(cvecs, ada_w, ada_b).reshape(DEPTH, 8, 3, D_MODEL)
    bias = _nb_bias(rpb_b)
    rope_tabs = _rope_tables()
    w_in_bf = w_in.astype(BF16)
    w_out_bf = w_out.astype(BF16)
    cx = ctx
    for l in range(DEPTH):
        need_ctx = l < DEPTH - 1
        mod_x, mod_c = mod[l, :bsz], mod[l, bsz:bsz + 1]
        nw = norm_w[l][None]
        qn_t = jnp.tile(q_norm_a[l], N_HEADS)[None]
        kn_t = jnp.tile(k_norm_a[l], N_KV)[None]
        qa, ka, va, qb, kb, vb, qc, kc, vc, sg = _in_proj(
            x, mod_x, nw, w_in_bf[l], qn_t, kn_t, rope_tabs, tm=512)
        qa_c, ka_c, va_c, qb_c, kb_c, vb_c, qc_c, kc_c, vc_c, sg_c = _in_proj(
            cx, mod_c, nw, w_in_bf[l], qn_t, kn_t, None, tm=CTX_LEN)
        ya = _attn_a(qa, ka, va, ka_c, va_c)
        yb = _attn_b(qb, kb, vb, kb_c, vb_c, bias, l)
        yc = _attn_c(sink_c[l], qc, kc, vc, kc_c, vc_c)
        if need_ctx:
            ya_c, yb_c, yc_c = _attn_ctx(sink_c[l], qa_c, ka_c, va_c, qb_c, kb_c, vb_c, qc_c, kc_c, vc_c)
            cx = _out_proj(ya_c, yb_c, yc_c, sg_c, cx, mod_c, w_out_bf[l], None, tm=CTX_LEN)
        x = _out_proj(ya, yb, yc, sg, x, mod_x, w_out_bf[l],
                      None if need_ctx else final_norm_w[None], tm=512)
    return x
```

```python
import functools

import jax
import jax.numpy as jnp
from jax import lax
from jax.experimental import pallas as pl
from jax.experimental.pallas import tpu as pltpu

D_MODEL = 1024
SEQ = 8192
DEPTH = 2
GRID_W = 64
GRID_H = SEQ // GRID_W
CTX_LEN = 256
HEAD_DIM = 64
N_HEADS = 8
N_KV = 2
GROUP = N_HEADS // N_KV
Q_W = N_HEADS * HEAD_DIM
KV_W = N_KV * HEAD_DIM
MIX_WIDTH = 3 * Q_W
IN_WIDTH = 3 * Q_W + 2 * KV_W + 2 * Q_W + 2 * KV_W + MIX_WIDTH
NB_ROWS = 8
NB_COLS = 16
WINDOW = 128
ROPE_THETA = 10000.0
EPS = 1e-6
SCALE = HEAD_DIM ** -0.5
LOG2E = 1.4426950408889634
NEG = -1e30

LANES = 128
VMEM_LIMIT = 56 * 1024 * 1024

OFF_AQ, OFF_AK, OFF_AV = 0, 512, 640
OFF_BQ, OFF_BK, OFF_BV = 768, 1280, 1792
OFF_CQ, OFF_CK, OFF_CV = 2304, 2816, 2944
OFF_G = 3072

NB_Q_ROWS = 4
NB_TQ = NB_Q_ROWS * GRID_W
NB_WIN_ROWS = NB_Q_ROWS + NB_ROWS
NB_WIN = NB_WIN_ROWS * GRID_W
NB_KEY_BLK = 2 * GRID_W
N_REL = (2 * NB_ROWS - 1) * (2 * NB_COLS - 1)


def _nb_window_row(r0):
    return min(max(r0 - NB_ROWS // 2, 0), GRID_H - NB_WIN_ROWS)


def _nb_first_key_row(r):
    return min(max(r - NB_ROWS // 2, 0), GRID_H - NB_ROWS)


def _nb_bias_pattern(r0):
    w0 = _nb_window_row(r0)
    pat = []
    for qr in range(NB_Q_ROWS):
        rs = _nb_first_key_row(r0 + qr)
        pat.append(tuple((w0 + kr) - (r0 + qr) + NB_ROWS - 1 if rs <= w0 + kr < rs + NB_ROWS else None
                         for kr in range(NB_WIN_ROWS)))
    return tuple(pat)


def _nb_cases():
    patterns, tile_case = [], []
    for r0 in range(0, GRID_H, NB_Q_ROWS):
        pat = _nb_bias_pattern(r0)
        if pat not in patterns:
            patterns.append(pat)
        tile_case.append(patterns.index(pat))
    return patterns, tile_case


NB_PATTERNS, NB_TILE_CASE = _nb_cases()

F32 = jnp.float32
BF16 = jnp.bfloat16


def _cparams(sem):
    return pltpu.CompilerParams(dimension_semantics=sem, vmem_limit_bytes=VMEM_LIMIT)


def _dot(a, b):
    return jnp.dot(a, b, preferred_element_type=F32)


def _dot_nt(a, b):
    return lax.dot_general(a, b, (((1,), (1,)), ((), ())), preferred_element_type=F32)


def _transpose_bf16(z):
    return z.astype(F32).T.astype(BF16)


def _silu(x):
    return x * (1.0 / (1.0 + jnp.exp(-x)))


def _mod_kernel(c_ref, w_ref, b_ref, o_ref):
    cs = _silu(c_ref[...])
    o_ref[0] = jnp.dot(cs, w_ref[0], preferred_element_type=F32,
                       precision=lax.Precision.HIGHEST) + b_ref[0]


def _modulation(cvecs, ada_w, ada_b):
    tn = 1024
    return pl.pallas_call(
        _mod_kernel,
        grid=(DEPTH, 3 * D_MODEL // tn),
        in_specs=[
            pl.BlockSpec((8, D_MODEL), lambda l, j: (0, 0)),
            pl.BlockSpec((1, D_MODEL, tn), lambda l, j: (l, 0, j)),
            pl.BlockSpec((1, 1, tn), lambda l, j: (l, 0, j)),
        ],
        out_specs=pl.BlockSpec((1, 8, tn), lambda l, j: (l, 0, j)),
        out_shape=jax.ShapeDtypeStruct((DEPTH, 8, 3 * D_MODEL), F32),
        compiler_params=_cparams(("arbitrary", "arbitrary")),
        name="adaln_mod",
    )(cvecs, ada_w, ada_b.reshape(DEPTH, 1, 3 * D_MODEL))


def _head_rms(z, wn):
    w = z.shape[1]
    r = lax.broadcasted_iota(jnp.int32, (w, w), 0) // HEAD_DIM
    c = lax.broadcasted_iota(jnp.int32, (w, w), 1) // HEAD_DIM
    ones = (r == c).astype(BF16)
    ss = _dot((z * z).astype(BF16), ones)
    return (z * lax.rsqrt(ss * (1.0 / HEAD_DIM) + EPS)) * wn


def _rope(z, cos_t, sin_t):
    w = z.shape[1]
    reps = w // LANES
    if reps > 1:
        cos_t = jnp.concatenate([cos_t] * reps, axis=1)
        sin_t = jnp.concatenate([sin_t] * reps, axis=1)
    lane = lax.broadcasted_iota(jnp.int32, z.shape, 1)
    partner = jnp.where((lane & (HEAD_DIM // 2)) == 0,
                        pltpu.roll(z, w - HEAD_DIM // 2, 1), pltpu.roll(z, HEAD_DIM // 2, 1))
    return z * cos_t + partner * sin_t


def _in_proj_kernel(*refs, rope):
    if rope:
        (x_ref, mod_ref, nw_ref, w_ref, qn_ref, kn_ref, cos_ref, sin_ref,
         qa_ref, ka_ref, va_ref, qb_ref, kb_ref, vb_ref, qc_ref, kc_ref, vc_ref, sg_ref) = refs
        cos_t, sin_t = cos_ref[...], sin_ref[...]
        rot = lambda z: _rope(z, cos_t, sin_t)
    else:
        (x_ref, mod_ref, nw_ref, w_ref, qn_ref, kn_ref,
         qa_ref, ka_ref, va_ref, qb_ref, kb_ref, vb_ref, qc_ref, kc_ref, vc_ref, sg_ref) = refs
        rot = lambda z: z
    x = x_ref[0]
    y = (x * lax.rsqrt(jnp.mean(x * x, axis=-1, keepdims=True) + EPS)) * nw_ref[...]
    h = (y * (1.0 + mod_ref[0, 1:2, :]) + mod_ref[0, 0:1, :]).astype(BF16)

    def proj(off, width):
        return _dot(h, w_ref[:, off:off + width])

    q_scale = SCALE * LOG2E if rope else SCALE
    gate_w = MIX_WIDTH // 3
    groups = [
        (OFF_AQ, Q_W, qa_ref, lambda z: rot(_head_rms(z, qn_ref[...])) * q_scale),
        (OFF_AK, KV_W, ka_ref, lambda z: rot(_head_rms(z, kn_ref[...]))),
        (OFF_AV, KV_W, va_ref, lambda z: z),
        (OFF_BQ, Q_W, qb_ref, lambda z: z * q_scale),
        (OFF_BK, Q_W, kb_ref, lambda z: z),
        (OFF_BV, Q_W, vb_ref, lambda z: z),
        (OFF_CQ, Q_W, qc_ref, lambda z: rot(z) * q_scale),
        (OFF_CK, KV_W, kc_ref, lambda z: rot(z)),
        (OFF_CV, KV_W, vc_ref, lambda z: z),
    ] + [(OFF_G + t * gate_w, gate_w, (sg_ref, t * gate_w), _silu) for t in range(3)]
    z = proj(groups[0][0], groups[0][1])
    for gi, (off, width, dst, epilogue) in enumerate(groups):
        if gi + 1 < len(groups):
            z_next = proj(groups[gi + 1][0], groups[gi + 1][1])
        out = epilogue(z).astype(BF16)
        if isinstance(dst, tuple):
            dst[0][0, :, dst[1]:dst[1] + width] = out
        else:
            dst[0] = out
        z = z_next


def _in_proj(x, mod, nw, w_bf, qn_t, kn_t, rope_tabs, tm):
    b, n, _ = x.shape
    rope = rope_tabs is not None
    mod_idx = (lambda bi, i: (bi, 0, 0)) if mod.shape[0] > 1 else (lambda bi, i: (0, 0, 0))
    const2 = lambda bi, i: (0, 0)
    in_specs = [
        pl.BlockSpec((1, tm, D_MODEL), lambda bi, i: (bi, i, 0)),
        pl.BlockSpec((1, 3, D_MODEL), mod_idx),
        pl.BlockSpec((1, D_MODEL), const2),
        pl.BlockSpec((D_MODEL, IN_WIDTH), const2, pipeline_mode=pl.Buffered(1)),
        pl.BlockSpec((1, Q_W), const2),
        pl.BlockSpec((1, KV_W), const2),
    ]
    args = [x, mod, nw, w_bf, qn_t, kn_t]
    if rope:
        in_specs += [pl.BlockSpec((tm, LANES), lambda bi, i: (i, 0))] * 2
        args += list(rope_tabs)
    widths = (Q_W, KV_W, KV_W, Q_W, Q_W, Q_W, Q_W, KV_W, KV_W, MIX_WIDTH)
    out_specs = [pl.BlockSpec((1, tm, w), lambda bi, i: (bi, i, 0)) for w in widths]
    out_shape = [jax.ShapeDtypeStruct((b, n, w), BF16) for w in widths]
    return pl.pallas_call(
        functools.partial(_in_proj_kernel, rope=rope),
        grid=(b, n // tm),
        in_specs=in_specs,
        out_specs=out_specs,
        out_shape=out_shape,
        compiler_params=_cparams(("parallel", "parallel")),
        name="in_proj_rope" if rope else "in_proj_ctx",
    )(*args)


def _stack_group_q(q, g):
    t = q.shape[0]
    lane = lax.broadcasted_iota(jnp.int32, (t, LANES), 1)
    keep = (lane // HEAD_DIM) == g
    parts = []
    for j in range(GROUP):
        h = GROUP * g + j
        chunk = q[:, (h // 2) * LANES:(h // 2 + 1) * LANES].astype(F32)
        if h % 2 != g:
            chunk = pltpu.roll(chunk, HEAD_DIM, 1)
        parts.append(jnp.where(keep, chunk, 0.0).astype(BF16))
    return jnp.concatenate(parts, axis=0)


def _unstack_group_out(o, g, t):
    lane = lax.broadcasted_iota(jnp.int32, (t, LANES), 1)
    left = lane < HEAD_DIM
    chunks = []
    for cl in range(2):
        even = o[(2 * cl) * t:(2 * cl + 1) * t]
        odd = o[(2 * cl + 1) * t:(2 * cl + 2) * t]
        if g == 0:
            odd = pltpu.roll(odd, HEAD_DIM, 1)
        else:
            even = pltpu.roll(even, HEAD_DIM, 1)
        chunks.append(jnp.where(left, even, odd))
    return chunks


def _softmax_pv(s_list, v_list, extra_logit=None):
    m = s_list[0].max(axis=1, keepdims=True)
    for s in s_list[1:]:
        m = jnp.maximum(m, s.max(axis=1, keepdims=True))
    if extra_logit is not None:
        m = jnp.maximum(m, extra_logit)
    l = None
    o = None
    for s, v in zip(s_list, v_list):
        p = jnp.exp(s - m)
        ps = p.sum(axis=1, keepdims=True)
        po = _dot(p.astype(BF16), v)
        l = ps if l is None else l + ps
        o = po if o is None else o + po
    if extra_logit is not None:
        l = l + jnp.exp(extra_logit - m)
    return o / l


def _sink_column(sink_ref, g, t):
    row = lax.broadcasted_iota(jnp.int32, (GROUP * t, 1), 0) // t
    col = jnp.full((GROUP * t, 1), sink_ref[GROUP * g], F32)
    for j in range(1, GROUP):
        col = jnp.where(row == j, sink_ref[GROUP * g + j], col)
    return col


ONES_ROWS = 16


def _attn_a_kernel(q_ref, k_ref, v_ref, kc_ref, vc_ref, o_ref, vt_ref, vct_ref, qs_ref, m_ref, acc_ref,
                   s0_ref, s1_ref, s2_ref, mx0_ref, mx1_ref, mx2_ref, *, tq, tk, col_blk):
    n_chunks = SEQ // tk

    @pl.when(pl.program_id(1) == 0)
    def _():
        for j in range(SEQ // tk):
            vt_ref[j] = _transpose_bf16(v_ref[0, j * tk:(j + 1) * tk, :])
        vct_ref[...] = _transpose_bf16(vc_ref[0])

    col_blocks = [slice(c0, c0 + col_blk) for c0 in range(0, GROUP * tq, col_blk)]
    bufs = ((s0_ref, mx0_ref), (s1_ref, mx1_ref), (s2_ref, mx2_ref))

    def score_block(k, buf, cols):
        s_ref, mx_ref = bufs[buf]
        s = _dot(k, qs_ref[:, cols])
        s_ref[0:k.shape[0], cols] = s
        mx_ref[:, cols] = s.max(axis=0, keepdims=True)

    def absorb_block(n, buf, v_ext, cols):
        s_ref, mx_ref = bufs[buf]
        m_prev = m_ref[:, cols]
        m_new = jnp.maximum(m_prev, mx_ref[:, cols])
        alpha = jnp.exp2(m_prev - m_new)
        p = jnp.exp2(s_ref[0:n, cols] - m_new).astype(BF16)
        acc_ref[:, cols] = alpha * acc_ref[:, cols] + _dot(v_ext, p)
        m_ref[:, cols] = m_new

    def latent_keys(j):
        return k_ref[0, pl.ds(pl.multiple_of(j * tk, tk), tk), :]

    def trip(n_cur, cur, vt_g, k_next, nxt):
        v_ext = _with_ones_rows(vt_g)
        for cols in col_blocks:
            if k_next is not None:
                score_block(k_next, nxt, cols)
            absorb_block(n_cur, cur, v_ext, cols)

    assert n_chunks % 3 == 1
    qt = _transpose_bf16(q_ref[0])
    for g in range(N_KV):
        rows = slice(g * HEAD_DIM, (g + 1) * HEAD_DIM)
        qs_ref[...] = jnp.zeros(qs_ref.shape, BF16)
        for j in range(GROUP):
            h = GROUP * g + j
            qs_ref[rows, j * tq:(j + 1) * tq] = qt[h * HEAD_DIM:(h + 1) * HEAD_DIM, :]
        m_ref[...] = jnp.full(m_ref.shape, NEG, F32)
        acc_ref[...] = jnp.zeros(acc_ref.shape, F32)

        kc = kc_ref[0]
        for cols in col_blocks:
            score_block(kc, 2, cols)
        trip(CTX_LEN, 2, vct_ref[rows, :], latent_keys(0), 0)

        def body(i, carry, rows=rows):
            j = 3 * i
            trip(tk, 0, vt_ref[j, rows, :], latent_keys(j + 1), 1)
            trip(tk, 1, vt_ref[j + 1, rows, :], latent_keys(j + 2), 2)
            trip(tk, 2, vt_ref[j + 2, rows, :], latent_keys(j + 3), 0)
            return carry

        lax.fori_loop(0, n_chunks // 3, body, 0)
        trip(tk, 0, vt_ref[n_chunks - 1, rows, :], None, None)
        o = acc_ref[0:HEAD_DIM, :] / acc_ref[HEAD_DIM:HEAD_DIM + 1, :]
        for cl in range(GROUP // 2):
            pair = jnp.concatenate([o[:, (2 * cl) * tq:(2 * cl + 1) * tq],
                                    o[:, (2 * cl + 1) * tq:(2 * cl + 2) * tq]], axis=0)
            cc = (GROUP // 2) * g + cl
            o_ref[0, :, cc * LANES:(cc + 1) * LANES] = _transpose_bf16(pair)


def _attn_a(q, k, v, kc, vc, tq=512, tk=512, col_blk=256):
    b = q.shape[0]
    res = lambda n: pl.BlockSpec((1, n, KV_W), lambda bi, i: (bi, 0, 0))
    return pl.pallas_call(
        functools.partial(_attn_a_kernel, tq=tq, tk=tk, col_blk=col_blk),
        grid=(b, SEQ // tq),
        in_specs=[pl.BlockSpec((1, tq, Q_W), lambda bi, i: (bi, i, 0)),
                  res(SEQ), res(SEQ), res(CTX_LEN), res(CTX_LEN)],
        out_specs=pl.BlockSpec((1, tq, Q_W), lambda bi, i: (bi, i, 0)),
        out_shape=jax.ShapeDtypeStruct((b, SEQ, Q_W), BF16),
        scratch_shapes=[pltpu.VMEM((SEQ // tk, KV_W, tk), BF16),
                        pltpu.VMEM((KV_W, CTX_LEN), BF16),
                        pltpu.VMEM((LANES, GROUP * tq), BF16),
                        pltpu.VMEM((1, GROUP * tq), F32),
                        pltpu.VMEM((HEAD_DIM + ONES_ROWS, GROUP * tq), F32),
                        pltpu.VMEM((tk, GROUP * tq), F32),
                        pltpu.VMEM((tk, GROUP * tq), F32),
                        pltpu.VMEM((tk, GROUP * tq), F32),
                        pltpu.VMEM((1, GROUP * tq), F32),
                        pltpu.VMEM((1, GROUP * tq), F32),
                        pltpu.VMEM((1, GROUP * tq), F32)],
        compiler_params=_cparams(("parallel", "arbitrary")),
        name="attn_global",
    )(q, k, v, kc, vc)


def _fill_feature_major(v_ref, vt_ref, blk):
    def fill(j, carry):
        off = pl.multiple_of(j * blk, blk)
        vt_ref[j] = _transpose_bf16(v_ref[0, pl.ds(off, blk), :])
        return carry

    lax.fori_loop(0, v_ref.shape[1] // blk, fill, 0)


def _with_ones_rows(vt):
    return jnp.concatenate([vt, jnp.ones((ONES_ROWS, vt.shape[1]), BF16)], axis=0)


def _attn_c_kernel(sink_ref, q_ref, k_ref, v_ref, kc_ref, vc_ref, o_ref, vt_ref, vct_ref,
                   qs0_ref, qs1_ref, s0_ref, s1_ref, mx0_ref, mx1_ref, band_ref, *, tq):
    win = tq + 2 * WINDOW
    n_keys = win + CTX_LEN
    i = pl.program_id(1)

    @pl.when(i == 0)
    def _():
        _fill_feature_major(v_ref, vt_ref, LANES)
        vct_ref[...] = _transpose_bf16(vc_ref[0])

    ws = pl.multiple_of(jnp.clip(i * tq - WINDOW, 0, SEQ - win), LANES)
    blk0 = ws // LANES
    kw = k_ref[0, pl.ds(ws, win), :]
    kc = kc_ref[0]
    kpos = ws + lax.broadcasted_iota(jnp.int32, (win, tq), 0)
    qpos = i * tq + lax.broadcasted_iota(jnp.int32, (win, tq), 1)
    band_ref[...] = jnp.where(jnp.abs(qpos - kpos) <= WINDOW, 0.0, NEG)
    qt = _transpose_bf16(q_ref[0])
    bufs = ((qs0_ref, s0_ref, mx0_ref), (qs1_ref, s1_ref, mx1_ref))
    col_blocks = [slice(j * tq, (j + 1) * tq) for j in range(GROUP)]

    def stack_queries(g):
        qs_ref = bufs[g][0]
        qs_ref[...] = jnp.zeros(qs_ref.shape, BF16)
        for j in range(GROUP):
            h = GROUP * g + j
            qs_ref[g * HEAD_DIM:(g + 1) * HEAD_DIM, j * tq:(j + 1) * tq] = (
                qt[h * HEAD_DIM:(h + 1) * HEAD_DIM, :])

    def score_block(g, j):
        qs_ref, s_ref, mx_ref = bufs[g]
        cols = col_blocks[j]
        s_win = _dot(kw, qs_ref[:, cols]) + band_ref[...]
        s_ctx = _dot(kc, qs_ref[:, cols])
        s_ref[0:win, cols] = s_win
        s_ref[win:n_keys, cols] = s_ctx
        sink2 = sink_ref[GROUP * g + j] * LOG2E
        mx_ref[:, cols] = jnp.maximum(
            jnp.maximum(s_win.max(axis=0, keepdims=True), s_ctx.max(axis=0, keepdims=True)), sink2)

    def values(g):
        rows = slice(g * HEAD_DIM, (g + 1) * HEAD_DIM)
        blocks = [vt_ref[blk0 + t, rows, :] for t in range(win // LANES)] + [vct_ref[rows, :]]
        return _with_ones_rows(jnp.concatenate(blocks, axis=1))

    def out_block(g, j, v_ext):
        _, s_ref, mx_ref = bufs[g]
        cols = col_blocks[j]
        m = mx_ref[:, cols]
        p = jnp.exp2(s_ref[:, cols] - m).astype(BF16)
        acc = _dot(v_ext, p)
        l = acc[HEAD_DIM:HEAD_DIM + 1, :] + jnp.exp2(sink_ref[GROUP * g + j] * LOG2E - m)
        return acc[0:HEAD_DIM, :] / l

    def store_pair(g, cl, o_even, o_odd):
        c = (GROUP // 2) * g + cl
        o_ref[0, :, c * LANES:(c + 1) * LANES] = _transpose_bf16(
            jnp.concatenate([o_even, o_odd], axis=0))

    stack_queries(0)
    stack_queries(1)
    for j in range(GROUP):
        score_block(0, j)
    v_ext = values(0)
    outs = []
    for j in range(GROUP):
        score_block(1, j)
        outs.append(out_block(0, j, v_ext))
        if j % 2 == 1:
            store_pair(0, j // 2, outs[j - 1], outs[j])
    v_ext = values(1)
    outs = []
    for j in range(GROUP):
        outs.append(out_block(1, j, v_ext))
        if j % 2 == 1:
            store_pair(1, j // 2, outs[j - 1], outs[j])


def _attn_c(sink, q, k, v, kc, vc, tq=256):
    b = q.shape[0]
    n_keys = tq + 2 * WINDOW + CTX_LEN
    res = lambda n: pl.BlockSpec((1, n, KV_W), lambda bi, i: (bi, 0, 0))
    return pl.pallas_call(
        functools.partial(_attn_c_kernel, tq=tq),
        grid=(b, SEQ // tq),
        in_specs=[pl.BlockSpec(memory_space=pltpu.SMEM),
                  pl.BlockSpec((1, tq, Q_W), lambda bi, i: (bi, i, 0)),
                  res(SEQ), res(SEQ), res(CTX_LEN), res(CTX_LEN)],
        out_specs=pl.BlockSpec((1, tq, Q_W), lambda bi, i: (bi, i, 0)),
        out_shape=jax.ShapeDtypeStruct((b, SEQ, Q_W), BF16),
        scratch_shapes=[pltpu.VMEM((SEQ // LANES, KV_W, LANES), BF16),
                        pltpu.VMEM((KV_W, CTX_LEN), BF16),
                        pltpu.VMEM((LANES, GROUP * tq), BF16),
                        pltpu.VMEM((LANES, GROUP * tq), BF16),
                        pltpu.VMEM((n_keys, GROUP * tq), F32),
                        pltpu.VMEM((n_keys, GROUP * tq), F32),
                        pltpu.VMEM((1, GROUP * tq), F32),
                        pltpu.VMEM((1, GROUP * tq), F32),
                        pltpu.VMEM((tq + 2 * WINDOW, tq), F32)],
        compiler_params=_cparams(("parallel", "arbitrary")),
        name="attn_window",
    )(sink, q, k, v, kc, vc)


def _nb_bias_kernel(rpb_ref, o_ref, tt_ref):
    ck = lax.broadcasted_iota(jnp.int32, (GRID_W, LANES), 0)
    lane = lax.broadcasted_iota(jnp.int32, (GRID_W, LANES), 1)
    cq = lane & (GRID_W - 1)
    d = ck - cq + (NB_COLS - 1)
    cs = jnp.clip(cq - NB_COLS // 2, 0, GRID_W - NB_COLS)
    col_ok = (ck >= cs) & (ck < cs + NB_COLS)
    n_dc = 2 * NB_COLS - 1
    neg_tile = jnp.full((GRID_W, LANES), NEG, F32)
    even_row = lane < GRID_W
    for h in range(2):
        base = ((pl.program_id(0) * (N_HEADS // 2) + pl.program_id(1)) * 2 + h) * N_REL
        for a in range(2 * NB_ROWS - 1):
            t = neg_tile
            for dd in range(n_dc):
                t = jnp.where(d == dd, rpb_ref[base + a * n_dc + dd] * LOG2E, t)
            tt_ref[a] = jnp.where(col_ok, t, NEG)
        for case, pat in enumerate(NB_PATTERNS):
            for kr in range(NB_WIN_ROWS):
                for qv in range(NB_Q_ROWS // 2):
                    a0, a1 = pat[2 * qv][kr], pat[2 * qv + 1][kr]
                    lo = neg_tile if a0 is None else tt_ref[a0]
                    hi = neg_tile if a1 is None else tt_ref[a1]
                    c0 = h * NB_TQ + qv * LANES
                    o_ref[0, case, 0, kr * GRID_W:(kr + 1) * GRID_W, c0:c0 + LANES] = (
                        jnp.where(even_row, lo, hi))


def _nb_bias(rpb):
    n_case = len(NB_PATTERNS)
    return pl.pallas_call(
        _nb_bias_kernel,
        grid=(DEPTH, N_HEADS // 2),
        in_specs=[pl.BlockSpec(memory_space=pltpu.SMEM)],
        out_specs=pl.BlockSpec((1, n_case, 1, NB_WIN, 2 * NB_TQ), lambda l, c: (l, 0, c, 0, 0)),
        out_shape=jax.ShapeDtypeStruct((DEPTH, n_case, N_HEADS // 2, NB_WIN, 2 * NB_TQ), F32),
        scratch_shapes=[pltpu.VMEM((2 * NB_ROWS - 1, GRID_W, LANES), F32)],
        compiler_params=_cparams(("arbitrary", "arbitrary")),
        name="nb_bias",
    )(rpb.reshape(-1))


def _pair_attention(qp, s_extra_fn, k_list, v_list):
    t = qp.shape[0]
    lane = lax.broadcasted_iota(jnp.int32, (t, LANES), 1)
    qf = qp.astype(F32)
    outs = []
    for h in range(2):
        qpad = jnp.where((lane // HEAD_DIM) == h, qf, 0.0).astype(BF16)
        s_list = [_dot_nt(qpad, k) for k in k_list]
        s_list = s_extra_fn(h, s_list)
        outs.append(_softmax_pv(s_list, v_list))
    return jnp.where(lane < HEAD_DIM, outs[0], outs[1])


def _attn_b_kernel(q_ref, k_ref, v_ref, kc_ref, vc_ref, bias_ref, o_ref, vt_ref, vct_ref,
                   s0_ref, s1_ref, mx0_ref, mx1_ref):
    i = pl.program_id(1)

    @pl.when(i == 0)
    def _():
        _fill_feature_major(v_ref, vt_ref, NB_KEY_BLK)
        vct_ref[...] = _transpose_bf16(vc_ref[0])

    w0 = jnp.clip(NB_Q_ROWS * i - NB_ROWS // 2, 0, GRID_H - NB_WIN_ROWS)
    off = pl.multiple_of(w0 * GRID_W, NB_KEY_BLK)
    blk0 = w0 // 2
    bufs = ((s0_ref, mx0_ref), (s1_ref, mx1_ref))
    zeros = jnp.zeros((NB_TQ, LANES), BF16)
    head0_lanes = lax.broadcasted_iota(jnp.int32, (NB_TQ, LANES), 1) < HEAD_DIM

    def score_stage(h):
        s_ref, mx_ref = bufs[h % 2]
        c, t = divmod(h, 2)
        cols = slice(c * LANES, (c + 1) * LANES)
        qp = q_ref[0, :, cols]
        qpad = jnp.where(head0_lanes, qp, zeros) if t == 0 else jnp.where(head0_lanes, zeros, qp)
        s_win = (_dot_nt(k_ref[0, pl.ds(off, NB_WIN), cols], qpad)
                 + bias_ref[0, 0, c, :, t * NB_TQ:(t + 1) * NB_TQ])
        s_ctx = _dot_nt(kc_ref[0, :, cols], qpad)
        s_ref[0:NB_WIN, :] = s_win
        s_ref[NB_WIN:NB_WIN + CTX_LEN, :] = s_ctx
        mx_ref[...] = jnp.maximum(s_win.max(axis=0, keepdims=True), s_ctx.max(axis=0, keepdims=True))

    def pv_stage(h):
        s_ref, mx_ref = bufs[h % 2]
        rows = slice(h * HEAD_DIM, (h + 1) * HEAD_DIM)
        p = jnp.exp2(s_ref[...] - mx_ref[...]).astype(BF16)
        blocks = [vt_ref[blk0 + t, rows, :] for t in range(NB_WIN // NB_KEY_BLK)] + [vct_ref[rows, :]]
        acc = _dot(_with_ones_rows(jnp.concatenate(blocks, axis=1)), p)
        return acc[0:HEAD_DIM, :] / acc[HEAD_DIM:HEAD_DIM + 1, :]

    score_stage(0)
    outs = []
    for h in range(N_HEADS):
        if h + 1 < N_HEADS:
            score_stage(h + 1)
        outs.append(pv_stage(h))
        if h % 2 == 1:
            o_ref[0, :, (h // 2) * LANES:(h // 2 + 1) * LANES] = _transpose_bf16(
                jnp.concatenate([outs[h - 1], outs[h]], axis=0))


def _attn_b(q, k, v, kc, vc, bias, layer):
    b = q.shape[0]
    n_tiles = SEQ // NB_TQ
    res = lambda n: pl.BlockSpec((1, n, Q_W), lambda bi, i: (bi, 0, 0), pipeline_mode=pl.Buffered(1))

    n_case = len(NB_PATTERNS)
    interior = max(set(NB_TILE_CASE), key=NB_TILE_CASE.count)
    n_lo, n_hi = interior, n_case - 1 - interior
    case_of = lambda t: min(t, n_lo) + max(t - (n_tiles - 1 - n_hi), 0)
    assert [case_of(t) for t in range(n_tiles)] == NB_TILE_CASE

    def bias_idx(bi, i):
        return (layer, jnp.minimum(i, n_lo) + jnp.maximum(i - (n_tiles - 1 - n_hi), 0), 0, 0, 0)

    return pl.pallas_call(
        _attn_b_kernel,
        grid=(b, n_tiles),
        in_specs=[pl.BlockSpec((1, NB_TQ, Q_W), lambda bi, i: (bi, i, 0)),
                  res(SEQ), res(SEQ), res(CTX_LEN), res(CTX_LEN),
                  pl.BlockSpec((1, 1, N_HEADS // 2, NB_WIN, 2 * NB_TQ), bias_idx)],
        out_specs=pl.BlockSpec((1, NB_TQ, Q_W), lambda bi, i: (bi, i, 0)),
        out_shape=jax.ShapeDtypeStruct((b, SEQ, Q_W), BF16),
        scratch_shapes=[pltpu.VMEM((SEQ // NB_KEY_BLK, Q_W, NB_KEY_BLK), BF16),
                        pltpu.VMEM((Q_W, CTX_LEN), BF16),
                        pltpu.VMEM((NB_WIN + CTX_LEN, NB_TQ), F32),
                        pltpu.VMEM((NB_WIN + CTX_LEN, NB_TQ), F32),
                        pltpu.VMEM((1, NB_TQ), F32),
                        pltpu.VMEM((1, NB_TQ), F32)],
        compiler_params=_cparams(("parallel", "arbitrary")),
        name="attn_neighbourhood",
    )(q, k, v, kc, vc, bias)


def _attn_ctx_kernel(sink_ref, qa_ref, ka_ref, va_ref, qb_ref, kb_ref, vb_ref, qc_ref, kc_ref, vc_ref,
                     oa_ref, ob_ref, oc_ref):
    t = CTX_LEN
    for g in range(N_KV):
        for q_ref, k_ref, v_ref, o_ref, sink in ((qa_ref, ka_ref, va_ref, oa_ref, False),
                                                 (qc_ref, kc_ref, vc_ref, oc_ref, True)):
            qs = _stack_group_q(q_ref[0], g)
            extra = _sink_column(sink_ref, g, t) if sink else None
            o = _softmax_pv([_dot_nt(qs, k_ref[0])], [v_ref[0]], extra)
            chunks = _unstack_group_out(o, g, t)
            for cl in range(2):
                c = 2 * g + cl
                o_ref[0, :, c * LANES:(c + 1) * LANES] = chunks[cl].astype(BF16)
    for c in range(N_HEADS // 2):
        cols = slice(c * LANES, (c + 1) * LANES)
        o = _pair_attention(qb_ref[0, :, cols], lambda h, s_list: s_list,
                            [kb_ref[0, :, cols]], [vb_ref[0, :, cols]])
        ob_ref[0, :, cols] = o.astype(BF16)


def _attn_ctx(sink, qa, ka, va, qb, kb, vb, qc, kc, vc):
    b = qa.shape[0]
    spec = lambda w: pl.BlockSpec((1, CTX_LEN, w), lambda bi: (bi, 0, 0))
    return pl.pallas_call(
        _attn_ctx_kernel,
        grid=(b,),
        in_specs=[pl.BlockSpec(memory_space=pltpu.SMEM),
                  spec(Q_W), spec(KV_W), spec(KV_W), spec(Q_W), spec(Q_W), spec(Q_W),
                  spec(Q_W), spec(KV_W), spec(KV_W)],
        out_specs=[spec(Q_W)] * 3,
        out_shape=[jax.ShapeDtypeStruct((b, CTX_LEN, Q_W), BF16)] * 3,
        compiler_params=_cparams(("parallel",)),
        name="attn_ctx",
    )(sink, qa, ka, va, qb, kb, vb, qc, kc, vc)


def _out_proj_kernel(*refs, final):
    if final:
        ya_ref, yb_ref, yc_ref, sg_ref, x_ref, mod_ref, w_ref, fw_ref, o_ref = refs
    else:
        ya_ref, yb_ref, yc_ref, sg_ref, x_ref, mod_ref, w_ref, o_ref = refs
    acc = None
    for idx, y_ref in enumerate((ya_ref, yb_ref, yc_ref)):
        cols = slice(idx * Q_W, (idx + 1) * Q_W)
        u = (y_ref[0].astype(F32) * sg_ref[0, :, cols].astype(F32)).astype(BF16)
        part = _dot(u, w_ref[cols, :])
        acc = part if acc is None else acc + part
    xn = x_ref[0] + mod_ref[0, 2:3, :] * acc
    if final:
        xn = (xn * lax.rsqrt(jnp.mean(xn * xn, axis=-1, keepdims=True) + EPS)) * fw_ref[...]
    o_ref[0] = xn


def _out_proj(ya, yb, yc, sg, x, mod, w_bf, final_w, tm):
    b, n, _ = x.shape
    final = final_w is not None
    mod_idx = (lambda bi, i: (bi, 0, 0)) if mod.shape[0] > 1 else (lambda bi, i: (0, 0, 0))
    row = lambda w: pl.BlockSpec((1, tm, w), lambda bi, i: (bi, i, 0))
    in_specs = [row(Q_W), row(Q_W), row(Q_W), row(MIX_WIDTH), row(D_MODEL),
                pl.BlockSpec((1, 3, D_MODEL), mod_idx),
                pl.BlockSpec((MIX_WIDTH, D_MODEL), lambda bi, i: (0, 0))]
    args = [ya, yb, yc, sg, x, mod, w_bf]
    if final:
        in_specs.append(pl.BlockSpec((1, D_MODEL), lambda bi, i: (0, 0)))
        args.append(final_w)
    return pl.pallas_call(
        functools.partial(_out_proj_kernel, final=final),
        grid=(b, n // tm),
        in_specs=in_specs,
        out_specs=row(D_MODEL),
        out_shape=jax.ShapeDtypeStruct((b, n, D_MODEL), F32),
        compiler_params=_cparams(("parallel", "parallel")),
        name="out_proj_final" if final else "out_proj",
    )(*args)


def _rope_tables():
    t = jnp.arange(SEQ, dtype=jnp.int32)
    rows = (t // GRID_W).astype(F32)
    cols = (t % GRID_W).astype(F32)
    n_freq = HEAD_DIM // 4
    freq = ROPE_THETA ** (-jnp.arange(n_freq, dtype=F32) / n_freq)
    ang = jnp.concatenate([rows[:, None] * freq, cols[:, None] * freq], axis=-1)
    cos, sin = jnp.cos(ang), jnp.sin(ang)
    cos_t = jnp.tile(jnp.concatenate([cos, cos], axis=-1), (1, LANES // HEAD_DIM))
    sin_t = jnp.tile(jnp.concatenate([-sin, sin], axis=-1), (1, LANES // HEAD_DIM))
    return cos_t, sin_t


def kernel(x, c, ctx, c_ctx, norm_w, ada_w, ada_b, w_in, w_out, q_norm_a, k_norm_a, rpb_b, sink_c,
           final_norm_w):
    bsz = x.shape[0]
    cvecs = jnp.concatenate([c, c_ctx[None], jnp.zeros((8 - bsz - 1, D_MODEL), F32)], axis=0)
    mod = _modulation(cvecs, ada_w, ada_b).reshape(DEPTH, 8, 3, D_MODEL)
    bias = _nb_bias(rpb_b)
    rope_tabs = _rope_tables()
    w_in_bf = w_in.astype(BF16)
    w_out_bf = w_out.astype(BF16)
    cx = ctx
    for l in range(DEPTH):
        need_ctx = l < DEPTH - 1
        mod_x, mod_c = mod[l, :bsz], mod[l, bsz:bsz + 1]
        nw = norm_w[l][None]
        qn_t = jnp.tile(q_norm_a[l], N_HEADS)[None]
        kn_t = jnp.tile(k_norm_a[l], N_KV)[None]
        qa, ka, va, qb, kb, vb, qc, kc, vc, sg = _in_proj(
            x, mod_x, nw, w_in_bf[l], qn_t, kn_t, rope_tabs, tm=512)
        qa_c, ka_c, va_c, qb_c, kb_c, vb_c, qc_c, kc_c, vc_c, sg_c = _in_proj(
            cx, mod_c, nw, w_in_bf[l], qn_t, kn_t, None, tm=CTX_LEN)
        ya = _attn_a(qa, ka, va, ka_c, va_c)
        yb = _attn_b(qb, kb, vb, kb_c, vb_c, bias, l)
        yc = _attn_c(sink_c[l], qc, kc, vc, kc_c, vc_c)
        if need_ctx:
            ya_c, yb_c, yc_c = _attn_ctx(sink_c[l], qa_c, ka_c, va_c, qb_c, kb_c, vb_c, qc_c, kc_c, vc_c)
            cx = _out_proj(ya_c, yb_c, yc_c, sg_c, cx, mod_c, w_out_bf[l], None, tm=CTX_LEN)
        x = _out_proj(ya, yb, yc, sg, x, mod_x, w_out_bf[l],
                      None if need_ctx else final_norm_w[None], tm=1024)
    return x
```

```python
import functools

import jax
import jax.numpy as jnp
from jax import lax
from jax.experimental import pallas as pl
from jax.experimental.pallas import tpu as pltpu

D_MODEL = 1024
SEQ = 8192
DEPTH = 2
GRID_W = 64
GRID_H = SEQ // GRID_W
CTX_LEN = 256
HEAD_DIM = 64
N_HEADS = 8
N_KV = 2
GROUP = N_HEADS // N_KV
Q_W = N_HEADS * HEAD_DIM
KV_W = N_KV * HEAD_DIM
MIX_WIDTH = 3 * Q_W
IN_WIDTH = 3 * Q_W + 2 * KV_W + 2 * Q_W + 2 * KV_W + MIX_WIDTH
NB_ROWS = 8
NB_COLS = 16
WINDOW = 128
ROPE_THETA = 10000.0
EPS = 1e-6
SCALE = HEAD_DIM ** -0.5
LOG2E = 1.4426950408889634
NEG = -1e30

LANES = 128
VMEM_LIMIT = 56 * 1024 * 1024

OFF_AQ, OFF_AK, OFF_AV = 0, 512, 640
OFF_BQ, OFF_BK, OFF_BV = 768, 1280, 1792
OFF_CQ, OFF_CK, OFF_CV = 2304, 2816, 2944
OFF_G = 3072

NB_Q_ROWS = 4
NB_TQ = NB_Q_ROWS * GRID_W
NB_WIN_ROWS = NB_Q_ROWS + NB_ROWS
NB_WIN = NB_WIN_ROWS * GRID_W
NB_KEY_BLK = 2 * GRID_W
N_REL = (2 * NB_ROWS - 1) * (2 * NB_COLS - 1)


def _nb_window_row(r0):
    return min(max(r0 - NB_ROWS // 2, 0), GRID_H - NB_WIN_ROWS)


def _nb_first_key_row(r):
    return min(max(r - NB_ROWS // 2, 0), GRID_H - NB_ROWS)


def _nb_bias_pattern(r0):
    w0 = _nb_window_row(r0)
    pat = []
    for qr in range(NB_Q_ROWS):
        rs = _nb_first_key_row(r0 + qr)
        pat.append(tuple((w0 + kr) - (r0 + qr) + NB_ROWS - 1 if rs <= w0 + kr < rs + NB_ROWS else None
                         for kr in range(NB_WIN_ROWS)))
    return tuple(pat)


def _nb_cases():
    patterns, tile_case = [], []
    for r0 in range(0, GRID_H, NB_Q_ROWS):
        pat = _nb_bias_pattern(r0)
        if pat not in patterns:
            patterns.append(pat)
        tile_case.append(patterns.index(pat))
    return patterns, tile_case


NB_PATTERNS, NB_TILE_CASE = _nb_cases()

F32 = jnp.float32
BF16 = jnp.bfloat16


def _cparams(sem):
    return pltpu.CompilerParams(dimension_semantics=sem, vmem_limit_bytes=VMEM_LIMIT)


def _dot(a, b):
    return jnp.dot(a, b, preferred_element_type=F32)


def _dot_nt(a, b):
    return lax.dot_general(a, b, (((1,), (1,)), ((), ())), preferred_element_type=F32)


def _transpose_bf16(z):
    return z.astype(F32).T.astype(BF16)


def _silu(x):
    return x * (1.0 / (1.0 + jnp.exp(-x)))


def _mod_kernel(c_ref, w_ref, b_ref, o_ref):
    cs = _silu(c_ref[...])
    o_ref[0] = jnp.dot(cs, w_ref[0], preferred_element_type=F32,
                       precision=lax.Precision.HIGHEST) + b_ref[0]


def _modulation(cvecs, ada_w, ada_b):
    tn = 1024
    return pl.pallas_call(
        _mod_kernel,
        grid=(DEPTH, 3 * D_MODEL // tn),
        in_specs=[
            pl.BlockSpec((8, D_MODEL), lambda l, j: (0, 0)),
            pl.BlockSpec((1, D_MODEL, tn), lambda l, j: (l, 0, j)),
            pl.BlockSpec((1, 1, tn), lambda l, j: (l, 0, j)),
        ],
        out_specs=pl.BlockSpec((1, 8, tn), lambda l, j: (l, 0, j)),
        out_shape=jax.ShapeDtypeStruct((DEPTH, 8, 3 * D_MODEL), F32),
        compiler_params=_cparams(("arbitrary", "arbitrary")),
        name="adaln_mod",
    )(cvecs, ada_w, ada_b.reshape(DEPTH, 1, 3 * D_MODEL))


def _head_rms(z, wn):
    w = z.shape[1]
    r = lax.broadcasted_iota(jnp.int32, (w, w), 0) // HEAD_DIM
    c = lax.broadcasted_iota(jnp.int32, (w, w), 1) // HEAD_DIM
    ones = (r == c).astype(BF16)
    ss = _dot((z * z).astype(BF16), ones)
    return (z * lax.rsqrt(ss * (1.0 / HEAD_DIM) + EPS)) * wn


def _rope(z, cos_t, sin_t):
    w = z.shape[1]
    reps = w // LANES
    if reps > 1:
        cos_t = jnp.concatenate([cos_t] * reps, axis=1)
        sin_t = jnp.concatenate([sin_t] * reps, axis=1)
    lane = lax.broadcasted_iota(jnp.int32, z.shape, 1)
    partner = jnp.where((lane & (HEAD_DIM // 2)) == 0,
                        pltpu.roll(z, w - HEAD_DIM // 2, 1), pltpu.roll(z, HEAD_DIM // 2, 1))
    return z * cos_t + partner * sin_t


def _in_proj_kernel(*refs, rope):
    if rope:
        (x_ref, mod_ref, nw_ref, w_ref, qn_ref, kn_ref, cos_ref, sin_ref,
         qa_ref, ka_ref, va_ref, qb_ref, kb_ref, vb_ref, qc_ref, kc_ref, vc_ref, sg_ref) = refs
        cos_t, sin_t = cos_ref[...], sin_ref[...]
        rot = lambda z: _rope(z, cos_t, sin_t)
    else:
        (x_ref, mod_ref, nw_ref, w_ref, qn_ref, kn_ref,
         qa_ref, ka_ref, va_ref, qb_ref, kb_ref, vb_ref, qc_ref, kc_ref, vc_ref, sg_ref) = refs
        rot = lambda z: z
    x = x_ref[0]
    y = (x * lax.rsqrt(jnp.mean(x * x, axis=-1, keepdims=True) + EPS)) * nw_ref[...]
    h = (y * (1.0 + mod_ref[0, 1:2, :]) + mod_ref[0, 0:1, :]).astype(BF16)

    def proj(off, width):
        return _dot(h, w_ref[:, off:off + width])

    q_scale = SCALE * LOG2E if rope else SCALE
    gate_w = MIX_WIDTH // 3
    groups = [
        (OFF_AQ, Q_W, qa_ref, lambda z: rot(_head_rms(z, qn_ref[...])) * q_scale),
        (OFF_AK, KV_W, ka_ref, lambda z: rot(_head_rms(z, kn_ref[...]))),
        (OFF_AV, KV_W, va_ref, lambda z: z),
        (OFF_BQ, Q_W, qb_ref, lambda z: z * q_scale),
        (OFF_BK, Q_W, kb_ref, lambda z: z),
        (OFF_BV, Q_W, vb_ref, lambda z: z),
        (OFF_CQ, Q_W, qc_ref, lambda z: rot(z) * q_scale),
        (OFF_CK, KV_W, kc_ref, lambda z: rot(z)),
        (OFF_CV, KV_W, vc_ref, lambda z: z),
    ] + [(OFF_G + t * gate_w, gate_w, (sg_ref, t * gate_w), _silu) for t in range(3)]
    z = proj(groups[0][0], groups[0][1])
    for gi, (off, width, dst, epilogue) in enumerate(groups):
        if gi + 1 < len(groups):
            z_next = proj(groups[gi + 1][0], groups[gi + 1][1])
        out = epilogue(z).astype(BF16)
        if isinstance(dst, tuple):
            dst[0][0, :, dst[1]:dst[1] + width] = out
        elif dst is kb_ref and rope:
            for c in range(N_HEADS // 2):
                kb_ref[0, c] = out[:, c * LANES:(c + 1) * LANES]
        else:
            dst[0] = out
        z = z_next


def _in_proj(x, mod, nw, w_bf, qn_t, kn_t, rope_tabs, tm):
    b, n, _ = x.shape
    rope = rope_tabs is not None
    mod_idx = (lambda bi, i: (bi, 0, 0)) if mod.shape[0] > 1 else (lambda bi, i: (0, 0, 0))
    const2 = lambda bi, i: (0, 0)
    in_specs = [
        pl.BlockSpec((1, tm, D_MODEL), lambda bi, i: (bi, i, 0)),
        pl.BlockSpec((1, 3, D_MODEL), mod_idx),
        pl.BlockSpec((1, D_MODEL), const2),
        pl.BlockSpec((D_MODEL, IN_WIDTH), const2, pipeline_mode=pl.Buffered(1)),
        pl.BlockSpec((1, Q_W), const2),
        pl.BlockSpec((1, KV_W), const2),
    ]
    args = [x, mod, nw, w_bf, qn_t, kn_t]
    if rope:
        in_specs += [pl.BlockSpec((tm, LANES), lambda bi, i: (i, 0))] * 2
        args += list(rope_tabs)
    widths = (Q_W, KV_W, KV_W, Q_W, Q_W, Q_W, Q_W, KV_W, KV_W, MIX_WIDTH)
    out_specs = [pl.BlockSpec((1, tm, w), lambda bi, i: (bi, i, 0)) for w in widths]
    out_shape = [jax.ShapeDtypeStruct((b, n, w), BF16) for w in widths]
    if rope:
        out_specs[4] = pl.BlockSpec((1, N_HEADS // 2, tm, LANES), lambda bi, i: (bi, 0, i, 0))
        out_shape[4] = jax.ShapeDtypeStruct((b, N_HEADS // 2, n, LANES), BF16)
    return pl.pallas_call(
        functools.partial(_in_proj_kernel, rope=rope),
        grid=(b, n // tm),
        in_specs=in_specs,
        out_specs=out_specs,
        out_shape=out_shape,
        compiler_params=_cparams(("parallel", "parallel")),
        name="in_proj_rope" if rope else "in_proj_ctx",
    )(*args)


def _stack_group_q(q, g):
    t = q.shape[0]
    lane = lax.broadcasted_iota(jnp.int32, (t, LANES), 1)
    keep = (lane // HEAD_DIM) == g
    parts = []
    for j in range(GROUP):
        h = GROUP * g + j
        chunk = q[:, (h // 2) * LANES:(h // 2 + 1) * LANES].astype(F32)
        if h % 2 != g:
            chunk = pltpu.roll(chunk, HEAD_DIM, 1)
        parts.append(jnp.where(keep, chunk, 0.0).astype(BF16))
    return jnp.concatenate(parts, axis=0)


def _unstack_group_out(o, g, t):
    lane = lax.broadcasted_iota(jnp.int32, (t, LANES), 1)
    left = lane < HEAD_DIM
    chunks = []
    for cl in range(2):
        even = o[(2 * cl) * t:(2 * cl + 1) * t]
        odd = o[(2 * cl + 1) * t:(2 * cl + 2) * t]
        if g == 0:
            odd = pltpu.roll(odd, HEAD_DIM, 1)
        else:
            even = pltpu.roll(even, HEAD_DIM, 1)
        chunks.append(jnp.where(left, even, odd))
    return chunks


def _softmax_pv(s_list, v_list, extra_logit=None):
    m = s_list[0].max(axis=1, keepdims=True)
    for s in s_list[1:]:
        m = jnp.maximum(m, s.max(axis=1, keepdims=True))
    if extra_logit is not None:
        m = jnp.maximum(m, extra_logit)
    l = None
    o = None
    for s, v in zip(s_list, v_list):
        p = jnp.exp(s - m)
        ps = p.sum(axis=1, keepdims=True)
        po = _dot(p.astype(BF16), v)
        l = ps if l is None else l + ps
        o = po if o is None else o + po
    if extra_logit is not None:
        l = l + jnp.exp(extra_logit - m)
    return o / l


def _sink_column(sink_ref, g, t):
    row = lax.broadcasted_iota(jnp.int32, (GROUP * t, 1), 0) // t
    col = jnp.full((GROUP * t, 1), sink_ref[GROUP * g], F32)
    for j in range(1, GROUP):
        col = jnp.where(row == j, sink_ref[GROUP * g + j], col)
    return col


ONES_ROWS = 16


def _attn_a_kernel(q_ref, k_ref, v_ref, kc_ref, vc_ref, o_ref, vt_ref, vct_ref, qs_ref, m_ref, acc_ref,
                   s0_ref, s1_ref, s2_ref, mx0_ref, mx1_ref, mx2_ref, *, tq, tk, col_blk):
    n_chunks = SEQ // tk

    @pl.when(pl.program_id(1) == 0)
    def _():
        for j in range(SEQ // tk):
            vt_ref[j] = _transpose_bf16(v_ref[0, j * tk:(j + 1) * tk, :])
        vct_ref[...] = _transpose_bf16(vc_ref[0])

    col_blocks = [slice(c0, c0 + col_blk) for c0 in range(0, GROUP * tq, col_blk)]
    bufs = ((s0_ref, mx0_ref), (s1_ref, mx1_ref), (s2_ref, mx2_ref))

    def score_block(k, buf, cols):
        s_ref, mx_ref = bufs[buf]
        s = _dot(k, qs_ref[:, cols])
        s_ref[cols.start // col_blk, 0:k.shape[0], :] = s
        mx_ref[:, cols] = s.max(axis=0, keepdims=True)

    def absorb_block(n, buf, v_ext, cols):
        s_ref, mx_ref = bufs[buf]
        m_prev = m_ref[:, cols]
        m_new = jnp.maximum(m_prev, mx_ref[:, cols])
        alpha = jnp.exp2(m_prev - m_new)
        p = jnp.exp2(s_ref[cols.start // col_blk, 0:n, :] - m_new).astype(BF16)
        acc_ref[:, cols] = alpha * acc_ref[:, cols] + _dot(v_ext, p)
        m_ref[:, cols] = m_new

    def latent_keys(j):
        return k_ref[0, pl.ds(pl.multiple_of(j * tk, tk), tk), :]

    def trip(n_cur, cur, vt_g, k_next, nxt):
        v_ext = _with_ones_rows(vt_g)
        for cols in col_blocks:
            if k_next is not None:
                score_block(k_next, nxt, cols)
            absorb_block(n_cur, cur, v_ext, cols)

    assert n_chunks % 3 == 1
    qt = _transpose_bf16(q_ref[0])
    for g in range(N_KV):
        rows = slice(g * HEAD_DIM, (g + 1) * HEAD_DIM)
        qs_ref[...] = jnp.zeros(qs_ref.shape, BF16)
        for j in range(GROUP):
            h = GROUP * g + j
            qs_ref[rows, j * tq:(j + 1) * tq] = qt[h * HEAD_DIM:(h + 1) * HEAD_DIM, :]
        m_ref[...] = jnp.full(m_ref.shape, NEG, F32)
        acc_ref[...] = jnp.zeros(acc_ref.shape, F32)

        kc = kc_ref[0]
        for cols in col_blocks:
            score_block(kc, 2, cols)
        trip(CTX_LEN, 2, vct_ref[rows, :], latent_keys(0), 0)

        def body(i, carry, rows=rows):
            j = 3 * i
            trip(tk, 0, vt_ref[j, rows, :], latent_keys(j + 1), 1)
            trip(tk, 1, vt_ref[j + 1, rows, :], latent_keys(j + 2), 2)
            trip(tk, 2, vt_ref[j + 2, rows, :], latent_keys(j + 3), 0)
            return carry

        lax.fori_loop(0, n_chunks // 3, body, 0)
        trip(tk, 0, vt_ref[n_chunks - 1, rows, :], None, None)
        o = acc_ref[0:HEAD_DIM, :] / acc_ref[HEAD_DIM:HEAD_DIM + 1, :]
        for cl in range(GROUP // 2):
            pair = jnp.concatenate([o[:, (2 * cl) * tq:(2 * cl + 1) * tq],
                                    o[:, (2 * cl + 1) * tq:(2 * cl + 2) * tq]], axis=0)
            cc = (GROUP // 2) * g + cl
            o_ref[0, :, cc * LANES:(cc + 1) * LANES] = _transpose_bf16(pair)


def _attn_a(q, k, v, kc, vc, tq=512, tk=512, col_blk=256):
    b = q.shape[0]
    res = lambda n: pl.BlockSpec((1, n, KV_W), lambda bi, i: (bi, 0, 0))
    return pl.pallas_call(
        functools.partial(_attn_a_kernel, tq=tq, tk=tk, col_blk=col_blk),
        grid=(b, SEQ // tq),
        in_specs=[pl.BlockSpec((1, tq, Q_W), lambda bi, i: (bi, i, 0)),
                  res(SEQ), res(SEQ), res(CTX_LEN), res(CTX_LEN)],
        out_specs=pl.BlockSpec((1, tq, Q_W), lambda bi, i: (bi, i, 0)),
        out_shape=jax.ShapeDtypeStruct((b, SEQ, Q_W), BF16),
        scratch_shapes=[pltpu.VMEM((SEQ // tk, KV_W, tk), BF16),
                        pltpu.VMEM((KV_W, CTX_LEN), BF16),
                        pltpu.VMEM((LANES, GROUP * tq), BF16),
                        pltpu.VMEM((1, GROUP * tq), F32),
                        pltpu.VMEM((HEAD_DIM + ONES_ROWS, GROUP * tq), F32),
                        pltpu.VMEM((GROUP * tq // col_blk, tk, col_blk), F32),
                        pltpu.VMEM((GROUP * tq // col_blk, tk, col_blk), F32),
                        pltpu.VMEM((GROUP * tq // col_blk, tk, col_blk), F32),
                        pltpu.VMEM((1, GROUP * tq), F32),
                        pltpu.VMEM((1, GROUP * tq), F32),
                        pltpu.VMEM((1, GROUP * tq), F32)],
        compiler_params=_cparams(("parallel", "arbitrary")),
        name="attn_global",
    )(q, k, v, kc, vc)


def _fill_feature_major(v_ref, vt_ref, blk):
    def fill(j, carry):
        off = pl.multiple_of(j * blk, blk)
        vt_ref[j] = _transpose_bf16(v_ref[0, pl.ds(off, blk), :])
        return carry

    lax.fori_loop(0, v_ref.shape[1] // blk, fill, 0)


def _with_ones_rows(vt):
    return jnp.concatenate([vt, jnp.ones((ONES_ROWS, vt.shape[1]), BF16)], axis=0)


def _attn_c_kernel(sink_ref, q_ref, k_ref, v_ref, kc_ref, vc_ref, o_ref, vt_ref, vct_ref,
                   qs0_ref, qs1_ref, s0_ref, s1_ref, mx0_ref, mx1_ref, band_ref, *, tq):
    win = tq + 2 * WINDOW
    n_keys = win + CTX_LEN
    i = pl.program_id(1)

    @pl.when(i == 0)
    def _():
        _fill_feature_major(v_ref, vt_ref, LANES)
        vct_ref[...] = _transpose_bf16(vc_ref[0])

    ws = pl.multiple_of(jnp.clip(i * tq - WINDOW, 0, SEQ - win), LANES)
    blk0 = ws // LANES
    kw = k_ref[0, pl.ds(ws, win), :]
    kc = kc_ref[0]
    kpos = ws + lax.broadcasted_iota(jnp.int32, (win, tq), 0)
    qpos = i * tq + lax.broadcasted_iota(jnp.int32, (win, tq), 1)
    band_ref[...] = jnp.where(jnp.abs(qpos - kpos) <= WINDOW, 0.0, NEG)
    qt = _transpose_bf16(q_ref[0])
    bufs = ((qs0_ref, s0_ref, mx0_ref), (qs1_ref, s1_ref, mx1_ref))
    col_blocks = [slice(j * tq, (j + 1) * tq) for j in range(GROUP)]

    def stack_queries(g):
        qs_ref = bufs[g][0]
        qs_ref[...] = jnp.zeros(qs_ref.shape, BF16)
        for j in range(GROUP):
            h = GROUP * g + j
            qs_ref[g * HEAD_DIM:(g + 1) * HEAD_DIM, j * tq:(j + 1) * tq] = (
                qt[h * HEAD_DIM:(h + 1) * HEAD_DIM, :])

    def score_block(g, j):
        qs_ref, s_ref, mx_ref = bufs[g]
        cols = col_blocks[j]
        s_win = _dot(kw, qs_ref[:, cols]) + band_ref[...]
        s_ctx = _dot(kc, qs_ref[:, cols])
        s_ref[j, 0:win, :] = s_win
        s_ref[j, win:n_keys, :] = s_ctx
        sink2 = sink_ref[GROUP * g + j] * LOG2E
        mx_ref[:, cols] = jnp.maximum(
            jnp.maximum(s_win.max(axis=0, keepdims=True), s_ctx.max(axis=0, keepdims=True)), sink2)

    def values(g):
        rows = slice(g * HEAD_DIM, (g + 1) * HEAD_DIM)
        blocks = [vt_ref[blk0 + t, rows, :] for t in range(win // LANES)] + [vct_ref[rows, :]]
        return _with_ones_rows(jnp.concatenate(blocks, axis=1))

    def out_block(g, j, v_ext):
        _, s_ref, mx_ref = bufs[g]
        cols = col_blocks[j]
        m = mx_ref[:, cols]
        p = jnp.exp2(s_ref[j] - m).astype(BF16)
        acc = _dot(v_ext, p)
        l = acc[HEAD_DIM:HEAD_DIM + 1, :] + jnp.exp2(sink_ref[GROUP * g + j] * LOG2E - m)
        return acc[0:HEAD_DIM, :] / l

    def store_pair(g, cl, o_even, o_odd):
        c = (GROUP // 2) * g + cl
        o_ref[0, :, c * LANES:(c + 1) * LANES] = _transpose_bf16(
            jnp.concatenate([o_even, o_odd], axis=0))

    stack_queries(0)
    stack_queries(1)
    for j in range(GROUP):
        score_block(0, j)
    v_ext = values(0)
    outs = []
    for j in range(GROUP):
        score_block(1, j)
        outs.append(out_block(0, j, v_ext))
        if j % 2 == 1:
            store_pair(0, j // 2, outs[j - 1], outs[j])
    v_ext = values(1)
    outs = []
    for j in range(GROUP):
        outs.append(out_block(1, j, v_ext))
        if j % 2 == 1:
            store_pair(1, j // 2, outs[j - 1], outs[j])


def _attn_c(sink, q, k, v, kc, vc, tq=256):
    b = q.shape[0]
    n_keys = tq + 2 * WINDOW + CTX_LEN
    res = lambda n: pl.BlockSpec((1, n, KV_W), lambda bi, i: (bi, 0, 0))
    return pl.pallas_call(
        functools.partial(_attn_c_kernel, tq=tq),
        grid=(b, SEQ // tq),
        in_specs=[pl.BlockSpec(memory_space=pltpu.SMEM),
                  pl.BlockSpec((1, tq, Q_W), lambda bi, i: (bi, i, 0)),
                  res(SEQ), res(SEQ), res(CTX_LEN), res(CTX_LEN)],
        out_specs=pl.BlockSpec((1, tq, Q_W), lambda bi, i: (bi, i, 0)),
        out_shape=jax.ShapeDtypeStruct((b, SEQ, Q_W), BF16),
        scratch_shapes=[pltpu.VMEM((SEQ // LANES, KV_W, LANES), BF16),
                        pltpu.VMEM((KV_W, CTX_LEN), BF16),
                        pltpu.VMEM((LANES, GROUP * tq), BF16),
                        pltpu.VMEM((LANES, GROUP * tq), BF16),
                        pltpu.VMEM((GROUP, n_keys, tq), F32),
                        pltpu.VMEM((GROUP, n_keys, tq), F32),
                        pltpu.VMEM((1, GROUP * tq), F32),
                        pltpu.VMEM((1, GROUP * tq), F32),
                        pltpu.VMEM((tq + 2 * WINDOW, tq), F32)],
        compiler_params=_cparams(("parallel", "arbitrary")),
        name="attn_window",
    )(sink, q, k, v, kc, vc)


def _nb_bias_kernel(rpb_ref, o_ref, tt_ref):
    ck = lax.broadcasted_iota(jnp.int32, (GRID_W, LANES), 0)
    lane = lax.broadcasted_iota(jnp.int32, (GRID_W, LANES), 1)
    cq = lane & (GRID_W - 1)
    d = ck - cq + (NB_COLS - 1)
    cs = jnp.clip(cq - NB_COLS // 2, 0, GRID_W - NB_COLS)
    col_ok = (ck >= cs) & (ck < cs + NB_COLS)
    n_dc = 2 * NB_COLS - 1
    neg_tile = jnp.full((GRID_W, LANES), NEG, F32)
    even_row = lane < GRID_W
    for h in range(2):
        base = ((pl.program_id(0) * (N_HEADS // 2) + pl.program_id(1)) * 2 + h) * N_REL
        for a in range(2 * NB_ROWS - 1):
            t = neg_tile
            for dd in range(n_dc):
                t = jnp.where(d == dd, rpb_ref[base + a * n_dc + dd] * LOG2E, t)
            tt_ref[a] = jnp.where(col_ok, t, NEG)
        for case, pat in enumerate(NB_PATTERNS):
            for kr in range(NB_WIN_ROWS):
                for qv in range(NB_Q_ROWS // 2):
                    a0, a1 = pat[2 * qv][kr], pat[2 * qv + 1][kr]
                    lo = neg_tile if a0 is None else tt_ref[a0]
                    hi = neg_tile if a1 is None else tt_ref[a1]
                    o_ref[0, case, 0, h, kr * GRID_W:(kr + 1) * GRID_W, qv * LANES:(qv + 1) * LANES] = (
                        jnp.where(even_row, lo, hi))


def _nb_bias(rpb):
    n_case = len(NB_PATTERNS)
    return pl.pallas_call(
        _nb_bias_kernel,
        grid=(DEPTH, N_HEADS // 2),
        in_specs=[pl.BlockSpec(memory_space=pltpu.SMEM)],
        out_specs=pl.BlockSpec((1, n_case, 1, 2, NB_WIN, NB_TQ), lambda l, c: (l, 0, c, 0, 0, 0)),
        out_shape=jax.ShapeDtypeStruct((DEPTH, n_case, N_HEADS // 2, 2, NB_WIN, NB_TQ), F32),
        scratch_shapes=[pltpu.VMEM((2 * NB_ROWS - 1, GRID_W, LANES), F32)],
        compiler_params=_cparams(("arbitrary", "arbitrary")),
        name="nb_bias",
    )(rpb.reshape(-1))


def _pair_attention(qp, s_extra_fn, k_list, v_list):
    t = qp.shape[0]
    lane = lax.broadcasted_iota(jnp.int32, (t, LANES), 1)
    qf = qp.astype(F32)
    outs = []
    for h in range(2):
        qpad = jnp.where((lane // HEAD_DIM) == h, qf, 0.0).astype(BF16)
        s_list = [_dot_nt(qpad, k) for k in k_list]
        s_list = s_extra_fn(h, s_list)
        outs.append(_softmax_pv(s_list, v_list))
    return jnp.where(lane < HEAD_DIM, outs[0], outs[1])


def _attn_b_kernel(q_ref, k_ref, v_ref, kc_ref, vc_ref, bias_ref, o_ref, vt_ref, vct_ref,
                   s0_ref, s1_ref, mx0_ref, mx1_ref):
    i = pl.program_id(1)

    @pl.when(i == 0)
    def _():
        _fill_feature_major(v_ref, vt_ref, NB_KEY_BLK)
        vct_ref[...] = _transpose_bf16(vc_ref[0])

    w0 = jnp.clip(NB_Q_ROWS * i - NB_ROWS // 2, 0, GRID_H - NB_WIN_ROWS)
    off = pl.multiple_of(w0 * GRID_W, NB_KEY_BLK)
    blk0 = w0 // 2
    bufs = ((s0_ref, mx0_ref), (s1_ref, mx1_ref))
    zeros = jnp.zeros((NB_TQ, LANES), BF16)
    head0_lanes = lax.broadcasted_iota(jnp.int32, (NB_TQ, LANES), 1) < HEAD_DIM

    def score_stage(h):
        s_ref, mx_ref = bufs[h % 2]
        c, t = divmod(h, 2)
        cols = slice(c * LANES, (c + 1) * LANES)
        qp = q_ref[0, :, cols]
        qpad = jnp.where(head0_lanes, qp, zeros) if t == 0 else jnp.where(head0_lanes, zeros, qp)
        s_win = _dot_nt(k_ref[0, c, pl.ds(off, NB_WIN), :], qpad) + bias_ref[0, 0, c, t]
        s_ctx = _dot_nt(kc_ref[0, :, cols], qpad)
        s_ref[0:NB_WIN, :] = s_win
        s_ref[NB_WIN:NB_WIN + CTX_LEN, :] = s_ctx
        mx_ref[...] = jnp.maximum(s_win.max(axis=0, keepdims=True), s_ctx.max(axis=0, keepdims=True))

    def pv_stage(h):
        s_ref, mx_ref = bufs[h % 2]
        rows = slice(h * HEAD_DIM, (h + 1) * HEAD_DIM)
        p = jnp.exp2(s_ref[...] - mx_ref[...]).astype(BF16)
        blocks = [vt_ref[blk0 + t, rows, :] for t in range(NB_WIN // NB_KEY_BLK)] + [vct_ref[rows, :]]
        acc = _dot(_with_ones_rows(jnp.concatenate(blocks, axis=1)), p)
        return acc[0:HEAD_DIM, :] / acc[HEAD_DIM:HEAD_DIM + 1, :]

    score_stage(0)
    outs = []
    for h in range(N_HEADS):
        if h + 1 < N_HEADS:
            score_stage(h + 1)
        outs.append(pv_stage(h))
        if h % 2 == 1:
            o_ref[0, :, (h // 2) * LANES:(h // 2 + 1) * LANES] = _transpose_bf16(
                jnp.concatenate([outs[h - 1], outs[h]], axis=0))


def _attn_b(q, k, v, kc, vc, bias, layer):
    b = q.shape[0]
    n_tiles = SEQ // NB_TQ
    res = lambda n: pl.BlockSpec((1, n, Q_W), lambda bi, i: (bi, 0, 0), pipeline_mode=pl.Buffered(1))

    n_case = len(NB_PATTERNS)
    interior = max(set(NB_TILE_CASE), key=NB_TILE_CASE.count)
    n_lo, n_hi = interior, n_case - 1 - interior
    case_of = lambda t: min(t, n_lo) + max(t - (n_tiles - 1 - n_hi), 0)
    assert [case_of(t) for t in range(n_tiles)] == NB_TILE_CASE

    def bias_idx(bi, i):
        return (layer, jnp.minimum(i, n_lo) + jnp.maximum(i - (n_tiles - 1 - n_hi), 0), 0, 0, 0, 0)

    return pl.pallas_call(
        _attn_b_kernel,
        grid=(b, n_tiles),
        in_specs=[pl.BlockSpec((1, NB_TQ, Q_W), lambda bi, i: (bi, i, 0)),
                  pl.BlockSpec((1, N_HEADS // 2, SEQ, LANES), lambda bi, i: (bi, 0, 0, 0),
                               pipeline_mode=pl.Buffered(1)),
                  res(SEQ), res(CTX_LEN), res(CTX_LEN),
                  pl.BlockSpec((1, 1, N_HEADS // 2, 2, NB_WIN, NB_TQ), bias_idx)],
        out_specs=pl.BlockSpec((1, NB_TQ, Q_W), lambda bi, i: (bi, i, 0)),
        out_shape=jax.ShapeDtypeStruct((b, SEQ, Q_W), BF16),
        scratch_shapes=[pltpu.VMEM((SEQ // NB_KEY_BLK, Q_W, NB_KEY_BLK), BF16),
                        pltpu.VMEM((Q_W, CTX_LEN), BF16),
                        pltpu.VMEM((NB_WIN + CTX_LEN, NB_TQ), F32),
                        pltpu.VMEM((NB_WIN + CTX_LEN, NB_TQ), F32),
                        pltpu.VMEM((1, NB_TQ), F32),
                        pltpu.VMEM((1, NB_TQ), F32)],
        compiler_params=_cparams(("parallel", "arbitrary")),
        name="attn_neighbourhood",
    )(q, k, v, kc, vc, bias)


def _attn_ctx_kernel(sink_ref, qa_ref, ka_ref, va_ref, qb_ref, kb_ref, vb_ref, qc_ref, kc_ref, vc_ref,
                     oa_ref, ob_ref, oc_ref):
    t = CTX_LEN
    for g in range(N_KV):
        for q_ref, k_ref, v_ref, o_ref, sink in ((qa_ref, ka_ref, va_ref, oa_ref, False),
                                                 (qc_ref, kc_ref, vc_ref, oc_ref, True)):
            qs = _stack_group_q(q_ref[0], g)
            extra = _sink_column(sink_ref, g, t) if sink else None
            o = _softmax_pv([_dot_nt(qs, k_ref[0])], [v_ref[0]], extra)
            chunks = _unstack_group_out(o, g, t)
            for cl in range(2):
                c = 2 * g + cl
                o_ref[0, :, c * LANES:(c + 1) * LANES] = chunks[cl].astype(BF16)
    for c in range(N_HEADS // 2):
        cols = slice(c * LANES, (c + 1) * LANES)
        o = _pair_attention(qb_ref[0, :, cols], lambda h, s_list: s_list,
                            [kb_ref[0, :, cols]], [vb_ref[0, :, cols]])
        ob_ref[0, :, cols] = o.astype(BF16)


def _attn_ctx(sink, qa, ka, va, qb, kb, vb, qc, kc, vc):
    b = qa.shape[0]
    spec = lambda w: pl.BlockSpec((1, CTX_LEN, w), lambda bi: (bi, 0, 0))
    return pl.pallas_call(
        _attn_ctx_kernel,
        grid=(b,),
        in_specs=[pl.BlockSpec(memory_space=pltpu.SMEM),
                  spec(Q_W), spec(KV_W), spec(KV_W), spec(Q_W), spec(Q_W), spec(Q_W),
                  spec(Q_W), spec(KV_W), spec(KV_W)],
        out_specs=[spec(Q_W)] * 3,
        out_shape=[jax.ShapeDtypeStruct((b, CTX_LEN, Q_W), BF16)] * 3,
        compiler_params=_cparams(("parallel",)),
        name="attn_ctx",
    )(sink, qa, ka, va, qb, kb, vb, qc, kc, vc)


def _out_proj_kernel(*refs, final):
    if final:
        ya_ref, yb_ref, yc_ref, sg_ref, x_ref, mod_ref, w_ref, fw_ref, o_ref = refs
    else:
        ya_ref, yb_ref, yc_ref, sg_ref, x_ref, mod_ref, w_ref, o_ref = refs
    acc = None
    for idx, y_ref in enumerate((ya_ref, yb_ref, yc_ref)):
        cols = slice(idx * Q_W, (idx + 1) * Q_W)
        u = (y_ref[0].astype(F32) * sg_ref[0, :, cols].astype(F32)).astype(BF16)
        part = _dot(u, w_ref[cols, :])
        acc = part if acc is None else acc + part
    xn = x_ref[0] + mod_ref[0, 2:3, :] * acc
    if final:
        xn = (xn * lax.rsqrt(jnp.mean(xn * xn, axis=-1, keepdims=True) + EPS)) * fw_ref[...]
    o_ref[0] = xn


def _out_proj(ya, yb, yc, sg, x, mod, w_bf, final_w, tm):
    b, n, _ = x.shape
    final = final_w is not None
    mod_idx = (lambda bi, i: (bi, 0, 0)) if mod.shape[0] > 1 else (lambda bi, i: (0, 0, 0))
    row = lambda w: pl.BlockSpec((1, tm, w), lambda bi, i: (bi, i, 0))
    in_specs = [row(Q_W), row(Q_W), row(Q_W), row(MIX_WIDTH), row(D_MODEL),
                pl.BlockSpec((1, 3, D_MODEL), mod_idx),
                pl.BlockSpec((MIX_WIDTH, D_MODEL), lambda bi, i: (0, 0))]
    args = [ya, yb, yc, sg, x, mod, w_bf]
    if final:
        in_specs.append(pl.BlockSpec((1, D_MODEL), lambda bi, i: (0, 0)))
        args.append(final_w)
    return pl.pallas_call(
        functools.partial(_out_proj_kernel, final=final),
        grid=(b, n // tm),
        in_specs=in_specs,
        out_specs=row(D_MODEL),
        out_shape=jax.ShapeDtypeStruct((b, n, D_MODEL), F32),
        compiler_params=_cparams(("parallel", "parallel")),
        name="out_proj_final" if final else "out_proj",
    )(*args)


def _rope_tables():
    t = jnp.arange(SEQ, dtype=jnp.int32)
    rows = (t // GRID_W).astype(F32)
    cols = (t % GRID_W).astype(F32)
    n_freq = HEAD_DIM // 4
    freq = ROPE_THETA ** (-jnp.arange(n_freq, dtype=F32) / n_freq)
    ang = jnp.concatenate([rows[:, None] * freq, cols[:, None] * freq], axis=-1)
    cos, sin = jnp.cos(ang), jnp.sin(ang)
    cos_t = jnp.tile(jnp.concatenate([cos, cos], axis=-1), (1, LANES // HEAD_DIM))
    sin_t = jnp.tile(jnp.concatenate([-sin, sin], axis=-1), (1, LANES // HEAD_DIM))
    return cos_t, sin_t


def kernel(x, c, ctx, c_ctx, norm_w, ada_w, ada_b, w_in, w_out, q_norm_a, k_norm_a, rpb_b, sink_c,
           final_norm_w):
    bsz = x.shape[0]
    cvecs = jnp.concatenate([c, c_ctx[None], jnp.zeros((8 - bsz - 1, D_MODEL), F32)], axis=0)
    mod = _modulation(cvecs, ada_w, ada_b).reshape(DEPTH, 8, 3, D_MODEL)
    bias = _nb_bias(rpb_b)
    rope_tabs = _rope_tables()
    w_in_bf = w_in.astype(BF16)
    w_out_bf = w_out.astype(BF16)
    cx = ctx
    for l in range(DEPTH):
        need_ctx = l < DEPTH - 1
        mod_x, mod_c = mod[l, :bsz], mod[l, bsz:bsz + 1]
        nw = norm_w[l][None]
        qn_t = jnp.tile(q_norm_a[l], N_HEADS)[None]
        kn_t = jnp.tile(k_norm_a[l], N_KV)[None]
        qa, ka, va, qb, kb, vb, qc, kc, vc, sg = _in_proj(
            x, mod_x, nw, w_in_bf[l], qn_t, kn_t, rope_tabs, tm=512)
        qa_c, ka_c, va_c, qb_c, kb_c, vb_c, qc_c, kc_c, vc_c, sg_c = _in_proj(
            cx, mod_c, nw, w_in_bf[l], qn_t, kn_t, None, tm=CTX_LEN)
        ya = _attn_a(qa, ka, va, ka_c, va_c)
        yb = _attn_b(qb, kb, vb, kb_c, vb_c, bias, l)
        yc = _attn_c(sink_c[l], qc, kc, vc, kc_c, vc_c)
        if need_ctx:
            ya_c, yb_c, yc_c = _attn_ctx(sink_c[l], qa_c, ka_c, va_c, qb_c, kb_c, vb_c, qc_c, kc_c, vc_c)
            cx = _out_proj(ya_c, yb_c, yc_c, sg_c, cx, mod_c, w_out_bf[l], None, tm=CTX_LEN)
        x = _out_proj(ya, yb, yc, sg, x, mod_x, w_out_bf[l],
                      None if need_ctx else final_norm_w[None], tm=1024)
    return x
```

```python
import functools

import jax
import jax.numpy as jnp
from jax import lax
from jax.experimental import pallas as pl
from jax.experimental.pallas import tpu as pltpu

D_MODEL = 1024
SEQ = 8192
DEPTH = 2
GRID_W = 64
GRID_H = SEQ // GRID_W
CTX_LEN = 256
HEAD_DIM = 64
N_HEADS = 8
N_KV = 2
GROUP = N_HEADS // N_KV
Q_W = N_HEADS * HEAD_DIM
KV_W = N_KV * HEAD_DIM
MIX_WIDTH = 3 * Q_W
IN_WIDTH = 3 * Q_W + 2 * KV_W + 2 * Q_W + 2 * KV_W + MIX_WIDTH
NB_ROWS = 8
NB_COLS = 16
WINDOW = 128
ROPE_THETA = 10000.0
EPS = 1e-6
SCALE = HEAD_DIM ** -0.5
LOG2E = 1.4426950408889634
NEG = -1e30

LANES = 128
MXU_N = 256
VMEM_LIMIT = 56 * 1024 * 1024

ROW_TILE = 1024
GLOBAL_TQ = 512
GLOBAL_TK = 512
WINDOW_TQ = 256

OFF_AQ, OFF_AK, OFF_AV = 0, 512, 640
OFF_BQ, OFF_BK, OFF_BV = 768, 1280, 1792
OFF_CQ, OFF_CK, OFF_CV = 2304, 2816, 2944
OFF_G = 3072

NB_Q_ROWS = 4
NB_TQ = NB_Q_ROWS * GRID_W
NB_WIN_ROWS = NB_Q_ROWS + NB_ROWS
NB_WIN = NB_WIN_ROWS * GRID_W
NB_KEY_BLK = 2 * GRID_W
N_REL = (2 * NB_ROWS - 1) * (2 * NB_COLS - 1)


def _nb_window_row(r0):
    return min(max(r0 - NB_ROWS // 2, 0), GRID_H - NB_WIN_ROWS)


def _nb_first_key_row(r):
    return min(max(r - NB_ROWS // 2, 0), GRID_H - NB_ROWS)


def _nb_bias_pattern(r0):
    w0 = _nb_window_row(r0)
    pat = []
    for qr in range(NB_Q_ROWS):
        rs = _nb_first_key_row(r0 + qr)
        pat.append(tuple((w0 + kr) - (r0 + qr) + NB_ROWS - 1 if rs <= w0 + kr < rs + NB_ROWS else None
                         for kr in range(NB_WIN_ROWS)))
    return tuple(pat)


def _nb_cases():
    patterns, tile_case = [], []
    for r0 in range(0, GRID_H, NB_Q_ROWS):
        pat = _nb_bias_pattern(r0)
        if pat not in patterns:
            patterns.append(pat)
        tile_case.append(patterns.index(pat))
    return patterns, tile_case


NB_PATTERNS, NB_TILE_CASE = _nb_cases()

F32 = jnp.float32
BF16 = jnp.bfloat16


def _cparams(sem):
    return pltpu.CompilerParams(dimension_semantics=sem, vmem_limit_bytes=VMEM_LIMIT)


def _dot(a, b):
    return jnp.dot(a, b, preferred_element_type=F32)


def _dot_nt(a, b):
    return lax.dot_general(a, b, (((1,), (1,)), ((), ())), preferred_element_type=F32)


def _transpose_bf16(z):
    return z.astype(F32).T.astype(BF16)


def _silu(x):
    return x * (1.0 / (1.0 + jnp.exp(-x)))


def _mod_kernel(c_ref, w_ref, b_ref, o_ref):
    cs = _silu(c_ref[...])
    o_ref[0] = jnp.dot(cs, w_ref[0], preferred_element_type=F32,
                       precision=lax.Precision.HIGHEST) + b_ref[0]


def _modulation(cvecs, ada_w, ada_b):
    tn = 1024
    return pl.pallas_call(
        _mod_kernel,
        grid=(DEPTH, 3 * D_MODEL // tn),
        in_specs=[
            pl.BlockSpec((8, D_MODEL), lambda l, j: (0, 0)),
            pl.BlockSpec((1, D_MODEL, tn), lambda l, j: (l, 0, j)),
            pl.BlockSpec((1, 1, tn), lambda l, j: (l, 0, j)),
        ],
        out_specs=pl.BlockSpec((1, 8, tn), lambda l, j: (l, 0, j)),
        out_shape=jax.ShapeDtypeStruct((DEPTH, 8, 3 * D_MODEL), F32),
        compiler_params=_cparams(("arbitrary", "arbitrary")),
        name="adaln_mod",
    )(cvecs, ada_w, ada_b.reshape(DEPTH, 1, 3 * D_MODEL))


def _head_rms(z, wn):
    w = z.shape[1]
    r = lax.broadcasted_iota(jnp.int32, (w, w), 0) // HEAD_DIM
    c = lax.broadcasted_iota(jnp.int32, (w, w), 1) // HEAD_DIM
    ones = (r == c).astype(BF16)
    ss = _dot((z * z).astype(BF16), ones)
    return (z * lax.rsqrt(ss * (1.0 / HEAD_DIM) + EPS)) * wn


def _rope(z, cos_t, sin_t):
    w = z.shape[1]
    reps = w // LANES
    if reps > 1:
        cos_t = jnp.concatenate([cos_t] * reps, axis=1)
        sin_t = jnp.concatenate([sin_t] * reps, axis=1)
    lane = lax.broadcasted_iota(jnp.int32, z.shape, 1)
    partner = jnp.where((lane & (HEAD_DIM // 2)) == 0,
                        pltpu.roll(z, w - HEAD_DIM // 2, 1), pltpu.roll(z, HEAD_DIM // 2, 1))
    return z * cos_t + partner * sin_t


def _in_proj_kernel(*refs, rope):
    if rope:
        (x_ref, mod_ref, nw_ref, w_ref, qn_ref, kn_ref, cos_ref, sin_ref,
         qa_ref, ka_ref, va_ref, qb_ref, kb_ref, vb_ref, qc_ref, kc_ref, vc_ref, sg_ref) = refs
        cos_t, sin_t = cos_ref[...], sin_ref[...]
        rot = lambda z: _rope(z, cos_t, sin_t)
    else:
        (x_ref, mod_ref, nw_ref, w_ref, qn_ref, kn_ref,
         qa_ref, ka_ref, va_ref, qb_ref, kb_ref, vb_ref, qc_ref, kc_ref, vc_ref, sg_ref) = refs
        rot = lambda z: z
    x = x_ref[0]
    y = (x * lax.rsqrt(jnp.mean(x * x, axis=-1, keepdims=True) + EPS)) * nw_ref[...]
    h = (y * (1.0 + mod_ref[0, 1:2, :]) + mod_ref[0, 0:1, :]).astype(BF16)

    def proj(off, width):
        return _dot(h, w_ref[:, off:off + width])

    q_scale = SCALE * LOG2E if rope else SCALE
    gate_w = MIX_WIDTH // 3
    groups = [
        (OFF_AQ, Q_W, qa_ref, lambda z: rot(_head_rms(z, qn_ref[...])) * q_scale),
        (OFF_AK, KV_W, ka_ref, lambda z: rot(_head_rms(z, kn_ref[...]))),
        (OFF_AV, KV_W, va_ref, lambda z: z),
        (OFF_BQ, Q_W, qb_ref, lambda z: z * q_scale),
        (OFF_BK, Q_W, kb_ref, lambda z: z),
        (OFF_BV, Q_W, vb_ref, lambda z: z),
        (OFF_CQ, Q_W, qc_ref, lambda z: rot(z) * q_scale),
        (OFF_CK, KV_W, kc_ref, lambda z: rot(z)),
        (OFF_CV, KV_W, vc_ref, lambda z: z),
    ] + [(OFF_G + t * gate_w, gate_w, (sg_ref, t * gate_w), _silu) for t in range(3)]
    z = proj(groups[0][0], groups[0][1])
    for gi, (off, width, dst, epilogue) in enumerate(groups):
        if gi + 1 < len(groups):
            z_next = proj(groups[gi + 1][0], groups[gi + 1][1])
        out = epilogue(z).astype(BF16)
        if isinstance(dst, tuple):
            dst[0][0, :, dst[1]:dst[1] + width] = out
        elif dst is kb_ref and rope:
            for c in range(N_HEADS // 2):
                kb_ref[0, c] = out[:, c * LANES:(c + 1) * LANES]
        else:
            dst[0] = out
        z = z_next


def _in_proj(x, mod, nw, w_bf, qn_t, kn_t, rope_tabs, tm):
    b, n, _ = x.shape
    rope = rope_tabs is not None
    mod_idx = (lambda bi, i: (bi, 0, 0)) if mod.shape[0] > 1 else (lambda bi, i: (0, 0, 0))
    const2 = lambda bi, i: (0, 0)
    in_specs = [
        pl.BlockSpec((1, tm, D_MODEL), lambda bi, i: (bi, i, 0)),
        pl.BlockSpec((1, 3, D_MODEL), mod_idx),
        pl.BlockSpec((1, D_MODEL), const2),
        pl.BlockSpec((D_MODEL, IN_WIDTH), const2, pipeline_mode=pl.Buffered(1)),
        pl.BlockSpec((1, Q_W), const2),
        pl.BlockSpec((1, KV_W), const2),
    ]
    args = [x, mod, nw, w_bf, qn_t, kn_t]
    if rope:
        in_specs += [pl.BlockSpec((tm, LANES), lambda bi, i: (i, 0))] * 2
        args += list(rope_tabs)
    widths = (Q_W, KV_W, KV_W, Q_W, Q_W, Q_W, Q_W, KV_W, KV_W, MIX_WIDTH)
    out_specs = [pl.BlockSpec((1, tm, w), lambda bi, i: (bi, i, 0)) for w in widths]
    out_shape = [jax.ShapeDtypeStruct((b, n, w), BF16) for w in widths]
    if rope:
        out_specs[4] = pl.BlockSpec((1, N_HEADS // 2, tm, LANES), lambda bi, i: (bi, 0, i, 0))
        out_shape[4] = jax.ShapeDtypeStruct((b, N_HEADS // 2, n, LANES), BF16)
    return pl.pallas_call(
        functools.partial(_in_proj_kernel, rope=rope),
        grid=(b, n // tm),
        in_specs=in_specs,
        out_specs=out_specs,
        out_shape=out_shape,
        compiler_params=_cparams(("parallel", "parallel")),
        name="in_proj_rope" if rope else "in_proj_ctx",
    )(*args)


def _stack_group_q(q, g):
    t = q.shape[0]
    lane = lax.broadcasted_iota(jnp.int32, (t, LANES), 1)
    keep = (lane // HEAD_DIM) == g
    parts = []
    for j in range(GROUP):
        h = GROUP * g + j
        chunk = q[:, (h // 2) * LANES:(h // 2 + 1) * LANES].astype(F32)
        if h % 2 != g:
            chunk = pltpu.roll(chunk, HEAD_DIM, 1)
        parts.append(jnp.where(keep, chunk, 0.0).astype(BF16))
    return jnp.concatenate(parts, axis=0)


def _unstack_group_out(o, g, t):
    lane = lax.broadcasted_iota(jnp.int32, (t, LANES), 1)
    left = lane < HEAD_DIM
    chunks = []
    for cl in range(2):
        even = o[(2 * cl) * t:(2 * cl + 1) * t]
        odd = o[(2 * cl + 1) * t:(2 * cl + 2) * t]
        if g == 0:
            odd = pltpu.roll(odd, HEAD_DIM, 1)
        else:
            even = pltpu.roll(even, HEAD_DIM, 1)
        chunks.append(jnp.where(left, even, odd))
    return chunks


def _softmax_pv(s_list, v_list, extra_logit=None):
    m = s_list[0].max(axis=1, keepdims=True)
    for s in s_list[1:]:
        m = jnp.maximum(m, s.max(axis=1, keepdims=True))
    if extra_logit is not None:
        m = jnp.maximum(m, extra_logit)
    l = None
    o = None
    for s, v in zip(s_list, v_list):
        p = jnp.exp(s - m)
        ps = p.sum(axis=1, keepdims=True)
        po = _dot(p.astype(BF16), v)
        l = ps if l is None else l + ps
        o = po if o is None else o + po
    if extra_logit is not None:
        l = l + jnp.exp(extra_logit - m)
    return o / l


def _sink_column(sink_ref, g, t):
    row = lax.broadcasted_iota(jnp.int32, (GROUP * t, 1), 0) // t
    col = jnp.full((GROUP * t, 1), sink_ref[GROUP * g], F32)
    for j in range(1, GROUP):
        col = jnp.where(row == j, sink_ref[GROUP * g + j], col)
    return col


ONES_ROWS = 16


def _attn_a_kernel(q_ref, k_ref, v_ref, kc_ref, vc_ref, o_ref, vt_ref, vct_ref, qs_ref, m_ref, acc_ref,
                   s0_ref, s1_ref, s2_ref, mx0_ref, mx1_ref, mx2_ref, *, tq, tk, col_blk):
    n_chunks = SEQ // tk

    @pl.when(pl.program_id(1) == 0)
    def _():
        for j in range(SEQ // tk):
            vt_ref[j] = _transpose_bf16(v_ref[0, j * tk:(j + 1) * tk, :])
        vct_ref[...] = _transpose_bf16(vc_ref[0])

    col_blocks = [slice(c0, c0 + col_blk) for c0 in range(0, GROUP * tq, col_blk)]
    bufs = ((s0_ref, mx0_ref), (s1_ref, mx1_ref), (s2_ref, mx2_ref))

    def score_block(k, buf, cols):
        s_ref, mx_ref = bufs[buf]
        s = _dot(k, qs_ref[:, cols])
        s_ref[cols.start // col_blk, 0:k.shape[0], :] = s
        mx_ref[:, cols] = s.max(axis=0, keepdims=True)

    def absorb_block(n, buf, v_ext, cols):
        s_ref, mx_ref = bufs[buf]
        m_prev = m_ref[:, cols]
        m_new = jnp.maximum(m_prev, mx_ref[:, cols])
        alpha = jnp.exp2(m_prev - m_new)
        p = jnp.exp2(s_ref[cols.start // col_blk, 0:n, :] - m_new).astype(BF16)
        acc_ref[:, cols] = alpha * acc_ref[:, cols] + _dot(v_ext, p)
        m_ref[:, cols] = m_new

    def latent_keys(j):
        return k_ref[0, pl.ds(pl.multiple_of(j * tk, tk), tk), :]

    def trip(n_cur, cur, vt_g, k_next, nxt):
        v_ext = _with_ones_rows(vt_g)
        for cols in col_blocks:
            if k_next is not None:
                score_block(k_next, nxt, cols)
            absorb_block(n_cur, cur, v_ext, cols)

    qt = _transpose_bf16(q_ref[0])
    for g in range(N_KV):
        rows = slice(g * HEAD_DIM, (g + 1) * HEAD_DIM)
        qs_ref[...] = jnp.zeros(qs_ref.shape, BF16)
        for j in range(GROUP):
            h = GROUP * g + j
            qs_ref[rows, j * tq:(j + 1) * tq] = qt[h * HEAD_DIM:(h + 1) * HEAD_DIM, :]
        m_ref[...] = jnp.full(m_ref.shape, NEG, F32)
        acc_ref[...] = jnp.zeros(acc_ref.shape, F32)

        kc = kc_ref[0]
        for cols in col_blocks:
            score_block(kc, 2, cols)
        trip(CTX_LEN, 2, vct_ref[rows, :], latent_keys(0), 0)

        def body(i, carry, rows=rows):
            j = 3 * i
            trip(tk, 0, vt_ref[j, rows, :], latent_keys(j + 1), 1)
            trip(tk, 1, vt_ref[j + 1, rows, :], latent_keys(j + 2), 2)
            trip(tk, 2, vt_ref[j + 2, rows, :], latent_keys(j + 3), 0)
            return carry

        n_loop = (n_chunks - 1) // 3
        lax.fori_loop(0, n_loop, body, 0)
        for j in range(3 * n_loop, n_chunks):
            cur = j % 3
            last = j == n_chunks - 1
            trip(tk, cur, vt_ref[j, rows, :], None if last else latent_keys(j + 1),
                 None if last else (cur + 1) % 3)
        o = acc_ref[0:HEAD_DIM, :] / acc_ref[HEAD_DIM:HEAD_DIM + 1, :]
        for cl in range(GROUP // 2):
            pair = jnp.concatenate([o[:, (2 * cl) * tq:(2 * cl + 1) * tq],
                                    o[:, (2 * cl + 1) * tq:(2 * cl + 2) * tq]], axis=0)
            cc = (GROUP // 2) * g + cl
            o_ref[0, :, cc * LANES:(cc + 1) * LANES] = _transpose_bf16(pair)


def _attn_a(q, k, v, kc, vc, tq=GLOBAL_TQ, tk=GLOBAL_TK, col_blk=MXU_N):
    b = q.shape[0]
    res = lambda n: pl.BlockSpec((1, n, KV_W), lambda bi, i: (bi, 0, 0))
    return pl.pallas_call(
        functools.partial(_attn_a_kernel, tq=tq, tk=tk, col_blk=col_blk),
        grid=(b, SEQ // tq),
        in_specs=[pl.BlockSpec((1, tq, Q_W), lambda bi, i: (bi, i, 0)),
                  res(SEQ), res(SEQ), res(CTX_LEN), res(CTX_LEN)],
        out_specs=pl.BlockSpec((1, tq, Q_W), lambda bi, i: (bi, i, 0)),
        out_shape=jax.ShapeDtypeStruct((b, SEQ, Q_W), BF16),
        scratch_shapes=[pltpu.VMEM((SEQ // tk, KV_W, tk), BF16),
                        pltpu.VMEM((KV_W, CTX_LEN), BF16),
                        pltpu.VMEM((LANES, GROUP * tq), BF16),
                        pltpu.VMEM((1, GROUP * tq), F32),
                        pltpu.VMEM((HEAD_DIM + ONES_ROWS, GROUP * tq), F32),
                        pltpu.VMEM((GROUP * tq // col_blk, tk, col_blk), F32),
                        pltpu.VMEM((GROUP * tq // col_blk, tk, col_blk), F32),
                        pltpu.VMEM((GROUP * tq // col_blk, tk, col_blk), F32),
                        pltpu.VMEM((1, GROUP * tq), F32),
                        pltpu.VMEM((1, GROUP * tq), F32),
                        pltpu.VMEM((1, GROUP * tq), F32)],
        compiler_params=_cparams(("parallel", "arbitrary")),
        name="attn_global",
    )(q, k, v, kc, vc)


def _fill_feature_major(v_ref, vt_ref, blk):
    def fill(j, carry):
        off = pl.multiple_of(j * blk, blk)
        vt_ref[j] = _transpose_bf16(v_ref[0, pl.ds(off, blk), :])
        return carry

    lax.fori_loop(0, v_ref.shape[1] // blk, fill, 0)


def _with_ones_rows(vt):
    return jnp.concatenate([vt, jnp.ones((ONES_ROWS, vt.shape[1]), BF16)], axis=0)


def _attn_c_kernel(sink_ref, q_ref, k_ref, v_ref, kc_ref, vc_ref, o_ref, vt_ref, vct_ref,
                   qs0_ref, qs1_ref, s0_ref, s1_ref, mx0_ref, mx1_ref, band_ref, *, tq):
    win = tq + 2 * WINDOW
    n_keys = win + CTX_LEN
    i = pl.program_id(1)

    @pl.when(i == 0)
    def _():
        _fill_feature_major(v_ref, vt_ref, LANES)
        vct_ref[...] = _transpose_bf16(vc_ref[0])

    ws = pl.multiple_of(jnp.clip(i * tq - WINDOW, 0, SEQ - win), LANES)
    blk0 = ws // LANES
    kw = k_ref[0, pl.ds(ws, win), :]
    kc = kc_ref[0]
    kpos = ws + lax.broadcasted_iota(jnp.int32, (win, tq), 0)
    qpos = i * tq + lax.broadcasted_iota(jnp.int32, (win, tq), 1)
    band_ref[...] = jnp.where(jnp.abs(qpos - kpos) <= WINDOW, 0.0, NEG)
    qt = _transpose_bf16(q_ref[0])
    bufs = ((qs0_ref, s0_ref, mx0_ref), (qs1_ref, s1_ref, mx1_ref))
    col_blocks = [slice(j * tq, (j + 1) * tq) for j in range(GROUP)]

    def stack_queries(g):
        qs_ref = bufs[g][0]
        qs_ref[...] = jnp.zeros(qs_ref.shape, BF16)
        for j in range(GROUP):
            h = GROUP * g + j
            qs_ref[g * HEAD_DIM:(g + 1) * HEAD_DIM, j * tq:(j + 1) * tq] = (
                qt[h * HEAD_DIM:(h + 1) * HEAD_DIM, :])

    def score_block(g, j):
        qs_ref, s_ref, mx_ref = bufs[g]
        cols = col_blocks[j]
        s_win = _dot(kw, qs_ref[:, cols]) + band_ref[...]
        s_ctx = _dot(kc, qs_ref[:, cols])
        s_ref[j, 0:win, :] = s_win
        s_ref[j, win:n_keys, :] = s_ctx
        sink2 = sink_ref[GROUP * g + j] * LOG2E
        mx_ref[:, cols] = jnp.maximum(
            jnp.maximum(s_win.max(axis=0, keepdims=True), s_ctx.max(axis=0, keepdims=True)), sink2)

    def values(g):
        rows = slice(g * HEAD_DIM, (g + 1) * HEAD_DIM)
        blocks = [vt_ref[blk0 + t, rows, :] for t in range(win // LANES)] + [vct_ref[rows, :]]
        return _with_ones_rows(jnp.concatenate(blocks, axis=1))

    def out_block(g, j, v_ext):
        _, s_ref, mx_ref = bufs[g]
        cols = col_blocks[j]
        m = mx_ref[:, cols]
        p = jnp.exp2(s_ref[j] - m).astype(BF16)
        acc = _dot(v_ext, p)
        l = acc[HEAD_DIM:HEAD_DIM + 1, :] + jnp.exp2(sink_ref[GROUP * g + j] * LOG2E - m)
        return acc[0:HEAD_DIM, :] / l

    def store_pair(g, cl, o_even, o_odd):
        c = (GROUP // 2) * g + cl
        o_ref[0, :, c * LANES:(c + 1) * LANES] = _transpose_bf16(
            jnp.concatenate([o_even, o_odd], axis=0))

    stack_queries(0)
    stack_queries(1)
    for j in range(GROUP):
        score_block(0, j)
    v_ext = values(0)
    outs = []
    for j in range(GROUP):
        score_block(1, j)
        outs.append(out_block(0, j, v_ext))
        if j % 2 == 1:
            store_pair(0, j // 2, outs[j - 1], outs[j])
    v_ext = values(1)
    outs = []
    for j in range(GROUP):
        outs.append(out_block(1, j, v_ext))
        if j % 2 == 1:
            store_pair(1, j // 2, outs[j - 1], outs[j])


def _attn_c(sink, q, k, v, kc, vc, tq=WINDOW_TQ):
    b = q.shape[0]
    n_keys = tq + 2 * WINDOW + CTX_LEN
    res = lambda n: pl.BlockSpec((1, n, KV_W), lambda bi, i: (bi, 0, 0))
    return pl.pallas_call(
        functools.partial(_attn_c_kernel, tq=tq),
        grid=(b, SEQ // tq),
        in_specs=[pl.BlockSpec(memory_space=pltpu.SMEM),
                  pl.BlockSpec((1, tq, Q_W), lambda bi, i: (bi, i, 0)),
                  res(SEQ), res(SEQ), res(CTX_LEN), res(CTX_LEN)],
        out_specs=pl.BlockSpec((1, tq, Q_W), lambda bi, i: (bi, i, 0)),
        out_shape=jax.ShapeDtypeStruct((b, SEQ, Q_W), BF16),
        scratch_shapes=[pltpu.VMEM((SEQ // LANES, KV_W, LANES), BF16),
                        pltpu.VMEM((KV_W, CTX_LEN), BF16),
                        pltpu.VMEM((LANES, GROUP * tq), BF16),
                        pltpu.VMEM((LANES, GROUP * tq), BF16),
                        pltpu.VMEM((GROUP, n_keys, tq), F32),
                        pltpu.VMEM((GROUP, n_keys, tq), F32),
                        pltpu.VMEM((1, GROUP * tq), F32),
                        pltpu.VMEM((1, GROUP * tq), F32),
                        pltpu.VMEM((tq + 2 * WINDOW, tq), F32)],
        compiler_params=_cparams(("parallel", "arbitrary")),
        name="attn_window",
    )(sink, q, k, v, kc, vc)


def _nb_bias_kernel(rpb_ref, o_ref, tt_ref):
    ck = lax.broadcasted_iota(jnp.int32, (GRID_W, LANES), 0)
    lane = lax.broadcasted_iota(jnp.int32, (GRID_W, LANES), 1)
    cq = lane & (GRID_W - 1)
    d = ck - cq + (NB_COLS - 1)
    cs = jnp.clip(cq - NB_COLS // 2, 0, GRID_W - NB_COLS)
    col_ok = (ck >= cs) & (ck < cs + NB_COLS)
    n_dc = 2 * NB_COLS - 1
    neg_tile = jnp.full((GRID_W, LANES), NEG, F32)
    even_row = lane < GRID_W
    for h in range(2):
        base = ((pl.program_id(0) * (N_HEADS // 2) + pl.program_id(1)) * 2 + h) * N_REL
        for a in range(2 * NB_ROWS - 1):
            t = neg_tile
            for dd in range(n_dc):
                t = jnp.where(d == dd, rpb_ref[base + a * n_dc + dd] * LOG2E, t)
            tt_ref[a] = jnp.where(col_ok, t, NEG)
        for case, pat in enumerate(NB_PATTERNS):
            for kr in range(NB_WIN_ROWS):
                for qv in range(NB_Q_ROWS // 2):
                    a0, a1 = pat[2 * qv][kr], pat[2 * qv + 1][kr]
                    lo = neg_tile if a0 is None else tt_ref[a0]
                    hi = neg_tile if a1 is None else tt_ref[a1]
                    o_ref[0, case, 0, h, kr * GRID_W:(kr + 1) * GRID_W, qv * LANES:(qv + 1) * LANES] = (
                        jnp.where(even_row, lo, hi))


def _nb_bias(rpb):
    n_case = len(NB_PATTERNS)
    return pl.pallas_call(
        _nb_bias_kernel,
        grid=(DEPTH, N_HEADS // 2),
        in_specs=[pl.BlockSpec(memory_space=pltpu.SMEM)],
        out_specs=pl.BlockSpec((1, n_case, 1, 2, NB_WIN, NB_TQ), lambda l, c: (l, 0, c, 0, 0, 0)),
        out_shape=jax.ShapeDtypeStruct((DEPTH, n_case, N_HEADS // 2, 2, NB_WIN, NB_TQ), F32),
        scratch_shapes=[pltpu.VMEM((2 * NB_ROWS - 1, GRID_W, LANES), F32)],
        compiler_params=_cparams(("arbitrary", "arbitrary")),
        name="nb_bias",
    )(rpb.reshape(-1))


def _pair_attention(qp, s_extra_fn, k_list, v_list):
    t = qp.shape[0]
    lane = lax.broadcasted_iota(jnp.int32, (t, LANES), 1)
    qf = qp.astype(F32)
    outs = []
    for h in range(2):
        qpad = jnp.where((lane // HEAD_DIM) == h, qf, 0.0).astype(BF16)
        s_list = [_dot_nt(qpad, k) for k in k_list]
        s_list = s_extra_fn(h, s_list)
        outs.append(_softmax_pv(s_list, v_list))
    return jnp.where(lane < HEAD_DIM, outs[0], outs[1])


def _attn_b_kernel(q_ref, k_ref, v_ref, kc_ref, vc_ref, bias_ref, o_ref, vt_ref, vct_ref,
                   s0_ref, s1_ref, mx0_ref, mx1_ref):
    i = pl.program_id(1)

    @pl.when(i == 0)
    def _():
        _fill_feature_major(v_ref, vt_ref, NB_KEY_BLK)
        vct_ref[...] = _transpose_bf16(vc_ref[0])

    w0 = jnp.clip(NB_Q_ROWS * i - NB_ROWS // 2, 0, GRID_H - NB_WIN_ROWS)
    off = pl.multiple_of(w0 * GRID_W, NB_KEY_BLK)
    blk0 = w0 // 2
    bufs = ((s0_ref, mx0_ref), (s1_ref, mx1_ref))
    zeros = jnp.zeros((NB_TQ, LANES), BF16)
    head0_lanes = lax.broadcasted_iota(jnp.int32, (NB_TQ, LANES), 1) < HEAD_DIM

    def score_stage(h):
        s_ref, mx_ref = bufs[h % 2]
        c, t = divmod(h, 2)
        cols = slice(c * LANES, (c + 1) * LANES)
        qp = q_ref[0, :, cols]
        qpad = jnp.where(head0_lanes, qp, zeros) if t == 0 else jnp.where(head0_lanes, zeros, qp)
        s_win = _dot_nt(k_ref[0, c, pl.ds(off, NB_WIN), :], qpad) + bias_ref[0, 0, c, t]
        s_ctx = _dot_nt(kc_ref[0, :, cols], qpad)
        s_ref[0:NB_WIN, :] = s_win
        s_ref[NB_WIN:NB_WIN + CTX_LEN, :] = s_ctx
        mx_ref[...] = jnp.maximum(s_win.max(axis=0, keepdims=True), s_ctx.max(axis=0, keepdims=True))

    def pv_stage(h):
        s_ref, mx_ref = bufs[h % 2]
        rows = slice(h * HEAD_DIM, (h + 1) * HEAD_DIM)
        p = jnp.exp2(s_ref[...] - mx_ref[...]).astype(BF16)
        blocks = [vt_ref[blk0 + t, rows, :] for t in range(NB_WIN // NB_KEY_BLK)] + [vct_ref[rows, :]]
        acc = _dot(_with_ones_rows(jnp.concatenate(blocks, axis=1)), p)
        return acc[0:HEAD_DIM, :] / acc[HEAD_DIM:HEAD_DIM + 1, :]

    score_stage(0)
    outs = []
    for h in range(N_HEADS):
        if h + 1 < N_HEADS:
            score_stage(h + 1)
        outs.append(pv_stage(h))
        if h % 2 == 1:
            o_ref[0, :, (h // 2) * LANES:(h // 2 + 1) * LANES] = _transpose_bf16(
                jnp.concatenate([outs[h - 1], outs[h]], axis=0))


def _attn_b(q, k, v, kc, vc, bias, layer):
    b = q.shape[0]
    n_tiles = SEQ // NB_TQ
    res = lambda n: pl.BlockSpec((1, n, Q_W), lambda bi, i: (bi, 0, 0), pipeline_mode=pl.Buffered(1))

    n_case = len(NB_PATTERNS)
    interior = max(set(NB_TILE_CASE), key=NB_TILE_CASE.count)
    n_lo, n_hi = interior, n_case - 1 - interior
    case_of = lambda t: min(t, n_lo) + max(t - (n_tiles - 1 - n_hi), 0)
    assert [case_of(t) for t in range(n_tiles)] == NB_TILE_CASE

    def bias_idx(bi, i):
        return (layer, jnp.minimum(i, n_lo) + jnp.maximum(i - (n_tiles - 1 - n_hi), 0), 0, 0, 0, 0)

    return pl.pallas_call(
        _attn_b_kernel,
        grid=(b, n_tiles),
        in_specs=[pl.BlockSpec((1, NB_TQ, Q_W), lambda bi, i: (bi, i, 0)),
                  pl.BlockSpec((1, N_HEADS // 2, SEQ, LANES), lambda bi, i: (bi, 0, 0, 0),
                               pipeline_mode=pl.Buffered(1)),
                  res(SEQ), res(CTX_LEN), res(CTX_LEN),
                  pl.BlockSpec((1, 1, N_HEADS // 2, 2, NB_WIN, NB_TQ), bias_idx)],
        out_specs=pl.BlockSpec((1, NB_TQ, Q_W), lambda bi, i: (bi, i, 0)),
        out_shape=jax.ShapeDtypeStruct((b, SEQ, Q_W), BF16),
        scratch_shapes=[pltpu.VMEM((SEQ // NB_KEY_BLK, Q_W, NB_KEY_BLK), BF16),
                        pltpu.VMEM((Q_W, CTX_LEN), BF16),
                        pltpu.VMEM((NB_WIN + CTX_LEN, NB_TQ), F32),
                        pltpu.VMEM((NB_WIN + CTX_LEN, NB_TQ), F32),
                        pltpu.VMEM((1, NB_TQ), F32),
                        pltpu.VMEM((1, NB_TQ), F32)],
        compiler_params=_cparams(("parallel", "arbitrary")),
        name="attn_neighbourhood",
    )(q, k, v, kc, vc, bias)


def _attn_ctx_kernel(sink_ref, qa_ref, ka_ref, va_ref, qb_ref, kb_ref, vb_ref, qc_ref, kc_ref, vc_ref,
                     oa_ref, ob_ref, oc_ref):
    t = CTX_LEN
    for g in range(N_KV):
        for q_ref, k_ref, v_ref, o_ref, sink in ((qa_ref, ka_ref, va_ref, oa_ref, False),
                                                 (qc_ref, kc_ref, vc_ref, oc_ref, True)):
            qs = _stack_group_q(q_ref[0], g)
            extra = _sink_column(sink_ref, g, t) if sink else None
            o = _softmax_pv([_dot_nt(qs, k_ref[0])], [v_ref[0]], extra)
            chunks = _unstack_group_out(o, g, t)
            for cl in range(2):
                c = 2 * g + cl
                o_ref[0, :, c * LANES:(c + 1) * LANES] = chunks[cl].astype(BF16)
    for c in range(N_HEADS // 2):
        cols = slice(c * LANES, (c + 1) * LANES)
        o = _pair_attention(qb_ref[0, :, cols], lambda h, s_list: s_list,
                            [kb_ref[0, :, cols]], [vb_ref[0, :, cols]])
        ob_ref[0, :, cols] = o.astype(BF16)


def _attn_ctx(sink, qa, ka, va, qb, kb, vb, qc, kc, vc):
    b = qa.shape[0]
    spec = lambda w: pl.BlockSpec((1, CTX_LEN, w), lambda bi: (bi, 0, 0))
    return pl.pallas_call(
        _attn_ctx_kernel,
        grid=(b,),
        in_specs=[pl.BlockSpec(memory_space=pltpu.SMEM),
                  spec(Q_W), spec(KV_W), spec(KV_W), spec(Q_W), spec(Q_W), spec(Q_W),
                  spec(Q_W), spec(KV_W), spec(KV_W)],
        out_specs=[spec(Q_W)] * 3,
        out_shape=[jax.ShapeDtypeStruct((b, CTX_LEN, Q_W), BF16)] * 3,
        compiler_params=_cparams(("parallel",)),
        name="attn_ctx",
    )(sink, qa, ka, va, qb, kb, vb, qc, kc, vc)


def _out_proj_kernel(*refs, final):
    if final:
        ya_ref, yb_ref, yc_ref, sg_ref, x_ref, mod_ref, w_ref, fw_ref, o_ref = refs
    else:
        ya_ref, yb_ref, yc_ref, sg_ref, x_ref, mod_ref, w_ref, o_ref = refs
    acc = None
    for idx, y_ref in enumerate((ya_ref, yb_ref, yc_ref)):
        cols = slice(idx * Q_W, (idx + 1) * Q_W)
        u = (y_ref[0].astype(F32) * sg_ref[0, :, cols].astype(F32)).astype(BF16)
        part = _dot(u, w_ref[cols, :])
        acc = part if acc is None else acc + part
    xn = x_ref[0] + mod_ref[0, 2:3, :] * acc
    if final:
        xn = (xn * lax.rsqrt(jnp.mean(xn * xn, axis=-1, keepdims=True) + EPS)) * fw_ref[...]
    o_ref[0] = xn


def _out_proj(ya, yb, yc, sg, x, mod, w_bf, final_w, tm):
    b, n, _ = x.shape
    final = final_w is not None
    mod_idx = (lambda bi, i: (bi, 0, 0)) if mod.shape[0] > 1 else (lambda bi, i: (0, 0, 0))
    row = lambda w: pl.BlockSpec((1, tm, w), lambda bi, i: (bi, i, 0))
    in_specs = [row(Q_W), row(Q_W), row(Q_W), row(MIX_WIDTH), row(D_MODEL),
                pl.BlockSpec((1, 3, D_MODEL), mod_idx),
                pl.BlockSpec((MIX_WIDTH, D_MODEL), lambda bi, i: (0, 0))]
    args = [ya, yb, yc, sg, x, mod, w_bf]
    if final:
        in_specs.append(pl.BlockSpec((1, D_MODEL), lambda bi, i: (0, 0)))
        args.append(final_w)
    return pl.pallas_call(
        functools.partial(_out_proj_kernel, final=final),
        grid=(b, n // tm),
        in_specs=in_specs,
        out_specs=row(D_MODEL),
        out_shape=jax.ShapeDtypeStruct((b, n, D_MODEL), F32),
        compiler_params=_cparams(("parallel", "parallel")),
        name="out_proj_final" if final else "out_proj",
    )(*args)


def _rope_tables():
    t = jnp.arange(SEQ, dtype=jnp.int32)
    rows = (t // GRID_W).astype(F32)
    cols = (t % GRID_W).astype(F32)
    n_freq = HEAD_DIM // 4
    freq = ROPE_THETA ** (-jnp.arange(n_freq, dtype=F32) / n_freq)
    ang = jnp.concatenate([rows[:, None] * freq, cols[:, None] * freq], axis=-1)
    cos, sin = jnp.cos(ang), jnp.sin(ang)
    cos_t = jnp.tile(jnp.concatenate([cos, cos], axis=-1), (1, LANES // HEAD_DIM))
    sin_t = jnp.tile(jnp.concatenate([-sin, sin], axis=-1), (1, LANES // HEAD_DIM))
    return cos_t, sin_t


def kernel(x, c, ctx, c_ctx, norm_w, ada_w, ada_b, w_in, w_out, q_norm_a, k_norm_a, rpb_b, sink_c,
           final_norm_w):
    bsz = x.shape[0]
    cvecs = jnp.concatenate([c, c_ctx[None], jnp.zeros((8 - bsz - 1, D_MODEL), F32)], axis=0)
    mod = _modulation(cvecs, ada_w, ada_b).reshape(DEPTH, 8, 3, D_MODEL)
    bias = _nb_bias(rpb_b)
    rope_tabs = _rope_tables()
    w_in_bf = w_in.astype(BF16)
    w_out_bf = w_out.astype(BF16)
    cx = ctx
    for l in range(DEPTH):
        need_ctx = l < DEPTH - 1
        mod_x, mod_c = mod[l, :bsz], mod[l, bsz:bsz + 1]
        nw = norm_w[l][None]
        qn_t = jnp.tile(q_norm_a[l], N_HEADS)[None]
        kn_t = jnp.tile(k_norm_a[l], N_KV)[None]
        qa, ka, va, qb, kb, vb, qc, kc, vc, sg = _in_proj(
            x, mod_x, nw, w_in_bf[l], qn_t, kn_t, rope_tabs, tm=ROW_TILE)
        qa_c, ka_c, va_c, qb_c, kb_c, vb_c, qc_c, kc_c, vc_c, sg_c = _in_proj(
            cx, mod_c, nw, w_in_bf[l], qn_t, kn_t, None, tm=CTX_LEN)
        ya = _attn_a(qa, ka, va, ka_c, va_c)
        yb = _attn_b(qb, kb, vb, kb_c, vb_c, bias, l)
        yc = _attn_c(sink_c[l], qc, kc, vc, kc_c, vc_c)
        if need_ctx:
            ya_c, yb_c, yc_c = _attn_ctx(sink_c[l], qa_c, ka_c, va_c, qb_c, kb_c, vb_c, qc_c, kc_c, vc_c)
            cx = _out_proj(ya_c, yb_c, yc_c, sg_c, cx, mod_c, w_out_bf[l], None, tm=CTX_LEN)
        x = _out_proj(ya, yb, yc, sg, x, mod_x, w_out_bf[l],
                      None if need_ctx else final_norm_w[None], tm=ROW_TILE)
    return x
```

```python
import functools

import jax
import jax.numpy as jnp
from jax import lax
from jax.experimental import pallas as pl
from jax.experimental.pallas import tpu as pltpu

D_MODEL = 1024
SEQ = 8192
DEPTH = 2
GRID_W = 64
GRID_H = SEQ // GRID_W
CTX_LEN = 256
HEAD_DIM = 64
N_HEADS = 8
N_KV = 2
GROUP = N_HEADS // N_KV
Q_W = N_HEADS * HEAD_DIM
KV_W = N_KV * HEAD_DIM
MIX_WIDTH = 3 * Q_W
IN_WIDTH = 3 * Q_W + 2 * KV_W + 2 * Q_W + 2 * KV_W + MIX_WIDTH
NB_ROWS = 8
NB_COLS = 16
WINDOW = 128
ROPE_THETA = 10000.0
EPS = 1e-6
SCALE = HEAD_DIM ** -0.5
LOG2E = 1.4426950408889634
NEG = -1e30

LANES = 128
MXU_N = 256
VMEM_LIMIT = 56 * 1024 * 1024

ROW_TILE = 1024
GLOBAL_TQ = 512
GLOBAL_TK = 512
WINDOW_TQ = 256

OFF_AQ, OFF_AK, OFF_AV = 0, 512, 640
OFF_BQ, OFF_BK, OFF_BV = 768, 1280, 1792
OFF_CQ, OFF_CK, OFF_CV = 2304, 2816, 2944
OFF_G = 3072

NB_Q_ROWS = 4
NB_TQ = NB_Q_ROWS * GRID_W
NB_WIN_ROWS = NB_Q_ROWS + NB_ROWS
NB_WIN = NB_WIN_ROWS * GRID_W
NB_KEY_BLK = 2 * GRID_W
N_REL = (2 * NB_ROWS - 1) * (2 * NB_COLS - 1)


def _nb_window_row(r0):
    return min(max(r0 - NB_ROWS // 2, 0), GRID_H - NB_WIN_ROWS)


def _nb_first_key_row(r):
    return min(max(r - NB_ROWS // 2, 0), GRID_H - NB_ROWS)


def _nb_bias_pattern(r0):
    w0 = _nb_window_row(r0)
    pat = []
    for qr in range(NB_Q_ROWS):
        rs = _nb_first_key_row(r0 + qr)
        pat.append(tuple((w0 + kr) - (r0 + qr) + NB_ROWS - 1 if rs <= w0 + kr < rs + NB_ROWS else None
                         for kr in range(NB_WIN_ROWS)))
    return tuple(pat)


def _nb_cases():
    patterns, tile_case = [], []
    for r0 in range(0, GRID_H, NB_Q_ROWS):
        pat = _nb_bias_pattern(r0)
        if pat not in patterns:
            patterns.append(pat)
        tile_case.append(patterns.index(pat))
    return patterns, tile_case


NB_PATTERNS, NB_TILE_CASE = _nb_cases()

F32 = jnp.float32
BF16 = jnp.bfloat16


def _cparams(sem):
    return pltpu.CompilerParams(dimension_semantics=sem, vmem_limit_bytes=VMEM_LIMIT)


def _dot(a, b):
    return jnp.dot(a, b, preferred_element_type=F32)


def _dot_nt(a, b):
    return lax.dot_general(a, b, (((1,), (1,)), ((), ())), preferred_element_type=F32)


def _transpose_bf16(z):
    return z.astype(BF16).T


def _silu(x):
    return x * (1.0 / (1.0 + jnp.exp(-x)))


def _mod_kernel(c_ref, w_ref, b_ref, o_ref):
    cs = _silu(c_ref[...])
    o_ref[0] = jnp.dot(cs, w_ref[0], preferred_element_type=F32,
                       precision=lax.Precision.HIGHEST) + b_ref[0]


def _modulation(cvecs, ada_w, ada_b):
    tn = 1024
    return pl.pallas_call(
        _mod_kernel,
        grid=(DEPTH, 3 * D_MODEL // tn),
        in_specs=[
            pl.BlockSpec((8, D_MODEL), lambda l, j: (0, 0)),
            pl.BlockSpec((1, D_MODEL, tn), lambda l, j: (l, 0, j)),
            pl.BlockSpec((1, 1, tn), lambda l, j: (l, 0, j)),
        ],
        out_specs=pl.BlockSpec((1, 8, tn), lambda l, j: (l, 0, j)),
        out_shape=jax.ShapeDtypeStruct((DEPTH, 8, 3 * D_MODEL), F32),
        compiler_params=_cparams(("arbitrary", "arbitrary")),
        name="adaln_mod",
    )(cvecs, ada_w, ada_b.reshape(DEPTH, 1, 3 * D_MODEL))


def _head_rms(z, wn):
    w = z.shape[1]
    r = lax.broadcasted_iota(jnp.int32, (w, w), 0) // HEAD_DIM
    c = lax.broadcasted_iota(jnp.int32, (w, w), 1) // HEAD_DIM
    ones = (r == c).astype(BF16)
    ss = _dot((z * z).astype(BF16), ones)
    return (z * lax.rsqrt(ss * (1.0 / HEAD_DIM) + EPS)) * wn


def _rope(z, cos_t, sin_t):
    w = z.shape[1]
    reps = w // LANES
    if reps > 1:
        cos_t = jnp.concatenate([cos_t] * reps, axis=1)
        sin_t = jnp.concatenate([sin_t] * reps, axis=1)
    lane = lax.broadcasted_iota(jnp.int32, z.shape, 1)
    partner = jnp.where((lane & (HEAD_DIM // 2)) == 0,
                        pltpu.roll(z, w - HEAD_DIM // 2, 1), pltpu.roll(z, HEAD_DIM // 2, 1))
    return z * cos_t + partner * sin_t


def _in_proj_kernel(*refs, rope):
    if rope:
        (x_ref, mod_ref, nw_ref, w_ref, qn_ref, kn_ref, cos_ref, sin_ref,
         qa_ref, ka_ref, va_ref, qb_ref, kb_ref, vb_ref, qc_ref, kc_ref, vc_ref, sg_ref) = refs
        cos_t, sin_t = cos_ref[...], sin_ref[...]
        rot = lambda z: _rope(z, cos_t, sin_t)
    else:
        (x_ref, mod_ref, nw_ref, w_ref, qn_ref, kn_ref,
         qa_ref, ka_ref, va_ref, qb_ref, kb_ref, vb_ref, qc_ref, kc_ref, vc_ref, sg_ref) = refs
        rot = lambda z: z
    x = x_ref[0]
    y = (x * lax.rsqrt(jnp.mean(x * x, axis=-1, keepdims=True) + EPS)) * nw_ref[...]
    h = (y * (1.0 + mod_ref[0, 1:2, :]) + mod_ref[0, 0:1, :]).astype(BF16)

    def proj(off, width):
        return _dot(h, w_ref[:, off:off + width])

    q_scale = SCALE * LOG2E if rope else SCALE
    gate_w = MIX_WIDTH // 3
    groups = [
        (OFF_AQ, Q_W, qa_ref, lambda z: rot(_head_rms(z, qn_ref[...])) * q_scale),
        (OFF_AK, KV_W, ka_ref, lambda z: rot(_head_rms(z, kn_ref[...]))),
        (OFF_AV, KV_W, va_ref, lambda z: z),
        (OFF_BQ, Q_W, qb_ref, lambda z: z * q_scale),
        (OFF_BK, Q_W, kb_ref, lambda z: z),
        (OFF_BV, Q_W, vb_ref, lambda z: z),
        (OFF_CQ, Q_W, qc_ref, lambda z: rot(z) * q_scale),
        (OFF_CK, KV_W, kc_ref, lambda z: rot(z)),
        (OFF_CV, KV_W, vc_ref, lambda z: z),
    ] + [(OFF_G + t * gate_w, gate_w, (sg_ref, t * gate_w), _silu) for t in range(3)]
    z = proj(groups[0][0], groups[0][1])
    for gi, (off, width, dst, epilogue) in enumerate(groups):
        if gi + 1 < len(groups):
            z_next = proj(groups[gi + 1][0], groups[gi + 1][1])
        out = epilogue(z).astype(BF16)
        if isinstance(dst, tuple):
            dst[0][0, :, dst[1]:dst[1] + width] = out
        elif dst is kb_ref and rope:
            for c in range(N_HEADS // 2):
                kb_ref[0, c] = out[:, c * LANES:(c + 1) * LANES]
        else:
            dst[0] = out
        z = z_next


def _in_proj(x, mod, nw, w_bf, qn_t, kn_t, rope_tabs, tm):
    b, n, _ = x.shape
    rope = rope_tabs is not None
    mod_idx = (lambda bi, i: (bi, 0, 0)) if mod.shape[0] > 1 else (lambda bi, i: (0, 0, 0))
    const2 = lambda bi, i: (0, 0)
    in_specs = [
        pl.BlockSpec((1, tm, D_MODEL), lambda bi, i: (bi, i, 0)),
        pl.BlockSpec((1, 3, D_MODEL), mod_idx),
        pl.BlockSpec((1, D_MODEL), const2),
        pl.BlockSpec((D_MODEL, IN_WIDTH), const2, pipeline_mode=pl.Buffered(1)),
        pl.BlockSpec((1, Q_W), const2),
        pl.BlockSpec((1, KV_W), const2),
    ]
    args = [x, mod, nw, w_bf, qn_t, kn_t]
    if rope:
        in_specs += [pl.BlockSpec((tm, LANES), lambda bi, i: (i, 0))] * 2
        args += list(rope_tabs)
    widths = (Q_W, KV_W, KV_W, Q_W, Q_W, Q_W, Q_W, KV_W, KV_W, MIX_WIDTH)
    out_specs = [pl.BlockSpec((1, tm, w), lambda bi, i: (bi, i, 0)) for w in widths]
    out_shape = [jax.ShapeDtypeStruct((b, n, w), BF16) for w in widths]
    if rope:
        out_specs[4] = pl.BlockSpec((1, N_HEADS // 2, tm, LANES), lambda bi, i: (bi, 0, i, 0))
        out_shape[4] = jax.ShapeDtypeStruct((b, N_HEADS // 2, n, LANES), BF16)
    return pl.pallas_call(
        functools.partial(_in_proj_kernel, rope=rope),
        grid=(b, n // tm),
        in_specs=in_specs,
        out_specs=out_specs,
        out_shape=out_shape,
        compiler_params=_cparams(("parallel", "parallel")),
        name="in_proj_rope" if rope else "in_proj_ctx",
    )(*args)


def _stack_group_q(q, g):
    t = q.shape[0]
    lane = lax.broadcasted_iota(jnp.int32, (t, LANES), 1)
    keep = (lane // HEAD_DIM) == g
    parts = []
    for j in range(GROUP):
        h = GROUP * g + j
        chunk = q[:, (h // 2) * LANES:(h // 2 + 1) * LANES].astype(F32)
        if h % 2 != g:
            chunk = pltpu.roll(chunk, HEAD_DIM, 1)
        parts.append(jnp.where(keep, chunk, 0.0).astype(BF16))
    return jnp.concatenate(parts, axis=0)


def _unstack_group_out(o, g, t):
    lane = lax.broadcasted_iota(jnp.int32, (t, LANES), 1)
    left = lane < HEAD_DIM
    chunks = []
    for cl in range(2):
        even = o[(2 * cl) * t:(2 * cl + 1) * t]
        odd = o[(2 * cl + 1) * t:(2 * cl + 2) * t]
        if g == 0:
            odd = pltpu.roll(odd, HEAD_DIM, 1)
        else:
            even = pltpu.roll(even, HEAD_DIM, 1)
        chunks.append(jnp.where(left, even, odd))
    return chunks


def _softmax_pv(s_list, v_list, extra_logit=None):
    m = s_list[0].max(axis=1, keepdims=True)
    for s in s_list[1:]:
        m = jnp.maximum(m, s.max(axis=1, keepdims=True))
    if extra_logit is not None:
        m = jnp.maximum(m, extra_logit)
    l = None
    o = None
    for s, v in zip(s_list, v_list):
        p = jnp.exp(s - m)
        ps = p.sum(axis=1, keepdims=True)
        po = _dot(p.astype(BF16), v)
        l = ps if l is None else l + ps
        o = po if o is None else o + po
    if extra_logit is not None:
        l = l + jnp.exp(extra_logit - m)
    return o / l


def _sink_column(sink_ref, g, t):
    row = lax.broadcasted_iota(jnp.int32, (GROUP * t, 1), 0) // t
    col = jnp.full((GROUP * t, 1), sink_ref[GROUP * g], F32)
    for j in range(1, GROUP):
        col = jnp.where(row == j, sink_ref[GROUP * g + j], col)
    return col


ONES_ROWS = 16


def _attn_a_kernel(q_ref, k_ref, v_ref, kc_ref, vc_ref, o_ref, vt_ref, vct_ref, qs_ref, m_ref, acc_ref,
                   s0_ref, s1_ref, s2_ref, mx0_ref, mx1_ref, mx2_ref, *, tq, tk, col_blk):
    n_chunks = SEQ // tk

    @pl.when(pl.program_id(1) == 0)
    def _():
        for j in range(SEQ // tk):
            vt_ref[j] = _transpose_bf16(v_ref[0, j * tk:(j + 1) * tk, :])
        vct_ref[...] = _transpose_bf16(vc_ref[0])

    col_blocks = [slice(c0, c0 + col_blk) for c0 in range(0, GROUP * tq, col_blk)]
    bufs = ((s0_ref, mx0_ref), (s1_ref, mx1_ref), (s2_ref, mx2_ref))

    def score_block(k, buf, cols):
        s_ref, mx_ref = bufs[buf]
        s = _dot(k, qs_ref[:, cols])
        s_ref[cols.start // col_blk, 0:k.shape[0], :] = s
        mx_ref[:, cols] = s.max(axis=0, keepdims=True)

    def absorb_block(n, buf, v_ext, cols):
        s_ref, mx_ref = bufs[buf]
        m_prev = m_ref[:, cols]
        m_new = jnp.maximum(m_prev, mx_ref[:, cols])
        alpha = jnp.exp2(m_prev - m_new)
        p = jnp.exp2(s_ref[cols.start // col_blk, 0:n, :] - m_new).astype(BF16)
        acc_ref[:, cols] = alpha * acc_ref[:, cols] + _dot(v_ext, p)
        m_ref[:, cols] = m_new

    def latent_keys(j):
        return k_ref[0, pl.ds(pl.multiple_of(j * tk, tk), tk), :]

    def trip(n_cur, cur, vt_g, k_next, nxt):
        v_ext = _with_ones_rows(vt_g)
        for cols in col_blocks:
            if k_next is not None:
                score_block(k_next, nxt, cols)
            absorb_block(n_cur, cur, v_ext, cols)

    qt = _transpose_bf16(q_ref[0])
    for g in range(N_KV):
        rows = slice(g * HEAD_DIM, (g + 1) * HEAD_DIM)
        qs_ref[...] = jnp.zeros(qs_ref.shape, BF16)
        for j in range(GROUP):
            h = GROUP * g + j
            qs_ref[rows, j * tq:(j + 1) * tq] = qt[h * HEAD_DIM:(h + 1) * HEAD_DIM, :]
        m_ref[...] = jnp.full(m_ref.shape, NEG, F32)
        acc_ref[...] = jnp.zeros(acc_ref.shape, F32)

        kc = kc_ref[0]
        for cols in col_blocks:
            score_block(kc, 2, cols)
        trip(CTX_LEN, 2, vct_ref[rows, :], latent_keys(0), 0)

        def body(i, carry, rows=rows):
            j = 3 * i
            trip(tk, 0, vt_ref[j, rows, :], latent_keys(j + 1), 1)
            trip(tk, 1, vt_ref[j + 1, rows, :], latent_keys(j + 2), 2)
            trip(tk, 2, vt_ref[j + 2, rows, :], latent_keys(j + 3), 0)
            return carry

        n_loop = (n_chunks - 1) // 3
        lax.fori_loop(0, n_loop, body, 0)
        for j in range(3 * n_loop, n_chunks):
            cur = j % 3
            last = j == n_chunks - 1
            trip(tk, cur, vt_ref[j, rows, :], None if last else latent_keys(j + 1),
                 None if last else (cur + 1) % 3)
        o = acc_ref[0:HEAD_DIM, :] / acc_ref[HEAD_DIM:HEAD_DIM + 1, :]
        for cl in range(GROUP // 2):
            pair = jnp.concatenate([o[:, (2 * cl) * tq:(2 * cl + 1) * tq],
                                    o[:, (2 * cl + 1) * tq:(2 * cl + 2) * tq]], axis=0)
            cc = (GROUP // 2) * g + cl
            o_ref[0, :, cc * LANES:(cc + 1) * LANES] = _transpose_bf16(pair)


def _attn_a(q, k, v, kc, vc, tq=GLOBAL_TQ, tk=GLOBAL_TK, col_blk=MXU_N):
    b = q.shape[0]
    res = lambda n: pl.BlockSpec((1, n, KV_W), lambda bi, i: (bi, 0, 0))
    return pl.pallas_call(
        functools.partial(_attn_a_kernel, tq=tq, tk=tk, col_blk=col_blk),
        grid=(b, SEQ // tq),
        in_specs=[pl.BlockSpec((1, tq, Q_W), lambda bi, i: (bi, i, 0)),
                  res(SEQ), res(SEQ), res(CTX_LEN), res(CTX_LEN)],
        out_specs=pl.BlockSpec((1, tq, Q_W), lambda bi, i: (bi, i, 0)),
        out_shape=jax.ShapeDtypeStruct((b, SEQ, Q_W), BF16),
        scratch_shapes=[pltpu.VMEM((SEQ // tk, KV_W, tk), BF16),
                        pltpu.VMEM((KV_W, CTX_LEN), BF16),
                        pltpu.VMEM((LANES, GROUP * tq), BF16),
                        pltpu.VMEM((1, GROUP * tq), F32),
                        pltpu.VMEM((HEAD_DIM + ONES_ROWS, GROUP * tq), F32),
                        pltpu.VMEM((GROUP * tq // col_blk, tk, col_blk), F32),
                        pltpu.VMEM((GROUP * tq // col_blk, tk, col_blk), F32),
                        pltpu.VMEM((GROUP * tq // col_blk, tk, col_blk), F32),
                        pltpu.VMEM((1, GROUP * tq), F32),
                        pltpu.VMEM((1, GROUP * tq), F32),
                        pltpu.VMEM((1, GROUP * tq), F32)],
        compiler_params=_cparams(("parallel", "arbitrary")),
        name="attn_global",
    )(q, k, v, kc, vc)


def _fill_feature_major(v_ref, vt_ref, blk):
    def fill(j, carry):
        off = pl.multiple_of(j * blk, blk)
        vt_ref[j] = _transpose_bf16(v_ref[0, pl.ds(off, blk), :])
        return carry

    lax.fori_loop(0, v_ref.shape[1] // blk, fill, 0)


def _with_ones_rows(vt):
    return jnp.concatenate([vt, jnp.ones((ONES_ROWS, vt.shape[1]), BF16)], axis=0)


def _attn_c_kernel(sink_ref, q_ref, k_ref, v_ref, kc_ref, vc_ref, o_ref, vt_ref, vct_ref,
                   qs0_ref, qs1_ref, s0_ref, s1_ref, mx0_ref, mx1_ref, band_ref, *, tq):
    win = tq + 2 * WINDOW
    n_keys = win + CTX_LEN
    i = pl.program_id(1)

    @pl.when(i == 0)
    def _():
        _fill_feature_major(v_ref, vt_ref, LANES)
        vct_ref[...] = _transpose_bf16(vc_ref[0])

    ws = pl.multiple_of(jnp.clip(i * tq - WINDOW, 0, SEQ - win), LANES)
    blk0 = ws // LANES
    kw = k_ref[0, pl.ds(ws, win), :]
    kc = kc_ref[0]
    kpos = ws + lax.broadcasted_iota(jnp.int32, (win, tq), 0)
    qpos = i * tq + lax.broadcasted_iota(jnp.int32, (win, tq), 1)
    band_ref[...] = jnp.where(jnp.abs(qpos - kpos) <= WINDOW, 0.0, NEG)
    qt = _transpose_bf16(q_ref[0])
    bufs = ((qs0_ref, s0_ref, mx0_ref), (qs1_ref, s1_ref, mx1_ref))
    col_blocks = [slice(j * tq, (j + 1) * tq) for j in range(GROUP)]

    def stack_queries(g):
        qs_ref = bufs[g][0]
        qs_ref[...] = jnp.zeros(qs_ref.shape, BF16)
        for j in range(GROUP):
            h = GROUP * g + j
            qs_ref[g * HEAD_DIM:(g + 1) * HEAD_DIM, j * tq:(j + 1) * tq] = (
                qt[h * HEAD_DIM:(h + 1) * HEAD_DIM, :])

    def score_block(g, j):
        qs_ref, s_ref, mx_ref = bufs[g]
        cols = col_blocks[j]
        s_win = _dot(kw, qs_ref[:, cols]) + band_ref[...]
        s_ctx = _dot(kc, qs_ref[:, cols])
        s_ref[j, 0:win, :] = s_win
        s_ref[j, win:n_keys, :] = s_ctx
        sink2 = sink_ref[GROUP * g + j] * LOG2E
        mx_ref[:, cols] = jnp.maximum(
            jnp.maximum(s_win.max(axis=0, keepdims=True), s_ctx.max(axis=0, keepdims=True)), sink2)

    def values(g):
        rows = slice(g * HEAD_DIM, (g + 1) * HEAD_DIM)
        blocks = [vt_ref[blk0 + t, rows, :] for t in range(win // LANES)] + [vct_ref[rows, :]]
        return _with_ones_rows(jnp.concatenate(blocks, axis=1))

    def out_block(g, j, v_ext):
        _, s_ref, mx_ref = bufs[g]
        cols = col_blocks[j]
        m = mx_ref[:, cols]
        p = jnp.exp2(s_ref[j] - m).astype(BF16)
        acc = _dot(v_ext, p)
        l = acc[HEAD_DIM:HEAD_DIM + 1, :] + jnp.exp2(sink_ref[GROUP * g + j] * LOG2E - m)
        return acc[0:HEAD_DIM, :] / l

    def store_pair(g, cl, o_even, o_odd):
        c = (GROUP // 2) * g + cl
        o_ref[0, :, c * LANES:(c + 1) * LANES] = _transpose_bf16(
            jnp.concatenate([o_even, o_odd], axis=0))

    stack_queries(0)
    stack_queries(1)
    for j in range(GROUP):
        score_block(0, j)
    v_ext = values(0)
    outs = []
    for j in range(GROUP):
        score_block(1, j)
        outs.append(out_block(0, j, v_ext))
        if j % 2 == 1:
            store_pair(0, j // 2, outs[j - 1], outs[j])
    v_ext = values(1)
    outs = []
    for j in range(GROUP):
        outs.append(out_block(1, j, v_ext))
        if j % 2 == 1:
            store_pair(1, j // 2, outs[j - 1], outs[j])


def _attn_c(sink, q, k, v, kc, vc, tq=WINDOW_TQ):
    b = q.shape[0]
    n_keys = tq + 2 * WINDOW + CTX_LEN
    res = lambda n: pl.BlockSpec((1, n, KV_W), lambda bi, i: (bi, 0, 0))
    return pl.pallas_call(
        functools.partial(_attn_c_kernel, tq=tq),
        grid=(b, SEQ // tq),
        in_specs=[pl.BlockSpec(memory_space=pltpu.SMEM),
                  pl.BlockSpec((1, tq, Q_W), lambda bi, i: (bi, i, 0)),
                  res(SEQ), res(SEQ), res(CTX_LEN), res(CTX_LEN)],
        out_specs=pl.BlockSpec((1, tq, Q_W), lambda bi, i: (bi, i, 0)),
        out_shape=jax.ShapeDtypeStruct((b, SEQ, Q_W), BF16),
        scratch_shapes=[pltpu.VMEM((SEQ // LANES, KV_W, LANES), BF16),
                        pltpu.VMEM((KV_W, CTX_LEN), BF16),
                        pltpu.VMEM((LANES, GROUP * tq), BF16),
                        pltpu.VMEM((LANES, GROUP * tq), BF16),
                        pltpu.VMEM((GROUP, n_keys, tq), F32),
                        pltpu.VMEM((GROUP, n_keys, tq), F32),
                        pltpu.VMEM((1, GROUP * tq), F32),
                        pltpu.VMEM((1, GROUP * tq), F32),
                        pltpu.VMEM((tq + 2 * WINDOW, tq), F32)],
        compiler_params=_cparams(("parallel", "arbitrary")),
        name="attn_window",
    )(sink, q, k, v, kc, vc)


def _nb_bias_kernel(rpb_ref, o_ref, tt_ref):
    ck = lax.broadcasted_iota(jnp.int32, (GRID_W, LANES), 0)
    lane = lax.broadcasted_iota(jnp.int32, (GRID_W, LANES), 1)
    cq = lane & (GRID_W - 1)
    d = ck - cq + (NB_COLS - 1)
    cs = jnp.clip(cq - NB_COLS // 2, 0, GRID_W - NB_COLS)
    col_ok = (ck >= cs) & (ck < cs + NB_COLS)
    n_dc = 2 * NB_COLS - 1
    neg_tile = jnp.full((GRID_W, LANES), NEG, F32)
    even_row = lane < GRID_W
    for h in range(2):
        base = ((pl.program_id(0) * (N_HEADS // 2) + pl.program_id(1)) * 2 + h) * N_REL
        for a in range(2 * NB_ROWS - 1):
            t = neg_tile
            for dd in range(n_dc):
                t = jnp.where(d == dd, rpb_ref[base + a * n_dc + dd] * LOG2E, t)
            tt_ref[a] = jnp.where(col_ok, t, NEG)
        for case, pat in enumerate(NB_PATTERNS):
            for kr in range(NB_WIN_ROWS):
                for qv in range(NB_Q_ROWS // 2):
                    a0, a1 = pat[2 * qv][kr], pat[2 * qv + 1][kr]
                    lo = neg_tile if a0 is None else tt_ref[a0]
                    hi = neg_tile if a1 is None else tt_ref[a1]
                    o_ref[0, case, 0, h, kr * GRID_W:(kr + 1) * GRID_W, qv * LANES:(qv + 1) * LANES] = (
                        jnp.where(even_row, lo, hi))


def _nb_bias(rpb):
    n_case = len(NB_PATTERNS)
    return pl.pallas_call(
        _nb_bias_kernel,
        grid=(DEPTH, N_HEADS // 2),
        in_specs=[pl.BlockSpec(memory_space=pltpu.SMEM)],
        out_specs=pl.BlockSpec((1, n_case, 1, 2, NB_WIN, NB_TQ), lambda l, c: (l, 0, c, 0, 0, 0)),
        out_shape=jax.ShapeDtypeStruct((DEPTH, n_case, N_HEADS // 2, 2, NB_WIN, NB_TQ), F32),
        scratch_shapes=[pltpu.VMEM((2 * NB_ROWS - 1, GRID_W, LANES), F32)],
        compiler_params=_cparams(("arbitrary", "arbitrary")),
        name="nb_bias",
    )(rpb.reshape(-1))


def _pair_attention(qp, s_extra_fn, k_list, v_list):
    t = qp.shape[0]
    lane = lax.broadcasted_iota(jnp.int32, (t, LANES), 1)
    qf = qp.astype(F32)
    outs = []
    for h in range(2):
        qpad = jnp.where((lane // HEAD_DIM) == h, qf, 0.0).astype(BF16)
        s_list = [_dot_nt(qpad, k) for k in k_list]
        s_list = s_extra_fn(h, s_list)
        outs.append(_softmax_pv(s_list, v_list))
    return jnp.where(lane < HEAD_DIM, outs[0], outs[1])


def _attn_b_kernel(q_ref, k_ref, v_ref, kc_ref, vc_ref, bias_ref, o_ref, vt_ref, vct_ref,
                   s0_ref, s1_ref, mx0_ref, mx1_ref):
    i = pl.program_id(1)

    @pl.when(i == 0)
    def _():
        _fill_feature_major(v_ref, vt_ref, NB_KEY_BLK)
        vct_ref[...] = _transpose_bf16(vc_ref[0])

    w0 = jnp.clip(NB_Q_ROWS * i - NB_ROWS // 2, 0, GRID_H - NB_WIN_ROWS)
    off = pl.multiple_of(w0 * GRID_W, NB_KEY_BLK)
    blk0 = w0 // 2
    bufs = ((s0_ref, mx0_ref), (s1_ref, mx1_ref))
    zeros = jnp.zeros((NB_TQ, LANES), BF16)
    head0_lanes = lax.broadcasted_iota(jnp.int32, (NB_TQ, LANES), 1) < HEAD_DIM

    def score_stage(h):
        s_ref, mx_ref = bufs[h % 2]
        c, t = divmod(h, 2)
        cols = slice(c * LANES, (c + 1) * LANES)
        qp = q_ref[0, :, cols]
        qpad = jnp.where(head0_lanes, qp, zeros) if t == 0 else jnp.where(head0_lanes, zeros, qp)
        s_win = _dot_nt(k_ref[0, c, pl.ds(off, NB_WIN), :], qpad) + bias_ref[0, 0, c, t]
        s_ctx = _dot_nt(kc_ref[0, :, cols], qpad)
        s_ref[0:NB_WIN, :] = s_win
        s_ref[NB_WIN:NB_WIN + CTX_LEN, :] = s_ctx
        mx_ref[...] = jnp.maximum(s_win.max(axis=0, keepdims=True), s_ctx.max(axis=0, keepdims=True))

    def pv_stage(h):
        s_ref, mx_ref = bufs[h % 2]
        rows = slice(h * HEAD_DIM, (h + 1) * HEAD_DIM)
        p = jnp.exp2(s_ref[...] - mx_ref[...]).astype(BF16)
        blocks = [vt_ref[blk0 + t, rows, :] for t in range(NB_WIN // NB_KEY_BLK)] + [vct_ref[rows, :]]
        acc = _dot(_with_ones_rows(jnp.concatenate(blocks, axis=1)), p)
        return acc[0:HEAD_DIM, :] / acc[HEAD_DIM:HEAD_DIM + 1, :]

    score_stage(0)
    outs = []
    for h in range(N_HEADS):
        if h + 1 < N_HEADS:
            score_stage(h + 1)
        outs.append(pv_stage(h))
        if h % 2 == 1:
            o_ref[0, :, (h // 2) * LANES:(h // 2 + 1) * LANES] = _transpose_bf16(
                jnp.concatenate([outs[h - 1], outs[h]], axis=0))


def _attn_b(q, k, v, kc, vc, bias, layer):
    b = q.shape[0]
    n_tiles = SEQ // NB_TQ
    res = lambda n: pl.BlockSpec((1, n, Q_W), lambda bi, i: (bi, 0, 0), pipeline_mode=pl.Buffered(1))

    n_case = len(NB_PATTERNS)
    interior = max(set(NB_TILE_CASE), key=NB_TILE_CASE.count)
    n_lo, n_hi = interior, n_case - 1 - interior
    case_of = lambda t: min(t, n_lo) + max(t - (n_tiles - 1 - n_hi), 0)
    assert [case_of(t) for t in range(n_tiles)] == NB_TILE_CASE

    def bias_idx(bi, i):
        return (layer, jnp.minimum(i, n_lo) + jnp.maximum(i - (n_tiles - 1 - n_hi), 0), 0, 0, 0, 0)

    return pl.pallas_call(
        _attn_b_kernel,
        grid=(b, n_tiles),
        in_specs=[pl.BlockSpec((1, NB_TQ, Q_W), lambda bi, i: (bi, i, 0)),
                  pl.BlockSpec((1, N_HEADS // 2, SEQ, LANES), lambda bi, i: (bi, 0, 0, 0),
                               pipeline_mode=pl.Buffered(1)),
                  res(SEQ), res(CTX_LEN), res(CTX_LEN),
                  pl.BlockSpec((1, 1, N_HEADS // 2, 2, NB_WIN, NB_TQ), bias_idx)],
        out_specs=pl.BlockSpec((1, NB_TQ, Q_W), lambda bi, i: (bi, i, 0)),
        out_shape=jax.ShapeDtypeStruct((b, SEQ, Q_W), BF16),
        scratch_shapes=[pltpu.VMEM((SEQ // NB_KEY_BLK, Q_W, NB_KEY_BLK), BF16),
                        pltpu.VMEM((Q_W, CTX_LEN), BF16),
                        pltpu.VMEM((NB_WIN + CTX_LEN, NB_TQ), F32),
                        pltpu.VMEM((NB_WIN + CTX_LEN, NB_TQ), F32),
                        pltpu.VMEM((1, NB_TQ), F32),
                        pltpu.VMEM((1, NB_TQ), F32)],
        compiler_params=_cparams(("parallel", "arbitrary")),
        name="attn_neighbourhood",
    )(q, k, v, kc, vc, bias)


def _attn_ctx_kernel(sink_ref, qa_ref, ka_ref, va_ref, qb_ref, kb_ref, vb_ref, qc_ref, kc_ref, vc_ref,
                     oa_ref, ob_ref, oc_ref):
    t = CTX_LEN
    for g in range(N_KV):
        for q_ref, k_ref, v_ref, o_ref, sink in ((qa_ref, ka_ref, va_ref, oa_ref, False),
                                                 (qc_ref, kc_ref, vc_ref, oc_ref, True)):
            qs = _stack_group_q(q_ref[0], g)
            extra = _sink_column(sink_ref, g, t) if sink else None
            o = _softmax_pv([_dot_nt(qs, k_ref[0])], [v_ref[0]], extra)
            chunks = _unstack_group_out(o, g, t)
            for cl in range(2):
                c = 2 * g + cl
                o_ref[0, :, c * LANES:(c + 1) * LANES] = chunks[cl].astype(BF16)
    for c in range(N_HEADS // 2):
        cols = slice(c * LANES, (c + 1) * LANES)
        o = _pair_attention(qb_ref[0, :, cols], lambda h, s_list: s_list,
                            [kb_ref[0, :, cols]], [vb_ref[0, :, cols]])
        ob_ref[0, :, cols] = o.astype(BF16)


def _attn_ctx(sink, qa, ka, va, qb, kb, vb, qc, kc, vc):
    b = qa.shape[0]
    spec = lambda w: pl.BlockSpec((1, CTX_LEN, w), lambda bi: (bi, 0, 0))
    return pl.pallas_call(
        _attn_ctx_kernel,
        grid=(b,),
        in_specs=[pl.BlockSpec(memory_space=pltpu.SMEM),
                  spec(Q_W), spec(KV_W), spec(KV_W), spec(Q_W), spec(Q_W), spec(Q_W),
                  spec(Q_W), spec(KV_W), spec(KV_W)],
        out_specs=[spec(Q_W)] * 3,
        out_shape=[jax.ShapeDtypeStruct((b, CTX_LEN, Q_W), BF16)] * 3,
        compiler_params=_cparams(("parallel",)),
        name="attn_ctx",
    )(sink, qa, ka, va, qb, kb, vb, qc, kc, vc)


def _out_proj_kernel(*refs, final):
    if final:
        ya_ref, yb_ref, yc_ref, sg_ref, x_ref, mod_ref, w_ref, fw_ref, o_ref = refs
    else:
        ya_ref, yb_ref, yc_ref, sg_ref, x_ref, mod_ref, w_ref, o_ref = refs
    acc = None
    for idx, y_ref in enumerate((ya_ref, yb_ref, yc_ref)):
        cols = slice(idx * Q_W, (idx + 1) * Q_W)
        u = (y_ref[0].astype(F32) * sg_ref[0, :, cols].astype(F32)).astype(BF16)
        part = _dot(u, w_ref[cols, :])
        acc = part if acc is None else acc + part
    xn = x_ref[0] + mod_ref[0, 2:3, :] * acc
    if final:
        xn = (xn * lax.rsqrt(jnp.mean(xn * xn, axis=-1, keepdims=True) + EPS)) * fw_ref[...]
    o_ref[0] = xn


def _out_proj(ya, yb, yc, sg, x, mod, w_bf, final_w, tm):
    b, n, _ = x.shape
    final = final_w is not None
    mod_idx = (lambda bi, i: (bi, 0, 0)) if mod.shape[0] > 1 else (lambda bi, i: (0, 0, 0))
    row = lambda w: pl.BlockSpec((1, tm, w), lambda bi, i: (bi, i, 0))
    in_specs = [row(Q_W), row(Q_W), row(Q_W), row(MIX_WIDTH), row(D_MODEL),
                pl.BlockSpec((1, 3, D_MODEL), mod_idx),
                pl.BlockSpec((MIX_WIDTH, D_MODEL), lambda bi, i: (0, 0))]
    args = [ya, yb, yc, sg, x, mod, w_bf]
    if final:
        in_specs.append(pl.BlockSpec((1, D_MODEL), lambda bi, i: (0, 0)))
        args.append(final_w)
    return pl.pallas_call(
        functools.partial(_out_proj_kernel, final=final),
        grid=(b, n // tm),
        in_specs=in_specs,
        out_specs=row(D_MODEL),
        out_shape=jax.ShapeDtypeStruct((b, n, D_MODEL), F32),
        compiler_params=_cparams(("parallel", "parallel")),
        name="out_proj_final" if final else "out_proj",
    )(*args)


def _rope_tables():
    t = jnp.arange(SEQ, dtype=jnp.int32)
    rows = (t // GRID_W).astype(F32)
    cols = (t % GRID_W).astype(F32)
    n_freq = HEAD_DIM // 4
    freq = ROPE_THETA ** (-jnp.arange(n_freq, dtype=F32) / n_freq)
    ang = jnp.concatenate([rows[:, None] * freq, cols[:, None] * freq], axis=-1)
    cos, sin = jnp.cos(ang), jnp.sin(ang)
    cos_t = jnp.tile(jnp.concatenate([cos, cos], axis=-1), (1, LANES // HEAD_DIM))
    sin_t = jnp.tile(jnp.concatenate([-sin, sin], axis=-1), (1, LANES // HEAD_DIM))
    return cos_t, sin_t


def kernel(x, c, ctx, c_ctx, norm_w, ada_w, ada_b, w_in, w_out, q_norm_a, k_norm_a, rpb_b, sink_c,
           final_norm_w):
    bsz = x.shape[0]
    cvecs = jnp.concatenate([c, c_ctx[None], jnp.zeros((8 - bsz - 1, D_MODEL), F32)], axis=0)
    mod = _modulation(cvecs, ada_w, ada_b).reshape(DEPTH, 8, 3, D_MODEL)
    bias = _nb_bias(rpb_b)
    rope_tabs = _rope_tables()
    cx = ctx
    for l in range(DEPTH):
        need_ctx = l < DEPTH - 1
        w_in_l = w_in[l].astype(BF16)
        w_out_l = w_out[l].astype(BF16)
        mod_x, mod_c = mod[l, :bsz], mod[l, bsz:bsz + 1]
        nw = norm_w[l][None]
        qn_t = jnp.tile(q_norm_a[l], N_HEADS)[None]
        kn_t = jnp.tile(k_norm_a[l], N_KV)[None]
        qa, ka, va, qb, kb, vb, qc, kc, vc, sg = _in_proj(
            x, mod_x, nw, w_in_l, qn_t, kn_t, rope_tabs, tm=ROW_TILE)
        qa_c, ka_c, va_c, qb_c, kb_c, vb_c, qc_c, kc_c, vc_c, sg_c = _in_proj(
            cx, mod_c, nw, w_in_l, qn_t, kn_t, None, tm=CTX_LEN)
        ya = _attn_a(qa, ka, va, ka_c, va_c)
        yb = _attn_b(qb, kb, vb, kb_c, vb_c, bias, l)
        yc = _attn_c(sink_c[l], qc, kc, vc, kc_c, vc_c)
        if need_ctx:
            ya_c, yb_c, yc_c = _attn_ctx(sink_c[l], qa_c, ka_c, va_c, qb_c, kb_c, vb_c, qc_c, kc_c, vc_c)
            cx = _out_proj(ya_c, yb_c, yc_c, sg_c, cx, mod_c, w_out_l, None, tm=CTX_LEN)
        x = _out_proj(ya, yb, yc, sg, x, mod_x, w_out_l,
                      None if need_ctx else final_norm_w[None], tm=ROW_TILE)
    return x
```

```python
import functools

import jax
import jax.numpy as jnp
from jax import lax
from jax.experimental import pallas as pl
from jax.experimental.pallas import tpu as pltpu

D_MODEL = 1024
SEQ = 8192
DEPTH = 2
GRID_W = 64
GRID_H = SEQ // GRID_W
CTX_LEN = 256
HEAD_DIM = 64
N_HEADS = 8
N_KV = 2
GROUP = N_HEADS // N_KV
Q_W = N_HEADS * HEAD_DIM
KV_W = N_KV * HEAD_DIM
MIX_WIDTH = 3 * Q_W
IN_WIDTH = 3 * Q_W + 2 * KV_W + 2 * Q_W + 2 * KV_W + MIX_WIDTH
NB_ROWS = 8
NB_COLS = 16
WINDOW = 128
ROPE_THETA = 10000.0
EPS = 1e-6
SCALE = HEAD_DIM ** -0.5
LOG2E = 1.4426950408889634
NEG = -1e30

LANES = 128
MXU_N = 256
VMEM_LIMIT = 56 * 1024 * 1024

ROW_TILE = 1024
GLOBAL_TQ = 512
GLOBAL_TK = 512
WINDOW_TQ = 256

OFF_AQ, OFF_AK, OFF_AV = 0, 512, 640
OFF_BQ, OFF_BK, OFF_BV = 768, 1280, 1792
OFF_CQ, OFF_CK, OFF_CV = 2304, 2816, 2944
OFF_G = 3072

NB_Q_ROWS = 4
NB_TQ = NB_Q_ROWS * GRID_W
NB_WIN_ROWS = NB_Q_ROWS + NB_ROWS
NB_WIN = NB_WIN_ROWS * GRID_W
NB_KEY_BLK = 2 * GRID_W
N_REL = (2 * NB_ROWS - 1) * (2 * NB_COLS - 1)


def _nb_window_row(r0):
    return min(max(r0 - NB_ROWS // 2, 0), GRID_H - NB_WIN_ROWS)


def _nb_first_key_row(r):
    return min(max(r - NB_ROWS // 2, 0), GRID_H - NB_ROWS)


def _nb_bias_pattern(r0):
    w0 = _nb_window_row(r0)
    pat = []
    for qr in range(NB_Q_ROWS):
        rs = _nb_first_key_row(r0 + qr)
        pat.append(tuple((w0 + kr) - (r0 + qr) + NB_ROWS - 1 if rs <= w0 + kr < rs + NB_ROWS else None
                         for kr in range(NB_WIN_ROWS)))
    return tuple(pat)


def _nb_cases():
    patterns, tile_case = [], []
    for r0 in range(0, GRID_H, NB_Q_ROWS):
        pat = _nb_bias_pattern(r0)
        if pat not in patterns:
            patterns.append(pat)
        tile_case.append(patterns.index(pat))
    return patterns, tile_case


NB_PATTERNS, NB_TILE_CASE = _nb_cases()

F32 = jnp.float32
BF16 = jnp.bfloat16


def _cparams(sem):
    return pltpu.CompilerParams(dimension_semantics=sem, vmem_limit_bytes=VMEM_LIMIT)


def _dot(a, b):
    return jnp.dot(a, b, preferred_element_type=F32)


def _dot_nt(a, b):
    return lax.dot_general(a, b, (((1,), (1,)), ((), ())), preferred_element_type=F32)


def _transpose_bf16(z):
    return z.astype(BF16).T


def _silu(x):
    return x * (1.0 / (1.0 + jnp.exp(-x)))


def _mod_kernel(c_ref, w_ref, b_ref, o_ref):
    cs = _silu(c_ref[...])
    o_ref[0] = jnp.dot(cs, w_ref[0], preferred_element_type=F32,
                       precision=lax.Precision.HIGHEST) + b_ref[0]


def _modulation(cvecs, ada_w, ada_b):
    tn = 1024
    return pl.pallas_call(
        _mod_kernel,
        grid=(DEPTH, 3 * D_MODEL // tn),
        in_specs=[
            pl.BlockSpec((8, D_MODEL), lambda l, j: (0, 0)),
            pl.BlockSpec((1, D_MODEL, tn), lambda l, j: (l, 0, j)),
            pl.BlockSpec((1, 1, tn), lambda l, j: (l, 0, j)),
        ],
        out_specs=pl.BlockSpec((1, 8, tn), lambda l, j: (l, 0, j)),
        out_shape=jax.ShapeDtypeStruct((DEPTH, 8, 3 * D_MODEL), F32),
        compiler_params=_cparams(("arbitrary", "arbitrary")),
        name="adaln_mod",
    )(cvecs, ada_w, ada_b.reshape(DEPTH, 1, 3 * D_MODEL))


def _head_rms(z, wn):
    w = z.shape[1]
    r = lax.broadcasted_iota(jnp.int32, (w, w), 0) // HEAD_DIM
    c = lax.broadcasted_iota(jnp.int32, (w, w), 1) // HEAD_DIM
    ones = (r == c).astype(BF16)
    ss = _dot((z * z).astype(BF16), ones)
    return (z * lax.rsqrt(ss * (1.0 / HEAD_DIM) + EPS)) * wn


def _rope(z, cos_t, sin_t):
    w = z.shape[1]
    reps = w // LANES
    if reps > 1:
        cos_t = jnp.concatenate([cos_t] * reps, axis=1)
        sin_t = jnp.concatenate([sin_t] * reps, axis=1)
    lane = lax.broadcasted_iota(jnp.int32, z.shape, 1)
    partner = jnp.where((lane & (HEAD_DIM // 2)) == 0,
                        pltpu.roll(z, w - HEAD_DIM // 2, 1), pltpu.roll(z, HEAD_DIM // 2, 1))
    return z * cos_t + partner * sin_t


def _in_proj_kernel(*refs, rope):
    if rope:
        (x_ref, mod_ref, nw_ref, w_ref, qn_ref, kn_ref, cos_ref, sin_ref,
         qa_ref, ka_ref, va_ref, qb_ref, kb_ref, vb_ref, qc_ref, kc_ref, vc_ref, sg_ref) = refs
        cos_t, sin_t = cos_ref[...], sin_ref[...]
        rot = lambda z: _rope(z, cos_t, sin_t)
    else:
        (x_ref, mod_ref, nw_ref, w_ref, qn_ref, kn_ref,
         qa_ref, ka_ref, va_ref, qb_ref, kb_ref, vb_ref, qc_ref, kc_ref, vc_ref, sg_ref) = refs
        rot = lambda z: z
    x = x_ref[0]
    y = (x * lax.rsqrt(jnp.mean(x * x, axis=-1, keepdims=True) + EPS)) * nw_ref[...]
    h = (y * (1.0 + mod_ref[0, 1:2, :]) + mod_ref[0, 0:1, :]).astype(BF16)

    def proj(off, width):
        return _dot(h, w_ref[0, :, off:off + width])

    q_scale = SCALE * LOG2E if rope else SCALE
    gate_w = MIX_WIDTH // 3
    groups = [
        (OFF_AQ, Q_W, qa_ref, lambda z: rot(_head_rms(z, qn_ref[...])) * q_scale),
        (OFF_AK, KV_W, ka_ref, lambda z: rot(_head_rms(z, kn_ref[...]))),
        (OFF_AV, KV_W, va_ref, lambda z: z),
        (OFF_BQ, Q_W, qb_ref, lambda z: z * q_scale),
        (OFF_BK, Q_W, kb_ref, lambda z: z),
        (OFF_BV, Q_W, vb_ref, lambda z: z),
        (OFF_CQ, Q_W, qc_ref, lambda z: rot(z) * q_scale),
        (OFF_CK, KV_W, kc_ref, lambda z: rot(z)),
        (OFF_CV, KV_W, vc_ref, lambda z: z),
    ] + [(OFF_G + t * gate_w, gate_w, (sg_ref, t * gate_w), _silu) for t in range(3)]
    z = proj(groups[0][0], groups[0][1])
    for gi, (off, width, dst, epilogue) in enumerate(groups):
        if gi + 1 < len(groups):
            z_next = proj(groups[gi + 1][0], groups[gi + 1][1])
        out = epilogue(z).astype(BF16)
        if isinstance(dst, tuple):
            dst[0][0, :, dst[1]:dst[1] + width] = out
        elif dst is kb_ref and rope:
            for c in range(N_HEADS // 2):
                kb_ref[0, c] = out[:, c * LANES:(c + 1) * LANES]
        else:
            dst[0] = out
        z = z_next


def _in_proj(x, mod, nw, w_bf, layer, qn_t, kn_t, rope_tabs, tm):
    b, n, _ = x.shape
    rope = rope_tabs is not None
    mod_idx = (lambda bi, i: (bi, 0, 0)) if mod.shape[0] > 1 else (lambda bi, i: (0, 0, 0))
    const2 = lambda bi, i: (0, 0)
    in_specs = [
        pl.BlockSpec((1, tm, D_MODEL), lambda bi, i: (bi, i, 0)),
        pl.BlockSpec((1, 3, D_MODEL), mod_idx),
        pl.BlockSpec((1, D_MODEL), const2),
        pl.BlockSpec((1, D_MODEL, IN_WIDTH), lambda bi, i: (layer, 0, 0), pipeline_mode=pl.Buffered(1)),
        pl.BlockSpec((1, Q_W), const2),
        pl.BlockSpec((1, KV_W), const2),
    ]
    args = [x, mod, nw, w_bf, qn_t, kn_t]
    if rope:
        in_specs += [pl.BlockSpec((tm, LANES), lambda bi, i: (i, 0))] * 2
        args += list(rope_tabs)
    widths = (Q_W, KV_W, KV_W, Q_W, Q_W, Q_W, Q_W, KV_W, KV_W, MIX_WIDTH)
    out_specs = [pl.BlockSpec((1, tm, w), lambda bi, i: (bi, i, 0)) for w in widths]
    out_shape = [jax.ShapeDtypeStruct((b, n, w), BF16) for w in widths]
    if rope:
        out_specs[4] = pl.BlockSpec((1, N_HEADS // 2, tm, LANES), lambda bi, i: (bi, 0, i, 0))
        out_shape[4] = jax.ShapeDtypeStruct((b, N_HEADS // 2, n, LANES), BF16)
    return pl.pallas_call(
        functools.partial(_in_proj_kernel, rope=rope),
        grid=(b, n // tm),
        in_specs=in_specs,
        out_specs=out_specs,
        out_shape=out_shape,
        compiler_params=_cparams(("parallel", "parallel")),
        name="in_proj_rope" if rope else "in_proj_ctx",
    )(*args)


def _stack_group_q(q, g):
    t = q.shape[0]
    lane = lax.broadcasted_iota(jnp.int32, (t, LANES), 1)
    keep = (lane // HEAD_DIM) == g
    parts = []
    for j in range(GROUP):
        h = GROUP * g + j
        chunk = q[:, (h // 2) * LANES:(h // 2 + 1) * LANES].astype(F32)
        if h % 2 != g:
            chunk = pltpu.roll(chunk, HEAD_DIM, 1)
        parts.append(jnp.where(keep, chunk, 0.0).astype(BF16))
    return jnp.concatenate(parts, axis=0)


def _unstack_group_out(o, g, t):
    lane = lax.broadcasted_iota(jnp.int32, (t, LANES), 1)
    left = lane < HEAD_DIM
    chunks = []
    for cl in range(2):
        even = o[(2 * cl) * t:(2 * cl + 1) * t]
        odd = o[(2 * cl + 1) * t:(2 * cl + 2) * t]
        if g == 0:
            odd = pltpu.roll(odd, HEAD_DIM, 1)
        else:
            even = pltpu.roll(even, HEAD_DIM, 1)
        chunks.append(jnp.where(left, even, odd))
    return chunks


def _softmax_pv(s_list, v_list, extra_logit=None):
    m = s_list[0].max(axis=1, keepdims=True)
    for s in s_list[1:]:
        m = jnp.maximum(m, s.max(axis=1, keepdims=True))
    if extra_logit is not None:
        m = jnp.maximum(m, extra_logit)
    l = None
    o = None
    for s, v in zip(s_list, v_list):
        p = jnp.exp(s - m)
        ps = p.sum(axis=1, keepdims=True)
        po = _dot(p.astype(BF16), v)
        l = ps if l is None else l + ps
        o = po if o is None else o + po
    if extra_logit is not None:
        l = l + jnp.exp(extra_logit - m)
    return o / l


def _sink_column(sink_ref, g, t):
    row = lax.broadcasted_iota(jnp.int32, (GROUP * t, 1), 0) // t
    col = jnp.full((GROUP * t, 1), sink_ref[GROUP * g], F32)
    for j in range(1, GROUP):
        col = jnp.where(row == j, sink_ref[GROUP * g + j], col)
    return col


ONES_ROWS = 16


def _attn_a_kernel(q_ref, k_ref, v_ref, kc_ref, vc_ref, o_ref, vt_ref, vct_ref, qs_ref, m_ref, acc_ref,
                   s0_ref, s1_ref, s2_ref, mx0_ref, mx1_ref, mx2_ref, *, tq, tk, col_blk):
    n_chunks = SEQ // tk

    @pl.when(pl.program_id(1) == 0)
    def _():
        for j in range(SEQ // tk):
            vt_ref[j] = _transpose_bf16(v_ref[0, j * tk:(j + 1) * tk, :])
        vct_ref[...] = _transpose_bf16(vc_ref[0])

    col_blocks = [slice(c0, c0 + col_blk) for c0 in range(0, GROUP * tq, col_blk)]
    bufs = ((s0_ref, mx0_ref), (s1_ref, mx1_ref), (s2_ref, mx2_ref))

    def score_block(k, buf, cols):
        s_ref, mx_ref = bufs[buf]
        s = _dot(k, qs_ref[:, cols])
        s_ref[cols.start // col_blk, 0:k.shape[0], :] = s
        mx_ref[:, cols] = s.max(axis=0, keepdims=True)

    def absorb_block(n, buf, v_ext, cols):
        s_ref, mx_ref = bufs[buf]
        m_prev = m_ref[:, cols]
        m_new = jnp.maximum(m_prev, mx_ref[:, cols])
        alpha = jnp.exp2(m_prev - m_new)
        p = jnp.exp2(s_ref[cols.start // col_blk, 0:n, :] - m_new).astype(BF16)
        acc_ref[:, cols] = alpha * acc_ref[:, cols] + _dot(v_ext, p)
        m_ref[:, cols] = m_new

    def latent_keys(j):
        return k_ref[0, pl.ds(pl.multiple_of(j * tk, tk), tk), :]

    def trip(n_cur, cur, vt_g, k_next, nxt):
        v_ext = _with_ones_rows(vt_g)
        for cols in col_blocks:
            if k_next is not None:
                score_block(k_next, nxt, cols)
            absorb_block(n_cur, cur, v_ext, cols)

    qt = _transpose_bf16(q_ref[0])
    for g in range(N_KV):
        rows = slice(g * HEAD_DIM, (g + 1) * HEAD_DIM)
        qs_ref[...] = jnp.zeros(qs_ref.shape, BF16)
        for j in range(GROUP):
            h = GROUP * g + j
            qs_ref[rows, j * tq:(j + 1) * tq] = qt[h * HEAD_DIM:(h + 1) * HEAD_DIM, :]
        m_ref[...] = jnp.full(m_ref.shape, NEG, F32)
        acc_ref[...] = jnp.zeros(acc_ref.shape, F32)

        kc = kc_ref[0]
        for cols in col_blocks:
            score_block(kc, 2, cols)
        trip(CTX_LEN, 2, vct_ref[rows, :], latent_keys(0), 0)

        def body(i, carry, rows=rows):
            j = 3 * i
            trip(tk, 0, vt_ref[j, rows, :], latent_keys(j + 1), 1)
            trip(tk, 1, vt_ref[j + 1, rows, :], latent_keys(j + 2), 2)
            trip(tk, 2, vt_ref[j + 2, rows, :], latent_keys(j + 3), 0)
            return carry

        n_loop = (n_chunks - 1) // 3
        lax.fori_loop(0, n_loop, body, 0)
        for j in range(3 * n_loop, n_chunks):
            cur = j % 3
            last = j == n_chunks - 1
            trip(tk, cur, vt_ref[j, rows, :], None if last else latent_keys(j + 1),
                 None if last else (cur + 1) % 3)
        o = acc_ref[0:HEAD_DIM, :] / acc_ref[HEAD_DIM:HEAD_DIM + 1, :]
        for cl in range(GROUP // 2):
            pair = jnp.concatenate([o[:, (2 * cl) * tq:(2 * cl + 1) * tq],
                                    o[:, (2 * cl + 1) * tq:(2 * cl + 2) * tq]], axis=0)
            cc = (GROUP // 2) * g + cl
            o_ref[0, :, cc * LANES:(cc + 1) * LANES] = _transpose_bf16(pair)


def _attn_a(q, k, v, kc, vc, tq=GLOBAL_TQ, tk=GLOBAL_TK, col_blk=MXU_N):
    b = q.shape[0]
    res = lambda n: pl.BlockSpec((1, n, KV_W), lambda bi, i: (bi, 0, 0))
    return pl.pallas_call(
        functools.partial(_attn_a_kernel, tq=tq, tk=tk, col_blk=col_blk),
        grid=(b, SEQ // tq),
        in_specs=[pl.BlockSpec((1, tq, Q_W), lambda bi, i: (bi, i, 0)),
                  res(SEQ), res(SEQ), res(CTX_LEN), res(CTX_LEN)],
        out_specs=pl.BlockSpec((1, tq, Q_W), lambda bi, i: (bi, i, 0)),
        out_shape=jax.ShapeDtypeStruct((b, SEQ, Q_W), BF16),
        scratch_shapes=[pltpu.VMEM((SEQ // tk, KV_W, tk), BF16),
                        pltpu.VMEM((KV_W, CTX_LEN), BF16),
                        pltpu.VMEM((LANES, GROUP * tq), BF16),
                        pltpu.VMEM((1, GROUP * tq), F32),
                        pltpu.VMEM((HEAD_DIM + ONES_ROWS, GROUP * tq), F32),
                        pltpu.VMEM((GROUP * tq // col_blk, tk, col_blk), F32),
                        pltpu.VMEM((GROUP * tq // col_blk, tk, col_blk), F32),
                        pltpu.VMEM((GROUP * tq // col_blk, tk, col_blk), F32),
                        pltpu.VMEM((1, GROUP * tq), F32),
                        pltpu.VMEM((1, GROUP * tq), F32),
                        pltpu.VMEM((1, GROUP * tq), F32)],
        compiler_params=_cparams(("parallel", "arbitrary")),
        name="attn_global",
    )(q, k, v, kc, vc)


def _fill_feature_major(v_ref, vt_ref, blk):
    def fill(j, carry):
        off = pl.multiple_of(j * blk, blk)
        vt_ref[j] = _transpose_bf16(v_ref[0, pl.ds(off, blk), :])
        return carry

    lax.fori_loop(0, v_ref.shape[1] // blk, fill, 0)


def _with_ones_rows(vt):
    return jnp.concatenate([vt, jnp.ones((ONES_ROWS, vt.shape[1]), BF16)], axis=0)


def _attn_c_kernel(sink_ref, q_ref, k_ref, v_ref, kc_ref, vc_ref, o_ref, vt_ref, vct_ref,
                   qs0_ref, qs1_ref, s0_ref, s1_ref, mx0_ref, mx1_ref, band_ref, *, tq):
    win = tq + 2 * WINDOW
    n_keys = win + CTX_LEN
    i = pl.program_id(1)

    @pl.when(i == 0)
    def _():
        _fill_feature_major(v_ref, vt_ref, LANES)
        vct_ref[...] = _transpose_bf16(vc_ref[0])

    ws = pl.multiple_of(jnp.clip(i * tq - WINDOW, 0, SEQ - win), LANES)
    blk0 = ws // LANES
    kw = k_ref[0, pl.ds(ws, win), :]
    kc = kc_ref[0]
    kpos = ws + lax.broadcasted_iota(jnp.int32, (win, tq), 0)
    qpos = i * tq + lax.broadcasted_iota(jnp.int32, (win, tq), 1)
    band_ref[...] = jnp.where(jnp.abs(qpos - kpos) <= WINDOW, 0.0, NEG)
    qt = _transpose_bf16(q_ref[0])
    bufs = ((qs0_ref, s0_ref, mx0_ref), (qs1_ref, s1_ref, mx1_ref))
    col_blocks = [slice(j * tq, (j + 1) * tq) for j in range(GROUP)]

    def stack_queries(g):
        qs_ref = bufs[g][0]
        qs_ref[...] = jnp.zeros(qs_ref.shape, BF16)
        for j in range(GROUP):
            h = GROUP * g + j
            qs_ref[g * HEAD_DIM:(g + 1) * HEAD_DIM, j * tq:(j + 1) * tq] = (
                qt[h * HEAD_DIM:(h + 1) * HEAD_DIM, :])

    def score_block(g, j):
        qs_ref, s_ref, mx_ref = bufs[g]
        cols = col_blocks[j]
        s_win = _dot(kw, qs_ref[:, cols]) + band_ref[...]
        s_ctx = _dot(kc, qs_ref[:, cols])
        s_ref[j, 0:win, :] = s_win
        s_ref[j, win:n_keys, :] = s_ctx
        sink2 = sink_ref[GROUP * g + j] * LOG2E
        mx_ref[:, cols] = jnp.maximum(
            jnp.maximum(s_win.max(axis=0, keepdims=True), s_ctx.max(axis=0, keepdims=True)), sink2)

    def values(g):
        rows = slice(g * HEAD_DIM, (g + 1) * HEAD_DIM)
        blocks = [vt_ref[blk0 + t, rows, :] for t in range(win // LANES)] + [vct_ref[rows, :]]
        return _with_ones_rows(jnp.concatenate(blocks, axis=1))

    def out_block(g, j, v_ext):
        _, s_ref, mx_ref = bufs[g]
        cols = col_blocks[j]
        m = mx_ref[:, cols]
        p = jnp.exp2(s_ref[j] - m).astype(BF16)
        acc = _dot(v_ext, p)
        l = acc[HEAD_DIM:HEAD_DIM + 1, :] + jnp.exp2(sink_ref[GROUP * g + j] * LOG2E - m)
        return acc[0:HEAD_DIM, :] / l

    def store_pair(g, cl, o_even, o_odd):
        c = (GROUP // 2) * g + cl
        o_ref[0, :, c * LANES:(c + 1) * LANES] = _transpose_bf16(
            jnp.concatenate([o_even, o_odd], axis=0))

    stack_queries(0)
    stack_queries(1)
    for j in range(GROUP):
        score_block(0, j)
    v_ext = values(0)
    outs = []
    for j in range(GROUP):
        score_block(1, j)
        outs.append(out_block(0, j, v_ext))
        if j % 2 == 1:
            store_pair(0, j // 2, outs[j - 1], outs[j])
    v_ext = values(1)
    outs = []
    for j in range(GROUP):
        outs.append(out_block(1, j, v_ext))
        if j % 2 == 1:
            store_pair(1, j // 2, outs[j - 1], outs[j])


def _attn_c(sink, q, k, v, kc, vc, tq=WINDOW_TQ):
    b = q.shape[0]
    n_keys = tq + 2 * WINDOW + CTX_LEN
    res = lambda n: pl.BlockSpec((1, n, KV_W), lambda bi, i: (bi, 0, 0))
    return pl.pallas_call(
        functools.partial(_attn_c_kernel, tq=tq),
        grid=(b, SEQ // tq),
        in_specs=[pl.BlockSpec(memory_space=pltpu.SMEM),
                  pl.BlockSpec((1, tq, Q_W), lambda bi, i: (bi, i, 0)),
                  res(SEQ), res(SEQ), res(CTX_LEN), res(CTX_LEN)],
        out_specs=pl.BlockSpec((1, tq, Q_W), lambda bi, i: (bi, i, 0)),
        out_shape=jax.ShapeDtypeStruct((b, SEQ, Q_W), BF16),
        scratch_shapes=[pltpu.VMEM((SEQ // LANES, KV_W, LANES), BF16),
                        pltpu.VMEM((KV_W, CTX_LEN), BF16),
                        pltpu.VMEM((LANES, GROUP * tq), BF16),
                        pltpu.VMEM((LANES, GROUP * tq), BF16),
                        pltpu.VMEM((GROUP, n_keys, tq), F32),
                        pltpu.VMEM((GROUP, n_keys, tq), F32),
                        pltpu.VMEM((1, GROUP * tq), F32),
                        pltpu.VMEM((1, GROUP * tq), F32),
                        pltpu.VMEM((tq + 2 * WINDOW, tq), F32)],
        compiler_params=_cparams(("parallel", "arbitrary")),
        name="attn_window",
    )(sink, q, k, v, kc, vc)


def _nb_bias_kernel(rpb_ref, o_ref, tt_ref):
    ck = lax.broadcasted_iota(jnp.int32, (GRID_W, LANES), 0)
    lane = lax.broadcasted_iota(jnp.int32, (GRID_W, LANES), 1)
    cq = lane & (GRID_W - 1)
    d = ck - cq + (NB_COLS - 1)
    cs = jnp.clip(cq - NB_COLS // 2, 0, GRID_W - NB_COLS)
    col_ok = (ck >= cs) & (ck < cs + NB_COLS)
    n_dc = 2 * NB_COLS - 1
    neg_tile = jnp.full((GRID_W, LANES), NEG, F32)
    even_row = lane < GRID_W
    for h in range(2):
        base = ((pl.program_id(0) * (N_HEADS // 2) + pl.program_id(1)) * 2 + h) * N_REL
        for a in range(2 * NB_ROWS - 1):
            t = neg_tile
            for dd in range(n_dc):
                t = jnp.where(d == dd, rpb_ref[base + a * n_dc + dd] * LOG2E, t)
            tt_ref[a] = jnp.where(col_ok, t, NEG)
        for case, pat in enumerate(NB_PATTERNS):
            for kr in range(NB_WIN_ROWS):
                for qv in range(NB_Q_ROWS // 2):
                    a0, a1 = pat[2 * qv][kr], pat[2 * qv + 1][kr]
                    lo = neg_tile if a0 is None else tt_ref[a0]
                    hi = neg_tile if a1 is None else tt_ref[a1]
                    o_ref[0, case, 0, h, kr * GRID_W:(kr + 1) * GRID_W, qv * LANES:(qv + 1) * LANES] = (
                        jnp.where(even_row, lo, hi))


def _nb_bias(rpb):
    n_case = len(NB_PATTERNS)
    return pl.pallas_call(
        _nb_bias_kernel,
        grid=(DEPTH, N_HEADS // 2),
        in_specs=[pl.BlockSpec(memory_space=pltpu.SMEM)],
        out_specs=pl.BlockSpec((1, n_case, 1, 2, NB_WIN, NB_TQ), lambda l, c: (l, 0, c, 0, 0, 0)),
        out_shape=jax.ShapeDtypeStruct((DEPTH, n_case, N_HEADS // 2, 2, NB_WIN, NB_TQ), F32),
        scratch_shapes=[pltpu.VMEM((2 * NB_ROWS - 1, GRID_W, LANES), F32)],
        compiler_params=_cparams(("arbitrary", "arbitrary")),
        name="nb_bias",
    )(rpb.reshape(-1))


def _pair_attention(qp, s_extra_fn, k_list, v_list):
    t = qp.shape[0]
    lane = lax.broadcasted_iota(jnp.int32, (t, LANES), 1)
    qf = qp.astype(F32)
    outs = []
    for h in range(2):
        qpad = jnp.where((lane // HEAD_DIM) == h, qf, 0.0).astype(BF16)
        s_list = [_dot_nt(qpad, k) for k in k_list]
        s_list = s_extra_fn(h, s_list)
        outs.append(_softmax_pv(s_list, v_list))
    return jnp.where(lane < HEAD_DIM, outs[0], outs[1])


def _attn_b_kernel(q_ref, k_ref, v_ref, kc_ref, vc_ref, bias_ref, o_ref, vt_ref, vct_ref,
                   s0_ref, s1_ref, mx0_ref, mx1_ref):
    i = pl.program_id(1)

    @pl.when(i == 0)
    def _():
        _fill_feature_major(v_ref, vt_ref, NB_KEY_BLK)
        vct_ref[...] = _transpose_bf16(vc_ref[0])

    w0 = jnp.clip(NB_Q_ROWS * i - NB_ROWS // 2, 0, GRID_H - NB_WIN_ROWS)
    off = pl.multiple_of(w0 * GRID_W, NB_KEY_BLK)
    blk0 = w0 // 2
    bufs = ((s0_ref, mx0_ref), (s1_ref, mx1_ref))
    zeros = jnp.zeros((NB_TQ, LANES), BF16)
    head0_lanes = lax.broadcasted_iota(jnp.int32, (NB_TQ, LANES), 1) < HEAD_DIM

    def score_stage(h):
        s_ref, mx_ref = bufs[h % 2]
        c, t = divmod(h, 2)
        cols = slice(c * LANES, (c + 1) * LANES)
        qp = q_ref[0, :, cols]
        qpad = jnp.where(head0_lanes, qp, zeros) if t == 0 else jnp.where(head0_lanes, zeros, qp)
        s_win = _dot_nt(k_ref[0, c, pl.ds(off, NB_WIN), :], qpad) + bias_ref[0, 0, c, t]
        s_ctx = _dot_nt(kc_ref[0, :, cols], qpad)
        s_ref[0:NB_WIN, :] = s_win
        s_ref[NB_WIN:NB_WIN + CTX_LEN, :] = s_ctx
        mx_ref[...] = jnp.maximum(s_win.max(axis=0, keepdims=True), s_ctx.max(axis=0, keepdims=True))

    def pv_stage(h):
        s_ref, mx_ref = bufs[h % 2]
        rows = slice(h * HEAD_DIM, (h + 1) * HEAD_DIM)
        p = jnp.exp2(s_ref[...] - mx_ref[...]).astype(BF16)
        blocks = [vt_ref[blk0 + t, rows, :] for t in range(NB_WIN // NB_KEY_BLK)] + [vct_ref[rows, :]]
        acc = _dot(_with_ones_rows(jnp.concatenate(blocks, axis=1)), p)
        return acc[0:HEAD_DIM, :] / acc[HEAD_DIM:HEAD_DIM + 1, :]

    score_stage(0)
    outs = []
    for h in range(N_HEADS):
        if h + 1 < N_HEADS:
            score_stage(h + 1)
        outs.append(pv_stage(h))
        if h % 2 == 1:
            o_ref[0, :, (h // 2) * LANES:(h // 2 + 1) * LANES] = _transpose_bf16(
                jnp.concatenate([outs[h - 1], outs[h]], axis=0))


def _attn_b(q, k, v, kc, vc, bias, layer):
    b = q.shape[0]
    n_tiles = SEQ // NB_TQ
    res = lambda n: pl.BlockSpec((1, n, Q_W), lambda bi, i: (bi, 0, 0), pipeline_mode=pl.Buffered(1))

    n_case = len(NB_PATTERNS)
    interior = max(set(NB_TILE_CASE), key=NB_TILE_CASE.count)
    n_lo, n_hi = interior, n_case - 1 - interior
    case_of = lambda t: min(t, n_lo) + max(t - (n_tiles - 1 - n_hi), 0)
    assert [case_of(t) for t in range(n_tiles)] == NB_TILE_CASE

    def bias_idx(bi, i):
        return (layer, jnp.minimum(i, n_lo) + jnp.maximum(i - (n_tiles - 1 - n_hi), 0), 0, 0, 0, 0)

    return pl.pallas_call(
        _attn_b_kernel,
        grid=(b, n_tiles),
        in_specs=[pl.BlockSpec((1, NB_TQ, Q_W), lambda bi, i: (bi, i, 0)),
                  pl.BlockSpec((1, N_HEADS // 2, SEQ, LANES), lambda bi, i: (bi, 0, 0, 0),
                               pipeline_mode=pl.Buffered(1)),
                  res(SEQ), res(CTX_LEN), res(CTX_LEN),
                  pl.BlockSpec((1, 1, N_HEADS // 2, 2, NB_WIN, NB_TQ), bias_idx)],
        out_specs=pl.BlockSpec((1, NB_TQ, Q_W), lambda bi, i: (bi, i, 0)),
        out_shape=jax.ShapeDtypeStruct((b, SEQ, Q_W), BF16),
        scratch_shapes=[pltpu.VMEM((SEQ // NB_KEY_BLK, Q_W, NB_KEY_BLK), BF16),
                        pltpu.VMEM((Q_W, CTX_LEN), BF16),
                        pltpu.VMEM((NB_WIN + CTX_LEN, NB_TQ), F32),
                        pltpu.VMEM((NB_WIN + CTX_LEN, NB_TQ), F32),
                        pltpu.VMEM((1, NB_TQ), F32),
                        pltpu.VMEM((1, NB_TQ), F32)],
        compiler_params=_cparams(("parallel", "arbitrary")),
        name="attn_neighbourhood",
    )(q, k, v, kc, vc, bias)


def _attn_ctx_kernel(sink_ref, qa_ref, ka_ref, va_ref, qb_ref, kb_ref, vb_ref, qc_ref, kc_ref, vc_ref,
                     oa_ref, ob_ref, oc_ref):
    t = CTX_LEN
    for g in range(N_KV):
        for q_ref, k_ref, v_ref, o_ref, sink in ((qa_ref, ka_ref, va_ref, oa_ref, False),
                                                 (qc_ref, kc_ref, vc_ref, oc_ref, True)):
            qs = _stack_group_q(q_ref[0], g)
            extra = _sink_column(sink_ref, g, t) if sink else None
            o = _softmax_pv([_dot_nt(qs, k_ref[0])], [v_ref[0]], extra)
            chunks = _unstack_group_out(o, g, t)
            for cl in range(2):
                c = 2 * g + cl
                o_ref[0, :, c * LANES:(c + 1) * LANES] = chunks[cl].astype(BF16)
    for c in range(N_HEADS // 2):
        cols = slice(c * LANES, (c + 1) * LANES)
        o = _pair_attention(qb_ref[0, :, cols], lambda h, s_list: s_list,
                            [kb_ref[0, :, cols]], [vb_ref[0, :, cols]])
        ob_ref[0, :, cols] = o.astype(BF16)


def _attn_ctx(sink, qa, ka, va, qb, kb, vb, qc, kc, vc):
    b = qa.shape[0]
    spec = lambda w: pl.BlockSpec((1, CTX_LEN, w), lambda bi: (bi, 0, 0))
    return pl.pallas_call(
        _attn_ctx_kernel,
        grid=(b,),
        in_specs=[pl.BlockSpec(memory_space=pltpu.SMEM),
                  spec(Q_W), spec(KV_W), spec(KV_W), spec(Q_W), spec(Q_W), spec(Q_W),
                  spec(Q_W), spec(KV_W), spec(KV_W)],
        out_specs=[spec(Q_W)] * 3,
        out_shape=[jax.ShapeDtypeStruct((b, CTX_LEN, Q_W), BF16)] * 3,
        compiler_params=_cparams(("parallel",)),
        name="attn_ctx",
    )(sink, qa, ka, va, qb, kb, vb, qc, kc, vc)


def _out_proj_kernel(*refs, final):
    if final:
        ya_ref, yb_ref, yc_ref, sg_ref, x_ref, mod_ref, w_ref, fw_ref, o_ref = refs
    else:
        ya_ref, yb_ref, yc_ref, sg_ref, x_ref, mod_ref, w_ref, o_ref = refs
    acc = None
    for idx, y_ref in enumerate((ya_ref, yb_ref, yc_ref)):
        cols = slice(idx * Q_W, (idx + 1) * Q_W)
        u = (y_ref[0].astype(F32) * sg_ref[0, :, cols].astype(F32)).astype(BF16)
        part = _dot(u, w_ref[0, cols, :])
        acc = part if acc is None else acc + part
    xn = x_ref[0] + mod_ref[0, 2:3, :] * acc
    if final:
        xn = (xn * lax.rsqrt(jnp.mean(xn * xn, axis=-1, keepdims=True) + EPS)) * fw_ref[...]
    o_ref[0] = xn


def _out_proj(ya, yb, yc, sg, x, mod, w_bf, layer, final_w, tm):
    b, n, _ = x.shape
    final = final_w is not None
    mod_idx = (lambda bi, i: (bi, 0, 0)) if mod.shape[0] > 1 else (lambda bi, i: (0, 0, 0))
    row = lambda w: pl.BlockSpec((1, tm, w), lambda bi, i: (bi, i, 0))
    in_specs = [row(Q_W), row(Q_W), row(Q_W), row(MIX_WIDTH), row(D_MODEL),
                pl.BlockSpec((1, 3, D_MODEL), mod_idx),
                pl.BlockSpec((1, MIX_WIDTH, D_MODEL), lambda bi, i: (layer, 0, 0))]
    args = [ya, yb, yc, sg, x, mod, w_bf]
    if final:
        in_specs.append(pl.BlockSpec((1, D_MODEL), lambda bi, i: (0, 0)))
        args.append(final_w)
    return pl.pallas_call(
        functools.partial(_out_proj_kernel, final=final),
        grid=(b, n // tm),
        in_specs=in_specs,
        out_specs=row(D_MODEL),
        out_shape=jax.ShapeDtypeStruct((b, n, D_MODEL), F32),
        compiler_params=_cparams(("parallel", "parallel")),
        name="out_proj_final" if final else "out_proj",
    )(*args)


def _rope_tables():
    t = jnp.arange(SEQ, dtype=jnp.int32)
    rows = (t // GRID_W).astype(F32)
    cols = (t % GRID_W).astype(F32)
    n_freq = HEAD_DIM // 4
    freq = ROPE_THETA ** (-jnp.arange(n_freq, dtype=F32) / n_freq)
    ang = jnp.concatenate([rows[:, None] * freq, cols[:, None] * freq], axis=-1)
    cos, sin = jnp.cos(ang), jnp.sin(ang)
    cos_t = jnp.tile(jnp.concatenate([cos, cos], axis=-1), (1, LANES // HEAD_DIM))
    sin_t = jnp.tile(jnp.concatenate([-sin, sin], axis=-1), (1, LANES // HEAD_DIM))
    return cos_t, sin_t


def kernel(x, c, ctx, c_ctx, norm_w, ada_w, ada_b, w_in, w_out, q_norm_a, k_norm_a, rpb_b, sink_c,
           final_norm_w):
    bsz = x.shape[0]
    cvecs = jnp.concatenate([c, c_ctx[None], jnp.zeros((8 - bsz - 1, D_MODEL), F32)], axis=0)
    mod = _modulation(cvecs, ada_w, ada_b).reshape(DEPTH, 8, 3, D_MODEL)
    bias = _nb_bias(rpb_b)
    rope_tabs = _rope_tables()
    w_in_bf = w_in.astype(BF16)
    w_out_bf = w_out.astype(BF16)
    cx = ctx
    for l in range(DEPTH):
        need_ctx = l < DEPTH - 1
        mod_x, mod_c = mod[l, :bsz], mod[l, bsz:bsz + 1]
        nw = norm_w[l][None]
        qn_t = jnp.tile(q_norm_a[l], N_HEADS)[None]
        kn_t = jnp.tile(k_norm_a[l], N_KV)[None]
        qa, ka, va, qb, kb, vb, qc, kc, vc, sg = _in_proj(
            x, mod_x, nw, w_in_bf, l, qn_t, kn_t, rope_tabs, tm=ROW_TILE)
        qa_c, ka_c, va_c, qb_c, kb_c, vb_c, qc_c, kc_c, vc_c, sg_c = _in_proj(
            cx, mod_c, nw, w_in_bf, l, qn_t, kn_t, None, tm=CTX_LEN)
        ya = _attn_a(qa, ka, va, ka_c, va_c)
        yb = _attn_b(qb, kb, vb, kb_c, vb_c, bias, l)
        yc = _attn_c(sink_c[l], qc, kc, vc, kc_c, vc_c)
        if need_ctx:
            ya_c, yb_c, yc_c = _attn_ctx(sink_c[l], qa_c, ka_c, va_c, qb_c, kb_c, vb_c, qc_c, kc_c, vc_c)
            cx = _out_proj(ya_c, yb_c, yc_c, sg_c, cx, mod_c, w_out_bf, l, None, tm=CTX_LEN)
        x = _out_proj(ya, yb, yc, sg, x, mod_x, w_out_bf, l,
                      None if need_ctx else final_norm_w[None], tm=ROW_TILE)
    return x
```

```python
import functools

import jax
import jax.numpy as jnp
from jax import lax
from jax.experimental import pallas as pl
from jax.experimental.pallas import tpu as pltpu

D_MODEL = 1024
SEQ = 8192
DEPTH = 2
GRID_W = 64
GRID_H = SEQ // GRID_W
CTX_LEN = 256
HEAD_DIM = 64
N_HEADS = 8
N_KV = 2
GROUP = N_HEADS // N_KV
Q_W = N_HEADS * HEAD_DIM
KV_W = N_KV * HEAD_DIM
MIX_WIDTH = 3 * Q_W
IN_WIDTH = 3 * Q_W + 2 * KV_W + 2 * Q_W + 2 * KV_W + MIX_WIDTH
NB_ROWS = 8
NB_COLS = 16
WINDOW = 128
ROPE_THETA = 10000.0
EPS = 1e-6
SCALE = HEAD_DIM ** -0.5
LOG2E = 1.4426950408889634
NEG = -1e30

LANES = 128
MXU_N = 256
VMEM_LIMIT = 56 * 1024 * 1024

ROW_TILE = 1024
GLOBAL_TQ = 512
GLOBAL_TK = 512
WINDOW_TQ = 256

OFF_AQ, OFF_AK, OFF_AV = 0, 512, 640
OFF_BQ, OFF_BK, OFF_BV = 768, 1280, 1792
OFF_CQ, OFF_CK, OFF_CV = 2304, 2816, 2944
OFF_G = 3072

NB_Q_ROWS = 4
NB_TQ = NB_Q_ROWS * GRID_W
NB_WIN_ROWS = NB_Q_ROWS + NB_ROWS
NB_WIN = NB_WIN_ROWS * GRID_W
NB_KEY_BLK = 2 * GRID_W
NB_BUFS = 3
N_REL = (2 * NB_ROWS - 1) * (2 * NB_COLS - 1)


def _nb_window_row(r0):
    return min(max(r0 - NB_ROWS // 2, 0), GRID_H - NB_WIN_ROWS)


def _nb_first_key_row(r):
    return min(max(r - NB_ROWS // 2, 0), GRID_H - NB_ROWS)


def _nb_bias_pattern(r0):
    w0 = _nb_window_row(r0)
    pat = []
    for qr in range(NB_Q_ROWS):
        rs = _nb_first_key_row(r0 + qr)
        pat.append(tuple((w0 + kr) - (r0 + qr) + NB_ROWS - 1 if rs <= w0 + kr < rs + NB_ROWS else None
                         for kr in range(NB_WIN_ROWS)))
    return tuple(pat)


def _nb_cases():
    patterns, tile_case = [], []
    for r0 in range(0, GRID_H, NB_Q_ROWS):
        pat = _nb_bias_pattern(r0)
        if pat not in patterns:
            patterns.append(pat)
        tile_case.append(patterns.index(pat))
    return patterns, tile_case


NB_PATTERNS, NB_TILE_CASE = _nb_cases()

F32 = jnp.float32
BF16 = jnp.bfloat16


def _cparams(sem):
    return pltpu.CompilerParams(dimension_semantics=sem, vmem_limit_bytes=VMEM_LIMIT)


def _dot(a, b):
    return jnp.dot(a, b, preferred_element_type=F32)


def _dot_nt(a, b):
    return lax.dot_general(a, b, (((1,), (1,)), ((), ())), preferred_element_type=F32)


def _transpose_bf16(z):
    return z.astype(BF16).T


def _silu(x):
    return x * (1.0 / (1.0 + jnp.exp(-x)))


def _mod_kernel(c_ref, w_ref, b_ref, o_ref):
    cs = _silu(c_ref[...])
    o_ref[0] = jnp.dot(cs, w_ref[0], preferred_element_type=F32,
                       precision=lax.Precision.HIGHEST) + b_ref[0]


def _modulation(cvecs, ada_w, ada_b):
    tn = 1024
    return pl.pallas_call(
        _mod_kernel,
        grid=(DEPTH, 3 * D_MODEL // tn),
        in_specs=[
            pl.BlockSpec((8, D_MODEL), lambda l, j: (0, 0)),
            pl.BlockSpec((1, D_MODEL, tn), lambda l, j: (l, 0, j)),
            pl.BlockSpec((1, 1, tn), lambda l, j: (l, 0, j)),
        ],
        out_specs=pl.BlockSpec((1, 8, tn), lambda l, j: (l, 0, j)),
        out_shape=jax.ShapeDtypeStruct((DEPTH, 8, 3 * D_MODEL), F32),
        compiler_params=_cparams(("arbitrary", "arbitrary")),
        name="adaln_mod",
    )(cvecs, ada_w, ada_b.reshape(DEPTH, 1, 3 * D_MODEL))


def _head_rms(z, wn):
    w = z.shape[1]
    r = lax.broadcasted_iota(jnp.int32, (w, w), 0) // HEAD_DIM
    c = lax.broadcasted_iota(jnp.int32, (w, w), 1) // HEAD_DIM
    ones = (r == c).astype(BF16)
    ss = _dot((z * z).astype(BF16), ones)
    return (z * lax.rsqrt(ss * (1.0 / HEAD_DIM) + EPS)) * wn


def _rope(z, cos_t, sin_t):
    w = z.shape[1]
    reps = w // LANES
    if reps > 1:
        cos_t = jnp.concatenate([cos_t] * reps, axis=1)
        sin_t = jnp.concatenate([sin_t] * reps, axis=1)
    lane = lax.broadcasted_iota(jnp.int32, z.shape, 1)
    partner = jnp.where((lane & (HEAD_DIM // 2)) == 0,
                        pltpu.roll(z, w - HEAD_DIM // 2, 1), pltpu.roll(z, HEAD_DIM // 2, 1))
    return z * cos_t + partner * sin_t


def _in_proj_kernel(*refs, rope):
    if rope:
        (x_ref, mod_ref, nw_ref, w_ref, qn_ref, kn_ref, cos_ref, sin_ref,
         qa_ref, ka_ref, va_ref, qb_ref, kb_ref, vb_ref, qc_ref, kc_ref, vc_ref, sg_ref) = refs
        cos_t, sin_t = cos_ref[...], sin_ref[...]
        rot = lambda z: _rope(z, cos_t, sin_t)
    else:
        (x_ref, mod_ref, nw_ref, w_ref, qn_ref, kn_ref,
         qa_ref, ka_ref, va_ref, qb_ref, kb_ref, vb_ref, qc_ref, kc_ref, vc_ref, sg_ref) = refs
        rot = lambda z: z
    x = x_ref[0]
    y = (x * lax.rsqrt(jnp.mean(x * x, axis=-1, keepdims=True) + EPS)) * nw_ref[...]
    h = (y * (1.0 + mod_ref[0, 1:2, :]) + mod_ref[0, 0:1, :]).astype(BF16)

    def proj(off, width):
        return _dot(h, w_ref[:, off:off + width])

    q_scale = SCALE * LOG2E if rope else SCALE
    gate_w = MIX_WIDTH // 3
    groups = [
        (OFF_AQ, Q_W, qa_ref, lambda z: rot(_head_rms(z, qn_ref[...])) * q_scale),
        (OFF_AK, KV_W, ka_ref, lambda z: rot(_head_rms(z, kn_ref[...]))),
        (OFF_AV, KV_W, va_ref, lambda z: z),
        (OFF_BQ, Q_W, qb_ref, lambda z: z * q_scale),
        (OFF_BK, Q_W, kb_ref, lambda z: z),
        (OFF_BV, Q_W, vb_ref, lambda z: z),
        (OFF_CQ, Q_W, qc_ref, lambda z: rot(z) * q_scale),
        (OFF_CK, KV_W, kc_ref, lambda z: rot(z)),
        (OFF_CV, KV_W, vc_ref, lambda z: z),
    ] + [(OFF_G + t * gate_w, gate_w, (sg_ref, t * gate_w), _silu) for t in range(3)]
    z = proj(groups[0][0], groups[0][1])
    for gi, (off, width, dst, epilogue) in enumerate(groups):
        if gi + 1 < len(groups):
            z_next = proj(groups[gi + 1][0], groups[gi + 1][1])
        out = epilogue(z).astype(BF16)
        if isinstance(dst, tuple):
            dst[0][0, :, dst[1]:dst[1] + width] = out
        elif dst is kb_ref and rope:
            for c in range(N_HEADS // 2):
                kb_ref[0, c] = out[:, c * LANES:(c + 1) * LANES]
        else:
            dst[0] = out
        z = z_next


def _in_proj(x, mod, nw, w_bf, qn_t, kn_t, rope_tabs, tm):
    b, n, _ = x.shape
    rope = rope_tabs is not None
    mod_idx = (lambda bi, i: (bi, 0, 0)) if mod.shape[0] > 1 else (lambda bi, i: (0, 0, 0))
    const2 = lambda bi, i: (0, 0)
    in_specs = [
        pl.BlockSpec((1, tm, D_MODEL), lambda bi, i: (bi, i, 0)),
        pl.BlockSpec((1, 3, D_MODEL), mod_idx),
        pl.BlockSpec((1, D_MODEL), const2),
        pl.BlockSpec((D_MODEL, IN_WIDTH), const2, pipeline_mode=pl.Buffered(1)),
        pl.BlockSpec((1, Q_W), const2),
        pl.BlockSpec((1, KV_W), const2),
    ]
    args = [x, mod, nw, w_bf, qn_t, kn_t]
    if rope:
        in_specs += [pl.BlockSpec((tm, LANES), lambda bi, i: (i, 0))] * 2
        args += list(rope_tabs)
    widths = (Q_W, KV_W, KV_W, Q_W, Q_W, Q_W, Q_W, KV_W, KV_W, MIX_WIDTH)
    out_specs = [pl.BlockSpec((1, tm, w), lambda bi, i: (bi, i, 0)) for w in widths]
    out_shape = [jax.ShapeDtypeStruct((b, n, w), BF16) for w in widths]
    if rope:
        out_specs[4] = pl.BlockSpec((1, N_HEADS // 2, tm, LANES), lambda bi, i: (bi, 0, i, 0))
        out_shape[4] = jax.ShapeDtypeStruct((b, N_HEADS // 2, n, LANES), BF16)
    return pl.pallas_call(
        functools.partial(_in_proj_kernel, rope=rope),
        grid=(b, n // tm),
        in_specs=in_specs,
        out_specs=out_specs,
        out_shape=out_shape,
        compiler_params=_cparams(("parallel", "parallel")),
        name="in_proj_rope" if rope else "in_proj_ctx",
    )(*args)


def _stack_group_q(q, g):
    t = q.shape[0]
    lane = lax.broadcasted_iota(jnp.int32, (t, LANES), 1)
    keep = (lane // HEAD_DIM) == g
    parts = []
    for j in range(GROUP):
        h = GROUP * g + j
        chunk = q[:, (h // 2) * LANES:(h // 2 + 1) * LANES].astype(F32)
        if h % 2 != g:
            chunk = pltpu.roll(chunk, HEAD_DIM, 1)
        parts.append(jnp.where(keep, chunk, 0.0).astype(BF16))
    return jnp.concatenate(parts, axis=0)


def _unstack_group_out(o, g, t):
    lane = lax.broadcasted_iota(jnp.int32, (t, LANES), 1)
    left = lane < HEAD_DIM
    chunks = []
    for cl in range(2):
        even = o[(2 * cl) * t:(2 * cl + 1) * t]
        odd = o[(2 * cl + 1) * t:(2 * cl + 2) * t]
        if g == 0:
            odd = pltpu.roll(odd, HEAD_DIM, 1)
        else:
            even = pltpu.roll(even, HEAD_DIM, 1)
        chunks.append(jnp.where(left, even, odd))
    return chunks


def _softmax_pv(s_list, v_list, extra_logit=None):
    m = s_list[0].max(axis=1, keepdims=True)
    for s in s_list[1:]:
        m = jnp.maximum(m, s.max(axis=1, keepdims=True))
    if extra_logit is not None:
        m = jnp.maximum(m, extra_logit)
    l = None
    o = None
    for s, v in zip(s_list, v_list):
        p = jnp.exp(s - m)
        ps = p.sum(axis=1, keepdims=True)
        po = _dot(p.astype(BF16), v)
        l = ps if l is None else l + ps
        o = po if o is None else o + po
    if extra_logit is not None:
        l = l + jnp.exp(extra_logit - m)
    return o / l


def _sink_column(sink_ref, g, t):
    row = lax.broadcasted_iota(jnp.int32, (GROUP * t, 1), 0) // t
    col = jnp.full((GROUP * t, 1), sink_ref[GROUP * g], F32)
    for j in range(1, GROUP):
        col = jnp.where(row == j, sink_ref[GROUP * g + j], col)
    return col


ONES_ROWS = 16


def _attn_a_kernel(q_ref, k_ref, v_ref, kc_ref, vc_ref, o_ref, vt_ref, vct_ref, qs_ref, m_ref, acc_ref,
                   s0_ref, s1_ref, s2_ref, mx0_ref, mx1_ref, mx2_ref, *, tq, tk, col_blk):
    n_chunks = SEQ // tk

    @pl.when(pl.program_id(1) == 0)
    def _():
        for j in range(SEQ // tk):
            vt_ref[j] = _transpose_bf16(v_ref[0, j * tk:(j + 1) * tk, :])
        vct_ref[...] = _transpose_bf16(vc_ref[0])

    col_blocks = [slice(c0, c0 + col_blk) for c0 in range(0, GROUP * tq, col_blk)]
    bufs = ((s0_ref, mx0_ref), (s1_ref, mx1_ref), (s2_ref, mx2_ref))

    def score_block(k, buf, cols):
        s_ref, mx_ref = bufs[buf]
        s = _dot(k, qs_ref[:, cols])
        s_ref[cols.start // col_blk, 0:k.shape[0], :] = s
        mx_ref[:, cols] = s.max(axis=0, keepdims=True)

    def absorb_block(n, buf, v_ext, cols):
        s_ref, mx_ref = bufs[buf]
        m_prev = m_ref[:, cols]
        m_new = jnp.maximum(m_prev, mx_ref[:, cols])
        alpha = jnp.exp2(m_prev - m_new)
        p = jnp.exp2(s_ref[cols.start // col_blk, 0:n, :] - m_new).astype(BF16)
        acc_ref[:, cols] = alpha * acc_ref[:, cols] + _dot(v_ext, p)
        m_ref[:, cols] = m_new

    def latent_keys(j):
        return k_ref[0, pl.ds(pl.multiple_of(j * tk, tk), tk), :]

    def trip(n_cur, cur, vt_g, k_next, nxt):
        v_ext = _with_ones_rows(vt_g)
        for cols in col_blocks:
            if k_next is not None:
                score_block(k_next, nxt, cols)
            absorb_block(n_cur, cur, v_ext, cols)

    qt = _transpose_bf16(q_ref[0])
    for g in range(N_KV):
        rows = slice(g * HEAD_DIM, (g + 1) * HEAD_DIM)
        qs_ref[...] = jnp.zeros(qs_ref.shape, BF16)
        for j in range(GROUP):
            h = GROUP * g + j
            qs_ref[rows, j * tq:(j + 1) * tq] = qt[h * HEAD_DIM:(h + 1) * HEAD_DIM, :]
        m_ref[...] = jnp.full(m_ref.shape, NEG, F32)
        acc_ref[...] = jnp.zeros(acc_ref.shape, F32)

        n_bufs = len(bufs)
        ahead = n_bufs - 2
        kc = kc_ref[0]

        def item_keys(t):
            return kc if t == 0 else latent_keys(t - 1)

        for t in range(ahead):
            for cols in col_blocks:
                score_block(item_keys(t), t % n_bufs, cols)
        trip(CTX_LEN, 0, vct_ref[rows, :], item_keys(ahead), ahead % n_bufs)

        def body(i, carry, rows=rows):
            for u in range(n_bufs):
                j = n_bufs * i + u
                trip(tk, (u + 1) % n_bufs, vt_ref[j, rows, :],
                     latent_keys(j + ahead), (u + 1 + ahead) % n_bufs)
            return carry

        n_loop = (n_chunks - ahead) // n_bufs
        lax.fori_loop(0, n_loop, body, 0)
        for t in range(n_bufs * n_loop + 1, n_chunks + 1):
            more = t + ahead <= n_chunks
            trip(tk, t % n_bufs, vt_ref[t - 1, rows, :],
                 latent_keys(t - 1 + ahead) if more else None, (t + ahead) % n_bufs if more else None)
        o = acc_ref[0:HEAD_DIM, :] / acc_ref[HEAD_DIM:HEAD_DIM + 1, :]
        for cl in range(GROUP // 2):
            pair = jnp.concatenate([o[:, (2 * cl) * tq:(2 * cl + 1) * tq],
                                    o[:, (2 * cl + 1) * tq:(2 * cl + 2) * tq]], axis=0)
            cc = (GROUP // 2) * g + cl
            o_ref[0, :, cc * LANES:(cc + 1) * LANES] = _transpose_bf16(pair)


def _attn_a(q, k, v, kc, vc, tq=GLOBAL_TQ, tk=GLOBAL_TK, col_blk=MXU_N):
    b = q.shape[0]
    res = lambda n: pl.BlockSpec((1, n, KV_W), lambda bi, i: (bi, 0, 0))
    return pl.pallas_call(
        functools.partial(_attn_a_kernel, tq=tq, tk=tk, col_blk=col_blk),
        grid=(b, SEQ // tq),
        in_specs=[pl.BlockSpec((1, tq, Q_W), lambda bi, i: (bi, i, 0)),
                  res(SEQ), res(SEQ), res(CTX_LEN), res(CTX_LEN)],
        out_specs=pl.BlockSpec((1, tq, Q_W), lambda bi, i: (bi, i, 0)),
        out_shape=jax.ShapeDtypeStruct((b, SEQ, Q_W), BF16),
        scratch_shapes=[pltpu.VMEM((SEQ // tk, KV_W, tk), BF16),
                        pltpu.VMEM((KV_W, CTX_LEN), BF16),
                        pltpu.VMEM((LANES, GROUP * tq), BF16),
                        pltpu.VMEM((1, GROUP * tq), F32),
                        pltpu.VMEM((HEAD_DIM + ONES_ROWS, GROUP * tq), F32),
                        pltpu.VMEM((GROUP * tq // col_blk, tk, col_blk), F32),
                        pltpu.VMEM((GROUP * tq // col_blk, tk, col_blk), F32),
                        pltpu.VMEM((GROUP * tq // col_blk, tk, col_blk), F32),
                        pltpu.VMEM((1, GROUP * tq), F32),
                        pltpu.VMEM((1, GROUP * tq), F32),
                        pltpu.VMEM((1, GROUP * tq), F32)],
        compiler_params=_cparams(("parallel", "arbitrary")),
        name="attn_global",
    )(q, k, v, kc, vc)


def _fill_feature_major(v_ref, vt_ref, blk):
    def fill(j, carry):
        off = pl.multiple_of(j * blk, blk)
        vt_ref[j] = _transpose_bf16(v_ref[0, pl.ds(off, blk), :])
        return carry

    lax.fori_loop(0, v_ref.shape[1] // blk, fill, 0)


def _with_ones_rows(vt):
    return jnp.concatenate([vt, jnp.ones((ONES_ROWS, vt.shape[1]), BF16)], axis=0)


def _attn_c_kernel(sink_ref, q_ref, k_ref, v_ref, kc_ref, vc_ref, o_ref, vt_ref, vct_ref,
                   qs0_ref, qs1_ref, s0_ref, s1_ref, mx0_ref, mx1_ref, band_ref, *, tq):
    win = tq + 2 * WINDOW
    n_keys = win + CTX_LEN
    i = pl.program_id(1)

    @pl.when(i == 0)
    def _():
        _fill_feature_major(v_ref, vt_ref, LANES)
        vct_ref[...] = _transpose_bf16(vc_ref[0])

    ws = pl.multiple_of(jnp.clip(i * tq - WINDOW, 0, SEQ - win), LANES)
    blk0 = ws // LANES
    kw = k_ref[0, pl.ds(ws, win), :]
    kc = kc_ref[0]
    kpos = ws + lax.broadcasted_iota(jnp.int32, (win, tq), 0)
    qpos = i * tq + lax.broadcasted_iota(jnp.int32, (win, tq), 1)
    band_ref[...] = jnp.where(jnp.abs(qpos - kpos) <= WINDOW, 0.0, NEG)
    qt = _transpose_bf16(q_ref[0])
    bufs = ((qs0_ref, s0_ref, mx0_ref), (qs1_ref, s1_ref, mx1_ref))
    col_blocks = [slice(j * tq, (j + 1) * tq) for j in range(GROUP)]

    def stack_queries(g):
        qs_ref = bufs[g][0]
        qs_ref[...] = jnp.zeros(qs_ref.shape, BF16)
        for j in range(GROUP):
            h = GROUP * g + j
            qs_ref[g * HEAD_DIM:(g + 1) * HEAD_DIM, j * tq:(j + 1) * tq] = (
                qt[h * HEAD_DIM:(h + 1) * HEAD_DIM, :])

    def score_block(g, j):
        qs_ref, s_ref, mx_ref = bufs[g]
        cols = col_blocks[j]
        s_win = _dot(kw, qs_ref[:, cols]) + band_ref[...]
        s_ctx = _dot(kc, qs_ref[:, cols])
        s_ref[j, 0:win, :] = s_win
        s_ref[j, win:n_keys, :] = s_ctx
        sink2 = sink_ref[GROUP * g + j] * LOG2E
        mx_ref[:, cols] = jnp.maximum(
            jnp.maximum(s_win.max(axis=0, keepdims=True), s_ctx.max(axis=0, keepdims=True)), sink2)

    def values(g):
        rows = slice(g * HEAD_DIM, (g + 1) * HEAD_DIM)
        blocks = [vt_ref[blk0 + t, rows, :] for t in range(win // LANES)] + [vct_ref[rows, :]]
        return _with_ones_rows(jnp.concatenate(blocks, axis=1))

    def out_block(g, j, v_ext):
        _, s_ref, mx_ref = bufs[g]
        cols = col_blocks[j]
        m = mx_ref[:, cols]
        p = jnp.exp2(s_ref[j] - m).astype(BF16)
        acc = _dot(v_ext, p)
        l = acc[HEAD_DIM:HEAD_DIM + 1, :] + jnp.exp2(sink_ref[GROUP * g + j] * LOG2E - m)
        return acc[0:HEAD_DIM, :] / l

    def store_pair(g, cl, o_even, o_odd):
        c = (GROUP // 2) * g + cl
        o_ref[0, :, c * LANES:(c + 1) * LANES] = _transpose_bf16(
            jnp.concatenate([o_even, o_odd], axis=0))

    stack_queries(0)
    stack_queries(1)
    for j in range(GROUP):
        score_block(0, j)
    v_ext = values(0)
    outs = []
    for j in range(GROUP):
        score_block(1, j)
        outs.append(out_block(0, j, v_ext))
        if j % 2 == 1:
            store_pair(0, j // 2, outs[j - 1], outs[j])
    v_ext = values(1)
    outs = []
    for j in range(GROUP):
        outs.append(out_block(1, j, v_ext))
        if j % 2 == 1:
            store_pair(1, j // 2, outs[j - 1], outs[j])


def _attn_c(sink, q, k, v, kc, vc, tq=WINDOW_TQ):
    b = q.shape[0]
    n_keys = tq + 2 * WINDOW + CTX_LEN
    res = lambda n: pl.BlockSpec((1, n, KV_W), lambda bi, i: (bi, 0, 0))
    return pl.pallas_call(
        functools.partial(_attn_c_kernel, tq=tq),
        grid=(b, SEQ // tq),
        in_specs=[pl.BlockSpec(memory_space=pltpu.SMEM),
                  pl.BlockSpec((1, tq, Q_W), lambda bi, i: (bi, i, 0)),
                  res(SEQ), res(SEQ), res(CTX_LEN), res(CTX_LEN)],
        out_specs=pl.BlockSpec((1, tq, Q_W), lambda bi, i: (bi, i, 0)),
        out_shape=jax.ShapeDtypeStruct((b, SEQ, Q_W), BF16),
        scratch_shapes=[pltpu.VMEM((SEQ // LANES, KV_W, LANES), BF16),
                        pltpu.VMEM((KV_W, CTX_LEN), BF16),
                        pltpu.VMEM((LANES, GROUP * tq), BF16),
                        pltpu.VMEM((LANES, GROUP * tq), BF16),
                        pltpu.VMEM((GROUP, n_keys, tq), F32),
                        pltpu.VMEM((GROUP, n_keys, tq), F32),
                        pltpu.VMEM((1, GROUP * tq), F32),
                        pltpu.VMEM((1, GROUP * tq), F32),
                        pltpu.VMEM((tq + 2 * WINDOW, tq), F32)],
        compiler_params=_cparams(("parallel", "arbitrary")),
        name="attn_window",
    )(sink, q, k, v, kc, vc)


def _nb_bias_kernel(rpb_ref, o_ref, tt_ref):
    ck = lax.broadcasted_iota(jnp.int32, (GRID_W, LANES), 0)
    lane = lax.broadcasted_iota(jnp.int32, (GRID_W, LANES), 1)
    cq = lane & (GRID_W - 1)
    d = ck - cq + (NB_COLS - 1)
    cs = jnp.clip(cq - NB_COLS // 2, 0, GRID_W - NB_COLS)
    col_ok = (ck >= cs) & (ck < cs + NB_COLS)
    n_dc = 2 * NB_COLS - 1
    neg_tile = jnp.full((GRID_W, LANES), NEG, F32)
    even_row = lane < GRID_W
    for h in range(2):
        base = ((pl.program_id(0) * (N_HEADS // 2) + pl.program_id(1)) * 2 + h) * N_REL
        for a in range(2 * NB_ROWS - 1):
            t = neg_tile
            for dd in range(n_dc):
                t = jnp.where(d == dd, rpb_ref[base + a * n_dc + dd] * LOG2E, t)
            tt_ref[a] = jnp.where(col_ok, t, NEG)
        for case, pat in enumerate(NB_PATTERNS):
            for kr in range(NB_WIN_ROWS):
                for qv in range(NB_Q_ROWS // 2):
                    a0, a1 = pat[2 * qv][kr], pat[2 * qv + 1][kr]
                    lo = neg_tile if a0 is None else tt_ref[a0]
                    hi = neg_tile if a1 is None else tt_ref[a1]
                    o_ref[0, case, 0, h, kr * GRID_W:(kr + 1) * GRID_W, qv * LANES:(qv + 1) * LANES] = (
                        jnp.where(even_row, lo, hi))


def _nb_bias(rpb):
    n_case = len(NB_PATTERNS)
    return pl.pallas_call(
        _nb_bias_kernel,
        grid=(DEPTH, N_HEADS // 2),
        in_specs=[pl.BlockSpec(memory_space=pltpu.SMEM)],
        out_specs=pl.BlockSpec((1, n_case, 1, 2, NB_WIN, NB_TQ), lambda l, c: (l, 0, c, 0, 0, 0)),
        out_shape=jax.ShapeDtypeStruct((DEPTH, n_case, N_HEADS // 2, 2, NB_WIN, NB_TQ), F32),
        scratch_shapes=[pltpu.VMEM((2 * NB_ROWS - 1, GRID_W, LANES), F32)],
        compiler_params=_cparams(("arbitrary", "arbitrary")),
        name="nb_bias",
    )(rpb.reshape(-1))


def _pair_attention(qp, s_extra_fn, k_list, v_list):
    t = qp.shape[0]
    lane = lax.broadcasted_iota(jnp.int32, (t, LANES), 1)
    qf = qp.astype(F32)
    outs = []
    for h in range(2):
        qpad = jnp.where((lane // HEAD_DIM) == h, qf, 0.0).astype(BF16)
        s_list = [_dot_nt(qpad, k) for k in k_list]
        s_list = s_extra_fn(h, s_list)
        outs.append(_softmax_pv(s_list, v_list))
    return jnp.where(lane < HEAD_DIM, outs[0], outs[1])


def _attn_b_kernel(q_ref, k_ref, v_ref, kc_ref, vc_ref, bias_ref, o_ref, vt_ref, vct_ref,
                   s_all_ref, mx_all_ref):
    i = pl.program_id(1)

    @pl.when(i == 0)
    def _():
        _fill_feature_major(v_ref, vt_ref, NB_KEY_BLK)
        vct_ref[...] = _transpose_bf16(vc_ref[0])

    w0 = jnp.clip(NB_Q_ROWS * i - NB_ROWS // 2, 0, GRID_H - NB_WIN_ROWS)
    off = pl.multiple_of(w0 * GRID_W, NB_KEY_BLK)
    blk0 = w0 // 2
    n_bufs = s_all_ref.shape[0]
    ahead = n_bufs - 1
    zeros = jnp.zeros((NB_TQ, LANES), BF16)
    head0_lanes = lax.broadcasted_iota(jnp.int32, (NB_TQ, LANES), 1) < HEAD_DIM

    def score_stage(h):
        buf = h % n_bufs
        c, t = divmod(h, 2)
        cols = slice(c * LANES, (c + 1) * LANES)
        qp = q_ref[0, :, cols]
        qpad = jnp.where(head0_lanes, qp, zeros) if t == 0 else jnp.where(head0_lanes, zeros, qp)
        s_win = _dot_nt(k_ref[0, c, pl.ds(off, NB_WIN), :], qpad) + bias_ref[0, 0, c, t]
        s_ctx = _dot_nt(kc_ref[0, :, cols], qpad)
        s_all_ref[buf, 0:NB_WIN, :] = s_win
        s_all_ref[buf, NB_WIN:NB_WIN + CTX_LEN, :] = s_ctx
        mx_all_ref[buf] = jnp.maximum(s_win.max(axis=0, keepdims=True),
                                      s_ctx.max(axis=0, keepdims=True))

    def pv_stage(h):
        buf = h % n_bufs
        rows = slice(h * HEAD_DIM, (h + 1) * HEAD_DIM)
        p = jnp.exp2(s_all_ref[buf] - mx_all_ref[buf]).astype(BF16)
        blocks = [vt_ref[blk0 + t, rows, :] for t in range(NB_WIN // NB_KEY_BLK)] + [vct_ref[rows, :]]
        acc = _dot(_with_ones_rows(jnp.concatenate(blocks, axis=1)), p)
        return acc[0:HEAD_DIM, :] / acc[HEAD_DIM:HEAD_DIM + 1, :]

    for h in range(ahead):
        score_stage(h)
    outs = []
    for h in range(N_HEADS):
        if h + ahead < N_HEADS:
            score_stage(h + ahead)
        outs.append(pv_stage(h))
        if h % 2 == 1:
            o_ref[0, :, (h // 2) * LANES:(h // 2 + 1) * LANES] = _transpose_bf16(
                jnp.concatenate([outs[h - 1], outs[h]], axis=0))


def _attn_b(q, k, v, kc, vc, bias, layer):
    b = q.shape[0]
    n_tiles = SEQ // NB_TQ
    res = lambda n: pl.BlockSpec((1, n, Q_W), lambda bi, i: (bi, 0, 0), pipeline_mode=pl.Buffered(1))

    n_case = len(NB_PATTERNS)
    interior = max(set(NB_TILE_CASE), key=NB_TILE_CASE.count)
    n_lo, n_hi = interior, n_case - 1 - interior
    case_of = lambda t: min(t, n_lo) + max(t - (n_tiles - 1 - n_hi), 0)
    assert [case_of(t) for t in range(n_tiles)] == NB_TILE_CASE

    def bias_idx(bi, i):
        return (layer, jnp.minimum(i, n_lo) + jnp.maximum(i - (n_tiles - 1 - n_hi), 0), 0, 0, 0, 0)

    return pl.pallas_call(
        _attn_b_kernel,
        grid=(b, n_tiles),
        in_specs=[pl.BlockSpec((1, NB_TQ, Q_W), lambda bi, i: (bi, i, 0)),
                  pl.BlockSpec((1, N_HEADS // 2, SEQ, LANES), lambda bi, i: (bi, 0, 0, 0),
                               pipeline_mode=pl.Buffered(1)),
                  res(SEQ), res(CTX_LEN), res(CTX_LEN),
                  pl.BlockSpec((1, 1, N_HEADS // 2, 2, NB_WIN, NB_TQ), bias_idx)],
        out_specs=pl.BlockSpec((1, NB_TQ, Q_W), lambda bi, i: (bi, i, 0)),
        out_shape=jax.ShapeDtypeStruct((b, SEQ, Q_W), BF16),
        scratch_shapes=[pltpu.VMEM((SEQ // NB_KEY_BLK, Q_W, NB_KEY_BLK), BF16),
                        pltpu.VMEM((Q_W, CTX_LEN), BF16),
                        pltpu.VMEM((NB_BUFS, NB_WIN + CTX_LEN, NB_TQ), F32),
                        pltpu.VMEM((NB_BUFS, 1, NB_TQ), F32)],
        compiler_params=_cparams(("parallel", "arbitrary")),
        name="attn_neighbourhood",
    )(q, k, v, kc, vc, bias)


def _attn_ctx_kernel(sink_ref, qa_ref, ka_ref, va_ref, qb_ref, kb_ref, vb_ref, qc_ref, kc_ref, vc_ref,
                     oa_ref, ob_ref, oc_ref):
    t = CTX_LEN
    for g in range(N_KV):
        for q_ref, k_ref, v_ref, o_ref, sink in ((qa_ref, ka_ref, va_ref, oa_ref, False),
                                                 (qc_ref, kc_ref, vc_ref, oc_ref, True)):
            qs = _stack_group_q(q_ref[0], g)
            extra = _sink_column(sink_ref, g, t) if sink else None
            o = _softmax_pv([_dot_nt(qs, k_ref[0])], [v_ref[0]], extra)
            chunks = _unstack_group_out(o, g, t)
            for cl in range(2):
                c = 2 * g + cl
                o_ref[0, :, c * LANES:(c + 1) * LANES] = chunks[cl].astype(BF16)
    for c in range(N_HEADS // 2):
        cols = slice(c * LANES, (c + 1) * LANES)
        o = _pair_attention(qb_ref[0, :, cols], lambda h, s_list: s_list,
                            [kb_ref[0, :, cols]], [vb_ref[0, :, cols]])
        ob_ref[0, :, cols] = o.astype(BF16)


def _attn_ctx(sink, qa, ka, va, qb, kb, vb, qc, kc, vc):
    b = qa.shape[0]
    spec = lambda w: pl.BlockSpec((1, CTX_LEN, w), lambda bi: (bi, 0, 0))
    return pl.pallas_call(
        _attn_ctx_kernel,
        grid=(b,),
        in_specs=[pl.BlockSpec(memory_space=pltpu.SMEM),
                  spec(Q_W), spec(KV_W), spec(KV_W), spec(Q_W), spec(Q_W), spec(Q_W),
                  spec(Q_W), spec(KV_W), spec(KV_W)],
        out_specs=[spec(Q_W)] * 3,
        out_shape=[jax.ShapeDtypeStruct((b, CTX_LEN, Q_W), BF16)] * 3,
        compiler_params=_cparams(("parallel",)),
        name="attn_ctx",
    )(sink, qa, ka, va, qb, kb, vb, qc, kc, vc)


def _out_proj_kernel(*refs, final):
    if final:
        ya_ref, yb_ref, yc_ref, sg_ref, x_ref, mod_ref, w_ref, fw_ref, o_ref = refs
    else:
        ya_ref, yb_ref, yc_ref, sg_ref, x_ref, mod_ref, w_ref, o_ref = refs
    acc = None
    for idx, y_ref in enumerate((ya_ref, yb_ref, yc_ref)):
        cols = slice(idx * Q_W, (idx + 1) * Q_W)
        u = (y_ref[0].astype(F32) * sg_ref[0, :, cols].astype(F32)).astype(BF16)
        part = _dot(u, w_ref[cols, :])
        acc = part if acc is None else acc + part
    xn = x_ref[0] + mod_ref[0, 2:3, :] * acc
    if final:
        xn = (xn * lax.rsqrt(jnp.mean(xn * xn, axis=-1, keepdims=True) + EPS)) * fw_ref[...]
    o_ref[0] = xn


def _out_proj(ya, yb, yc, sg, x, mod, w_bf, final_w, tm):
    b, n, _ = x.shape
    final = final_w is not None
    mod_idx = (lambda bi, i: (bi, 0, 0)) if mod.shape[0] > 1 else (lambda bi, i: (0, 0, 0))
    row = lambda w: pl.BlockSpec((1, tm, w), lambda bi, i: (bi, i, 0))
    in_specs = [row(Q_W), row(Q_W), row(Q_W), row(MIX_WIDTH), row(D_MODEL),
                pl.BlockSpec((1, 3, D_MODEL), mod_idx),
                pl.BlockSpec((MIX_WIDTH, D_MODEL), lambda bi, i: (0, 0))]
    args = [ya, yb, yc, sg, x, mod, w_bf]
    if final:
        in_specs.append(pl.BlockSpec((1, D_MODEL), lambda bi, i: (0, 0)))
        args.append(final_w)
    return pl.pallas_call(
        functools.partial(_out_proj_kernel, final=final),
        grid=(b, n // tm),
        in_specs=in_specs,
        out_specs=row(D_MODEL),
        out_shape=jax.ShapeDtypeStruct((b, n, D_MODEL), F32),
        compiler_params=_cparams(("parallel", "parallel")),
        name="out_proj_final" if final else "out_proj",
    )(*args)


def _rope_tables():
    t = jnp.arange(SEQ, dtype=jnp.int32)
    rows = (t // GRID_W).astype(F32)
    cols = (t % GRID_W).astype(F32)
    n_freq = HEAD_DIM // 4
    freq = ROPE_THETA ** (-jnp.arange(n_freq, dtype=F32) / n_freq)
    ang = jnp.concatenate([rows[:, None] * freq, cols[:, None] * freq], axis=-1)
    cos, sin = jnp.cos(ang), jnp.sin(ang)
    cos_t = jnp.tile(jnp.concatenate([cos, cos], axis=-1), (1, LANES // HEAD_DIM))
    sin_t = jnp.tile(jnp.concatenate([-sin, sin], axis=-1), (1, LANES // HEAD_DIM))
    return cos_t, sin_t


def kernel(x, c, ctx, c_ctx, norm_w, ada_w, ada_b, w_in, w_out, q_norm_a, k_norm_a, rpb_b, sink_c,
           final_norm_w):
    bsz = x.shape[0]
    cvecs = jnp.concatenate([c, c_ctx[None], jnp.zeros((8 - bsz - 1, D_MODEL), F32)], axis=0)
    mod = _modulation(cvecs, ada_w, ada_b).reshape(DEPTH, 8, 3, D_MODEL)
    bias = _nb_bias(rpb_b)
    rope_tabs = _rope_tables()
    cx = ctx
    for l in range(DEPTH):
        need_ctx = l < DEPTH - 1
        w_in_l = w_in[l].astype(BF16)
        w_out_l = w_out[l].astype(BF16)
        mod_x, mod_c = mod[l, :bsz], mod[l, bsz:bsz + 1]
        nw = norm_w[l][None]
        qn_t = jnp.tile(q_norm_a[l], N_HEADS)[None]
        kn_t = jnp.tile(k_norm_a[l], N_KV)[None]
        qa, ka, va, qb, kb, vb, qc, kc, vc, sg = _in_proj(
            x, mod_x, nw, w_in_l, qn_t, kn_t, rope_tabs, tm=ROW_TILE)
        qa_c, ka_c, va_c, qb_c, kb_c, vb_c, qc_c, kc_c, vc_c, sg_c = _in_proj(
            cx, mod_c, nw, w_in_l, qn_t, kn_t, None, tm=CTX_LEN)
        ya = _attn_a(qa, ka, va, ka_c, va_c)
        yb = _attn_b(qb, kb, vb, kb_c, vb_c, bias, l)
        yc = _attn_c(sink_c[l], qc, kc, vc, kc_c, vc_c)
        if need_ctx:
            ya_c, yb_c, yc_c = _attn_ctx(sink_c[l], qa_c, ka_c, va_c, qb_c, kb_c, vb_c, qc_c, kc_c, vc_c)
            cx = _out_proj(ya_c, yb_c, yc_c, sg_c, cx, mod_c, w_out_l, None, tm=CTX_LEN)
        x = _out_proj(ya, yb, yc, sg, x, mod_x, w_out_l,
                      None if need_ctx else final_norm_w[None], tm=ROW_TILE)
    return x
```

```python
import functools

import jax
import jax.numpy as jnp
from jax import lax
from jax.experimental import pallas as pl
from jax.experimental.pallas import tpu as pltpu

D_MODEL = 1024
SEQ = 8192
DEPTH = 2
GRID_W = 64
GRID_H = SEQ // GRID_W
CTX_LEN = 256
HEAD_DIM = 64
N_HEADS = 8
N_KV = 2
GROUP = N_HEADS // N_KV
Q_W = N_HEADS * HEAD_DIM
KV_W = N_KV * HEAD_DIM
MIX_WIDTH = 3 * Q_W
IN_WIDTH = 3 * Q_W + 2 * KV_W + 2 * Q_W + 2 * KV_W + MIX_WIDTH
NB_ROWS = 8
NB_COLS = 16
WINDOW = 128
ROPE_THETA = 10000.0
EPS = 1e-6
SCALE = HEAD_DIM ** -0.5
LOG2E = 1.4426950408889634
NEG = -1e30

LANES = 128
MXU_N = 256
VMEM_LIMIT = 56 * 1024 * 1024

ROW_TILE = 1024
GLOBAL_TQ = 512
GLOBAL_TK = 512
WINDOW_TQ = 256

OFF_AQ, OFF_AK, OFF_AV = 0, 512, 640
OFF_BQ, OFF_BK, OFF_BV = 768, 1280, 1792
OFF_CQ, OFF_CK, OFF_CV = 2304, 2816, 2944
OFF_G = 3072

NB_Q_ROWS = 4
NB_TQ = NB_Q_ROWS * GRID_W
NB_WIN_ROWS = NB_Q_ROWS + NB_ROWS
NB_WIN = NB_WIN_ROWS * GRID_W
NB_KEY_BLK = 2 * GRID_W
NB_BUFS = 3
N_REL = (2 * NB_ROWS - 1) * (2 * NB_COLS - 1)


def _nb_window_row(r0):
    return min(max(r0 - NB_ROWS // 2, 0), GRID_H - NB_WIN_ROWS)


def _nb_first_key_row(r):
    return min(max(r - NB_ROWS // 2, 0), GRID_H - NB_ROWS)


def _nb_bias_pattern(r0):
    w0 = _nb_window_row(r0)
    pat = []
    for qr in range(NB_Q_ROWS):
        rs = _nb_first_key_row(r0 + qr)
        pat.append(tuple((w0 + kr) - (r0 + qr) + NB_ROWS - 1 if rs <= w0 + kr < rs + NB_ROWS else None
                         for kr in range(NB_WIN_ROWS)))
    return tuple(pat)


def _nb_cases():
    patterns, tile_case = [], []
    for r0 in range(0, GRID_H, NB_Q_ROWS):
        pat = _nb_bias_pattern(r0)
        if pat not in patterns:
            patterns.append(pat)
        tile_case.append(patterns.index(pat))
    return patterns, tile_case


NB_PATTERNS, NB_TILE_CASE = _nb_cases()

F32 = jnp.float32
BF16 = jnp.bfloat16


def _cparams(sem):
    return pltpu.CompilerParams(dimension_semantics=sem, vmem_limit_bytes=VMEM_LIMIT)


def _dot(a, b):
    return jnp.dot(a, b, preferred_element_type=F32)


def _dot_nt(a, b):
    return lax.dot_general(a, b, (((1,), (1,)), ((), ())), preferred_element_type=F32)


def _transpose_bf16(z):
    return z.astype(BF16).T


def _silu(x):
    return x * (1.0 / (1.0 + jnp.exp(-x)))


def _mod_kernel(c_ref, w_ref, b_ref, o_ref):
    cs = _silu(c_ref[...])
    o_ref[0] = jnp.dot(cs, w_ref[0], preferred_element_type=F32,
                       precision=lax.Precision.HIGHEST) + b_ref[0]


def _modulation(cvecs, ada_w, ada_b):
    tn = 1024
    return pl.pallas_call(
        _mod_kernel,
        grid=(DEPTH, 3 * D_MODEL // tn),
        in_specs=[
            pl.BlockSpec((8, D_MODEL), lambda l, j: (0, 0)),
            pl.BlockSpec((1, D_MODEL, tn), lambda l, j: (l, 0, j)),
            pl.BlockSpec((1, 1, tn), lambda l, j: (l, 0, j)),
        ],
        out_specs=pl.BlockSpec((1, 8, tn), lambda l, j: (l, 0, j)),
        out_shape=jax.ShapeDtypeStruct((DEPTH, 8, 3 * D_MODEL), F32),
        compiler_params=_cparams(("arbitrary", "arbitrary")),
        name="adaln_mod",
    )(cvecs, ada_w, ada_b.reshape(DEPTH, 1, 3 * D_MODEL))


def _head_rms(z, wn):
    w = z.shape[1]
    r = lax.broadcasted_iota(jnp.int32, (w, w), 0) // HEAD_DIM
    c = lax.broadcasted_iota(jnp.int32, (w, w), 1) // HEAD_DIM
    ones = (r == c).astype(BF16)
    ss = _dot((z * z).astype(BF16), ones)
    return (z * lax.rsqrt(ss * (1.0 / HEAD_DIM) + EPS)) * wn


def _rope(z, cos_t, sin_t):
    w = z.shape[1]
    reps = w // LANES
    if reps > 1:
        cos_t = jnp.concatenate([cos_t] * reps, axis=1)
        sin_t = jnp.concatenate([sin_t] * reps, axis=1)
    lane = lax.broadcasted_iota(jnp.int32, z.shape, 1)
    partner = jnp.where((lane & (HEAD_DIM // 2)) == 0,
                        pltpu.roll(z, w - HEAD_DIM // 2, 1), pltpu.roll(z, HEAD_DIM // 2, 1))
    return z * cos_t + partner * sin_t


def _in_proj_kernel(*refs, rope):
    if rope:
        (x_ref, mod_ref, nw_ref, w_ref, qn_ref, kn_ref, cos_ref, sin_ref,
         qa_ref, ka_ref, va_ref, qb_ref, kb_ref, vb_ref, qc_ref, kc_ref, vc_ref, sg_ref) = refs
        cos_t, sin_t = cos_ref[...], sin_ref[...]
        rot = lambda z: _rope(z, cos_t, sin_t)
    else:
        (x_ref, mod_ref, nw_ref, w_ref, qn_ref, kn_ref,
         qa_ref, ka_ref, va_ref, qb_ref, kb_ref, vb_ref, qc_ref, kc_ref, vc_ref, sg_ref) = refs
        rot = lambda z: z
    x = x_ref[0]
    y = (x * lax.rsqrt(jnp.mean(x * x, axis=-1, keepdims=True) + EPS)) * nw_ref[...]
    h = (y * (1.0 + mod_ref[0, 1:2, :]) + mod_ref[0, 0:1, :]).astype(BF16)

    def proj(off, width):
        return _dot(h, w_ref[:, off:off + width])

    q_scale = SCALE * LOG2E if rope else SCALE
    gate_w = MIX_WIDTH // 3
    groups = [
        (OFF_AQ, Q_W, qa_ref, lambda z: rot(_head_rms(z, qn_ref[...])) * q_scale),
        (OFF_AK, KV_W, ka_ref, lambda z: rot(_head_rms(z, kn_ref[...]))),
        (OFF_AV, KV_W, va_ref, lambda z: z),
        (OFF_BQ, Q_W, qb_ref, lambda z: z * q_scale),
        (OFF_BK, Q_W, kb_ref, lambda z: z),
        (OFF_BV, Q_W, vb_ref, lambda z: z),
        (OFF_CQ, Q_W, qc_ref, lambda z: rot(z) * q_scale),
        (OFF_CK, KV_W, kc_ref, lambda z: rot(z)),
        (OFF_CV, KV_W, vc_ref, lambda z: z),
    ] + [(OFF_G + t * gate_w, gate_w, (sg_ref, t * gate_w), _silu) for t in range(3)]
    z = proj(groups[0][0], groups[0][1])
    for gi, (off, width, dst, epilogue) in enumerate(groups):
        if gi + 1 < len(groups):
            z_next = proj(groups[gi + 1][0], groups[gi + 1][1])
        out = epilogue(z).astype(BF16)
        if isinstance(dst, tuple):
            dst[0][0, :, dst[1]:dst[1] + width] = out
        elif dst is kb_ref and rope:
            for c in range(N_HEADS // 2):
                kb_ref[0, c] = out[:, c * LANES:(c + 1) * LANES]
        else:
            dst[0] = out
        z = z_next


def _in_proj(x, mod, nw, w_bf, qn_t, kn_t, rope_tabs, tm):
    b, n, _ = x.shape
    rope = rope_tabs is not None
    mod_idx = (lambda bi, i: (bi, 0, 0)) if mod.shape[0] > 1 else (lambda bi, i: (0, 0, 0))
    const2 = lambda bi, i: (0, 0)
    in_specs = [
        pl.BlockSpec((1, tm, D_MODEL), lambda bi, i: (bi, i, 0)),
        pl.BlockSpec((1, 3, D_MODEL), mod_idx),
        pl.BlockSpec((1, D_MODEL), const2),
        pl.BlockSpec((D_MODEL, IN_WIDTH), const2, pipeline_mode=pl.Buffered(1)),
        pl.BlockSpec((1, Q_W), const2),
        pl.BlockSpec((1, KV_W), const2),
    ]
    args = [x, mod, nw, w_bf, qn_t, kn_t]
    if rope:
        in_specs += [pl.BlockSpec((tm, LANES), lambda bi, i: (i, 0))] * 2
        args += list(rope_tabs)
    widths = (Q_W, KV_W, KV_W, Q_W, Q_W, Q_W, Q_W, KV_W, KV_W, MIX_WIDTH)
    out_specs = [pl.BlockSpec((1, tm, w), lambda bi, i: (bi, i, 0)) for w in widths]
    out_shape = [jax.ShapeDtypeStruct((b, n, w), BF16) for w in widths]
    if rope:
        out_specs[4] = pl.BlockSpec((1, N_HEADS // 2, tm, LANES), lambda bi, i: (bi, 0, i, 0))
        out_shape[4] = jax.ShapeDtypeStruct((b, N_HEADS // 2, n, LANES), BF16)
    return pl.pallas_call(
        functools.partial(_in_proj_kernel, rope=rope),
        grid=(b, n // tm),
        in_specs=in_specs,
        out_specs=out_specs,
        out_shape=out_shape,
        compiler_params=_cparams(("parallel", "parallel")),
        name="in_proj_rope" if rope else "in_proj_ctx",
    )(*args)


def _stack_group_q(q, g):
    t = q.shape[0]
    lane = lax.broadcasted_iota(jnp.int32, (t, LANES), 1)
    keep = (lane // HEAD_DIM) == g
    parts = []
    for j in range(GROUP):
        h = GROUP * g + j
        chunk = q[:, (h // 2) * LANES:(h // 2 + 1) * LANES].astype(F32)
        if h % 2 != g:
            chunk = pltpu.roll(chunk, HEAD_DIM, 1)
        parts.append(jnp.where(keep, chunk, 0.0).astype(BF16))
    return jnp.concatenate(parts, axis=0)


def _unstack_group_out(o, g, t):
    lane = lax.broadcasted_iota(jnp.int32, (t, LANES), 1)
    left = lane < HEAD_DIM
    chunks = []
    for cl in range(2):
        even = o[(2 * cl) * t:(2 * cl + 1) * t]
        odd = o[(2 * cl + 1) * t:(2 * cl + 2) * t]
        if g == 0:
            odd = pltpu.roll(odd, HEAD_DIM, 1)
        else:
            even = pltpu.roll(even, HEAD_DIM, 1)
        chunks.append(jnp.where(left, even, odd))
    return chunks


def _softmax_pv(s_list, v_list, extra_logit=None):
    m = s_list[0].max(axis=1, keepdims=True)
    for s in s_list[1:]:
        m = jnp.maximum(m, s.max(axis=1, keepdims=True))
    if extra_logit is not None:
        m = jnp.maximum(m, extra_logit)
    l = None
    o = None
    for s, v in zip(s_list, v_list):
        p = jnp.exp(s - m)
        ps = p.sum(axis=1, keepdims=True)
        po = _dot(p.astype(BF16), v)
        l = ps if l is None else l + ps
        o = po if o is None else o + po
    if extra_logit is not None:
        l = l + jnp.exp(extra_logit - m)
    return o / l


def _sink_column(sink_ref, g, t):
    row = lax.broadcasted_iota(jnp.int32, (GROUP * t, 1), 0) // t
    col = jnp.full((GROUP * t, 1), sink_ref[GROUP * g], F32)
    for j in range(1, GROUP):
        col = jnp.where(row == j, sink_ref[GROUP * g + j], col)
    return col


ONES_ROWS = 16


def _attn_a_kernel(q_ref, k_ref, v_ref, kc_ref, vc_ref, o_ref, vt_ref, vct_ref, qs_ref, m_ref, acc_ref,
                   s0_ref, s1_ref, s2_ref, s3_ref, mx0_ref, mx1_ref, mx2_ref, mx3_ref,
                   *, tq, tk, col_blk):
    n_chunks = SEQ // tk

    @pl.when(pl.program_id(1) == 0)
    def _():
        for j in range(SEQ // tk):
            vt_ref[j] = _transpose_bf16(v_ref[0, j * tk:(j + 1) * tk, :])
        vct_ref[...] = _transpose_bf16(vc_ref[0])

    col_blocks = [slice(c0, c0 + col_blk) for c0 in range(0, GROUP * tq, col_blk)]
    bufs = ((s0_ref, mx0_ref), (s1_ref, mx1_ref), (s2_ref, mx2_ref), (s3_ref, mx3_ref))

    def score_block(k, buf, cols):
        s_ref, mx_ref = bufs[buf]
        s = _dot(k, qs_ref[:, cols])
        s_ref[cols.start // col_blk, 0:k.shape[0], :] = s
        mx_ref[:, cols] = s.max(axis=0, keepdims=True)

    def absorb_block(n, buf, v_ext, cols):
        s_ref, mx_ref = bufs[buf]
        m_prev = m_ref[:, cols]
        m_new = jnp.maximum(m_prev, mx_ref[:, cols])
        alpha = jnp.exp2(m_prev - m_new)
        p = jnp.exp2(s_ref[cols.start // col_blk, 0:n, :] - m_new).astype(BF16)
        acc_ref[:, cols] = alpha * acc_ref[:, cols] + _dot(v_ext, p)
        m_ref[:, cols] = m_new

    def latent_keys(j):
        return k_ref[0, pl.ds(pl.multiple_of(j * tk, tk), tk), :]

    def trip(n_cur, cur, vt_g, k_next, nxt):
        v_ext = _with_ones_rows(vt_g)
        for cols in col_blocks:
            if k_next is not None:
                score_block(k_next, nxt, cols)
            absorb_block(n_cur, cur, v_ext, cols)

    qt = _transpose_bf16(q_ref[0])
    for g in range(N_KV):
        rows = slice(g * HEAD_DIM, (g + 1) * HEAD_DIM)
        qs_ref[...] = jnp.zeros(qs_ref.shape, BF16)
        for j in range(GROUP):
            h = GROUP * g + j
            qs_ref[rows, j * tq:(j + 1) * tq] = qt[h * HEAD_DIM:(h + 1) * HEAD_DIM, :]
        m_ref[...] = jnp.full(m_ref.shape, NEG, F32)
        acc_ref[...] = jnp.zeros(acc_ref.shape, F32)

        n_bufs = len(bufs)
        ahead = n_bufs - 2
        kc = kc_ref[0]

        def item_keys(t):
            return kc if t == 0 else latent_keys(t - 1)

        for t in range(ahead):
            for cols in col_blocks:
                score_block(item_keys(t), t % n_bufs, cols)
        trip(CTX_LEN, 0, vct_ref[rows, :], item_keys(ahead), ahead % n_bufs)

        def body(i, carry, rows=rows):
            for u in range(n_bufs):
                j = n_bufs * i + u
                trip(tk, (u + 1) % n_bufs, vt_ref[j, rows, :],
                     latent_keys(j + ahead), (u + 1 + ahead) % n_bufs)
            return carry

        n_loop = (n_chunks - ahead) // n_bufs
        lax.fori_loop(0, n_loop, body, 0)
        for t in range(n_bufs * n_loop + 1, n_chunks + 1):
            more = t + ahead <= n_chunks
            trip(tk, t % n_bufs, vt_ref[t - 1, rows, :],
                 latent_keys(t - 1 + ahead) if more else None, (t + ahead) % n_bufs if more else None)
        o = acc_ref[0:HEAD_DIM, :] / acc_ref[HEAD_DIM:HEAD_DIM + 1, :]
        for cl in range(GROUP // 2):
            pair = jnp.concatenate([o[:, (2 * cl) * tq:(2 * cl + 1) * tq],
                                    o[:, (2 * cl + 1) * tq:(2 * cl + 2) * tq]], axis=0)
            cc = (GROUP // 2) * g + cl
            o_ref[0, :, cc * LANES:(cc + 1) * LANES] = _transpose_bf16(pair)


def _attn_a(q, k, v, kc, vc, tq=GLOBAL_TQ, tk=GLOBAL_TK, col_blk=MXU_N):
    b = q.shape[0]
    res = lambda n: pl.BlockSpec((1, n, KV_W), lambda bi, i: (bi, 0, 0))
    return pl.pallas_call(
        functools.partial(_attn_a_kernel, tq=tq, tk=tk, col_blk=col_blk),
        grid=(b, SEQ // tq),
        in_specs=[pl.BlockSpec((1, tq, Q_W), lambda bi, i: (bi, i, 0)),
                  res(SEQ), res(SEQ), res(CTX_LEN), res(CTX_LEN)],
        out_specs=pl.BlockSpec((1, tq, Q_W), lambda bi, i: (bi, i, 0)),
        out_shape=jax.ShapeDtypeStruct((b, SEQ, Q_W), BF16),
        scratch_shapes=[pltpu.VMEM((SEQ // tk, KV_W, tk), BF16),
                        pltpu.VMEM((KV_W, CTX_LEN), BF16),
                        pltpu.VMEM((LANES, GROUP * tq), BF16),
                        pltpu.VMEM((1, GROUP * tq), F32),
                        pltpu.VMEM((HEAD_DIM + ONES_ROWS, GROUP * tq), F32),
                        pltpu.VMEM((GROUP * tq // col_blk, tk, col_blk), F32),
                        pltpu.VMEM((GROUP * tq // col_blk, tk, col_blk), F32),
                        pltpu.VMEM((GROUP * tq // col_blk, tk, col_blk), F32),
                        pltpu.VMEM((GROUP * tq // col_blk, tk, col_blk), F32),
                        pltpu.VMEM((1, GROUP * tq), F32),
                        pltpu.VMEM((1, GROUP * tq), F32),
                        pltpu.VMEM((1, GROUP * tq), F32),
                        pltpu.VMEM((1, GROUP * tq), F32)],
        compiler_params=_cparams(("parallel", "arbitrary")),
        name="attn_global",
    )(q, k, v, kc, vc)


def _fill_feature_major(v_ref, vt_ref, blk):
    def fill(j, carry):
        off = pl.multiple_of(j * blk, blk)
        vt_ref[j] = _transpose_bf16(v_ref[0, pl.ds(off, blk), :])
        return carry

    lax.fori_loop(0, v_ref.shape[1] // blk, fill, 0)


def _with_ones_rows(vt):
    return jnp.concatenate([vt, jnp.ones((ONES_ROWS, vt.shape[1]), BF16)], axis=0)


def _attn_c_kernel(sink_ref, q_ref, k_ref, v_ref, kc_ref, vc_ref, o_ref, vt_ref, vct_ref,
                   qs0_ref, qs1_ref, s0_ref, s1_ref, mx0_ref, mx1_ref, band_ref, *, tq):
    win = tq + 2 * WINDOW
    n_keys = win + CTX_LEN
    i = pl.program_id(1)

    @pl.when(i == 0)
    def _():
        _fill_feature_major(v_ref, vt_ref, LANES)
        vct_ref[...] = _transpose_bf16(vc_ref[0])

    ws = pl.multiple_of(jnp.clip(i * tq - WINDOW, 0, SEQ - win), LANES)
    blk0 = ws // LANES
    kw = k_ref[0, pl.ds(ws, win), :]
    kc = kc_ref[0]
    kpos = ws + lax.broadcasted_iota(jnp.int32, (win, tq), 0)
    qpos = i * tq + lax.broadcasted_iota(jnp.int32, (win, tq), 1)
    band_ref[...] = jnp.where(jnp.abs(qpos - kpos) <= WINDOW, 0.0, NEG)
    qt = _transpose_bf16(q_ref[0])
    bufs = ((qs0_ref, s0_ref, mx0_ref), (qs1_ref, s1_ref, mx1_ref))
    col_blocks = [slice(j * tq, (j + 1) * tq) for j in range(GROUP)]

    def stack_queries(g):
        qs_ref = bufs[g][0]
        qs_ref[...] = jnp.zeros(qs_ref.shape, BF16)
        for j in range(GROUP):
            h = GROUP * g + j
            qs_ref[g * HEAD_DIM:(g + 1) * HEAD_DIM, j * tq:(j + 1) * tq] = (
                qt[h * HEAD_DIM:(h + 1) * HEAD_DIM, :])

    def score_block(g, j):
        qs_ref, s_ref, mx_ref = bufs[g]
        cols = col_blocks[j]
        s_win = _dot(kw, qs_ref[:, cols]) + band_ref[...]
        s_ctx = _dot(kc, qs_ref[:, cols])
        s_ref[j, 0:win, :] = s_win
        s_ref[j, win:n_keys, :] = s_ctx
        sink2 = sink_ref[GROUP * g + j] * LOG2E
        mx_ref[:, cols] = jnp.maximum(
            jnp.maximum(s_win.max(axis=0, keepdims=True), s_ctx.max(axis=0, keepdims=True)), sink2)

    def values(g):
        rows = slice(g * HEAD_DIM, (g + 1) * HEAD_DIM)
        blocks = [vt_ref[blk0 + t, rows, :] for t in range(win // LANES)] + [vct_ref[rows, :]]
        return _with_ones_rows(jnp.concatenate(blocks, axis=1))

    def out_block(g, j, v_ext):
        _, s_ref, mx_ref = bufs[g]
        cols = col_blocks[j]
        m = mx_ref[:, cols]
        p = jnp.exp2(s_ref[j] - m).astype(BF16)
        acc = _dot(v_ext, p)
        l = acc[HEAD_DIM:HEAD_DIM + 1, :] + jnp.exp2(sink_ref[GROUP * g + j] * LOG2E - m)
        return acc[0:HEAD_DIM, :] / l

    def store_pair(g, cl, o_even, o_odd):
        c = (GROUP // 2) * g + cl
        o_ref[0, :, c * LANES:(c + 1) * LANES] = _transpose_bf16(
            jnp.concatenate([o_even, o_odd], axis=0))

    stack_queries(0)
    stack_queries(1)
    for j in range(GROUP):
        score_block(0, j)
    v_ext = values(0)
    outs = []
    for j in range(GROUP):
        score_block(1, j)
        outs.append(out_block(0, j, v_ext))
        if j % 2 == 1:
            store_pair(0, j // 2, outs[j - 1], outs[j])
    v_ext = values(1)
    outs = []
    for j in range(GROUP):
        outs.append(out_block(1, j, v_ext))
        if j % 2 == 1:
            store_pair(1, j // 2, outs[j - 1], outs[j])


def _attn_c(sink, q, k, v, kc, vc, tq=WINDOW_TQ):
    b = q.shape[0]
    n_keys = tq + 2 * WINDOW + CTX_LEN
    res = lambda n: pl.BlockSpec((1, n, KV_W), lambda bi, i: (bi, 0, 0))
    return pl.pallas_call(
        functools.partial(_attn_c_kernel, tq=tq),
        grid=(b, SEQ // tq),
        in_specs=[pl.BlockSpec(memory_space=pltpu.SMEM),
                  pl.BlockSpec((1, tq, Q_W), lambda bi, i: (bi, i, 0)),
                  res(SEQ), res(SEQ), res(CTX_LEN), res(CTX_LEN)],
        out_specs=pl.BlockSpec((1, tq, Q_W), lambda bi, i: (bi, i, 0)),
        out_shape=jax.ShapeDtypeStruct((b, SEQ, Q_W), BF16),
        scratch_shapes=[pltpu.VMEM((SEQ // LANES, KV_W, LANES), BF16),
                        pltpu.VMEM((KV_W, CTX_LEN), BF16),
                        pltpu.VMEM((LANES, GROUP * tq), BF16),
                        pltpu.VMEM((LANES, GROUP * tq), BF16),
                        pltpu.VMEM((GROUP, n_keys, tq), F32),
                        pltpu.VMEM((GROUP, n_keys, tq), F32),
                        pltpu.VMEM((1, GROUP * tq), F32),
                        pltpu.VMEM((1, GROUP * tq), F32),
                        pltpu.VMEM((tq + 2 * WINDOW, tq), F32)],
        compiler_params=_cparams(("parallel", "arbitrary")),
        name="attn_window",
    )(sink, q, k, v, kc, vc)


def _nb_bias_kernel(rpb_ref, o_ref, tt_ref):
    ck = lax.broadcasted_iota(jnp.int32, (GRID_W, LANES), 0)
    lane = lax.broadcasted_iota(jnp.int32, (GRID_W, LANES), 1)
    cq = lane & (GRID_W - 1)
    d = ck - cq + (NB_COLS - 1)
    cs = jnp.clip(cq - NB_COLS // 2, 0, GRID_W - NB_COLS)
    col_ok = (ck >= cs) & (ck < cs + NB_COLS)
    n_dc = 2 * NB_COLS - 1
    neg_tile = jnp.full((GRID_W, LANES), NEG, F32)
    even_row = lane < GRID_W
    for h in range(2):
        base = ((pl.program_id(0) * (N_HEADS // 2) + pl.program_id(1)) * 2 + h) * N_REL
        for a in range(2 * NB_ROWS - 1):
            t = neg_tile
            for dd in range(n_dc):
                t = jnp.where(d == dd, rpb_ref[base + a * n_dc + dd] * LOG2E, t)
            tt_ref[a] = jnp.where(col_ok, t, NEG)
        for case, pat in enumerate(NB_PATTERNS):
            for kr in range(NB_WIN_ROWS):
                for qv in range(NB_Q_ROWS // 2):
                    a0, a1 = pat[2 * qv][kr], pat[2 * qv + 1][kr]
                    lo = neg_tile if a0 is None else tt_ref[a0]
                    hi = neg_tile if a1 is None else tt_ref[a1]
                    o_ref[0, case, 0, h, kr * GRID_W:(kr + 1) * GRID_W, qv * LANES:(qv + 1) * LANES] = (
                        jnp.where(even_row, lo, hi))


def _nb_bias(rpb):
    n_case = len(NB_PATTERNS)
    return pl.pallas_call(
        _nb_bias_kernel,
        grid=(DEPTH, N_HEADS // 2),
        in_specs=[pl.BlockSpec(memory_space=pltpu.SMEM)],
        out_specs=pl.BlockSpec((1, n_case, 1, 2, NB_WIN, NB_TQ), lambda l, c: (l, 0, c, 0, 0, 0)),
        out_shape=jax.ShapeDtypeStruct((DEPTH, n_case, N_HEADS // 2, 2, NB_WIN, NB_TQ), F32),
        scratch_shapes=[pltpu.VMEM((2 * NB_ROWS - 1, GRID_W, LANES), F32)],
        compiler_params=_cparams(("arbitrary", "arbitrary")),
        name="nb_bias",
    )(rpb.reshape(-1))


def _pair_attention(qp, s_extra_fn, k_list, v_list):
    t = qp.shape[0]
    lane = lax.broadcasted_iota(jnp.int32, (t, LANES), 1)
    qf = qp.astype(F32)
    outs = []
    for h in range(2):
        qpad = jnp.where((lane // HEAD_DIM) == h, qf, 0.0).astype(BF16)
        s_list = [_dot_nt(qpad, k) for k in k_list]
        s_list = s_extra_fn(h, s_list)
        outs.append(_softmax_pv(s_list, v_list))
    return jnp.where(lane < HEAD_DIM, outs[0], outs[1])


def _attn_b_kernel(q_ref, k_ref, v_ref, kc_ref, vc_ref, bias_ref, o_ref, vt_ref, vct_ref,
                   s_all_ref, mx_all_ref):
    i = pl.program_id(1)

    @pl.when(i == 0)
    def _():
        _fill_feature_major(v_ref, vt_ref, NB_KEY_BLK)
        vct_ref[...] = _transpose_bf16(vc_ref[0])

    w0 = jnp.clip(NB_Q_ROWS * i - NB_ROWS // 2, 0, GRID_H - NB_WIN_ROWS)
    off = pl.multiple_of(w0 * GRID_W, NB_KEY_BLK)
    blk0 = w0 // 2
    n_bufs = s_all_ref.shape[0]
    ahead = n_bufs - 1
    zeros = jnp.zeros((NB_TQ, LANES), BF16)
    head0_lanes = lax.broadcasted_iota(jnp.int32, (NB_TQ, LANES), 1) < HEAD_DIM

    def score_stage(h):
        buf = h % n_bufs
        c, t = divmod(h, 2)
        cols = slice(c * LANES, (c + 1) * LANES)
        qp = q_ref[0, :, cols]
        qpad = jnp.where(head0_lanes, qp, zeros) if t == 0 else jnp.where(head0_lanes, zeros, qp)
        s_win = _dot_nt(k_ref[0, c, pl.ds(off, NB_WIN), :], qpad) + bias_ref[0, 0, c, t]
        s_ctx = _dot_nt(kc_ref[0, :, cols], qpad)
        s_all_ref[buf, 0:NB_WIN, :] = s_win
        s_all_ref[buf, NB_WIN:NB_WIN + CTX_LEN, :] = s_ctx
        mx_all_ref[buf] = jnp.maximum(s_win.max(axis=0, keepdims=True),
                                      s_ctx.max(axis=0, keepdims=True))

    def pv_stage(h):
        buf = h % n_bufs
        rows = slice(h * HEAD_DIM, (h + 1) * HEAD_DIM)
        p = jnp.exp2(s_all_ref[buf] - mx_all_ref[buf]).astype(BF16)
        blocks = [vt_ref[blk0 + t, rows, :] for t in range(NB_WIN // NB_KEY_BLK)] + [vct_ref[rows, :]]
        acc = _dot(_with_ones_rows(jnp.concatenate(blocks, axis=1)), p)
        return acc[0:HEAD_DIM, :] / acc[HEAD_DIM:HEAD_DIM + 1, :]

    for h in range(ahead):
        score_stage(h)
    outs = []
    for h in range(N_HEADS):
        if h + ahead < N_HEADS:
            score_stage(h + ahead)
        outs.append(pv_stage(h))
        if h % 2 == 1:
            o_ref[0, :, (h // 2) * LANES:(h // 2 + 1) * LANES] = _transpose_bf16(
                jnp.concatenate([outs[h - 1], outs[h]], axis=0))


def _attn_b(q, k, v, kc, vc, bias, layer):
    b = q.shape[0]
    n_tiles = SEQ // NB_TQ
    res = lambda n: pl.BlockSpec((1, n, Q_W), lambda bi, i: (bi, 0, 0), pipeline_mode=pl.Buffered(1))

    n_case = len(NB_PATTERNS)
    interior = max(set(NB_TILE_CASE), key=NB_TILE_CASE.count)
    n_lo, n_hi = interior, n_case - 1 - interior
    case_of = lambda t: min(t, n_lo) + max(t - (n_tiles - 1 - n_hi), 0)
    assert [case_of(t) for t in range(n_tiles)] == NB_TILE_CASE

    def bias_idx(bi, i):
        return (layer, jnp.minimum(i, n_lo) + jnp.maximum(i - (n_tiles - 1 - n_hi), 0), 0, 0, 0, 0)

    return pl.pallas_call(
        _attn_b_kernel,
        grid=(b, n_tiles),
        in_specs=[pl.BlockSpec((1, NB_TQ, Q_W), lambda bi, i: (bi, i, 0)),
                  pl.BlockSpec((1, N_HEADS // 2, SEQ, LANES), lambda bi, i: (bi, 0, 0, 0),
                               pipeline_mode=pl.Buffered(1)),
                  res(SEQ), res(CTX_LEN), res(CTX_LEN),
                  pl.BlockSpec((1, 1, N_HEADS // 2, 2, NB_WIN, NB_TQ), bias_idx)],
        out_specs=pl.BlockSpec((1, NB_TQ, Q_W), lambda bi, i: (bi, i, 0)),
        out_shape=jax.ShapeDtypeStruct((b, SEQ, Q_W), BF16),
        scratch_shapes=[pltpu.VMEM((SEQ // NB_KEY_BLK, Q_W, NB_KEY_BLK), BF16),
                        pltpu.VMEM((Q_W, CTX_LEN), BF16),
                        pltpu.VMEM((NB_BUFS, NB_WIN + CTX_LEN, NB_TQ), F32),
                        pltpu.VMEM((NB_BUFS, 1, NB_TQ), F32)],
        compiler_params=_cparams(("parallel", "arbitrary")),
        name="attn_neighbourhood",
    )(q, k, v, kc, vc, bias)


def _attn_ctx_kernel(sink_ref, qa_ref, ka_ref, va_ref, qb_ref, kb_ref, vb_ref, qc_ref, kc_ref, vc_ref,
                     oa_ref, ob_ref, oc_ref):
    t = CTX_LEN
    for g in range(N_KV):
        for q_ref, k_ref, v_ref, o_ref, sink in ((qa_ref, ka_ref, va_ref, oa_ref, False),
                                                 (qc_ref, kc_ref, vc_ref, oc_ref, True)):
            qs = _stack_group_q(q_ref[0], g)
            extra = _sink_column(sink_ref, g, t) if sink else None
            o = _softmax_pv([_dot_nt(qs, k_ref[0])], [v_ref[0]], extra)
            chunks = _unstack_group_out(o, g, t)
            for cl in range(2):
                c = 2 * g + cl
                o_ref[0, :, c * LANES:(c + 1) * LANES] = chunks[cl].astype(BF16)
    for c in range(N_HEADS // 2):
        cols = slice(c * LANES, (c + 1) * LANES)
        o = _pair_attention(qb_ref[0, :, cols], lambda h, s_list: s_list,
                            [kb_ref[0, :, cols]], [vb_ref[0, :, cols]])
        ob_ref[0, :, cols] = o.astype(BF16)


def _attn_ctx(sink, qa, ka, va, qb, kb, vb, qc, kc, vc):
    b = qa.shape[0]
    spec = lambda w: pl.BlockSpec((1, CTX_LEN, w), lambda bi: (bi, 0, 0))
    return pl.pallas_call(
        _attn_ctx_kernel,
        grid=(b,),
        in_specs=[pl.BlockSpec(memory_space=pltpu.SMEM),
                  spec(Q_W), spec(KV_W), spec(KV_W), spec(Q_W), spec(Q_W), spec(Q_W),
                  spec(Q_W), spec(KV_W), spec(KV_W)],
        out_specs=[spec(Q_W)] * 3,
        out_shape=[jax.ShapeDtypeStruct((b, CTX_LEN, Q_W), BF16)] * 3,
        compiler_params=_cparams(("parallel",)),
        name="attn_ctx",
    )(sink, qa, ka, va, qb, kb, vb, qc, kc, vc)


def _out_proj_kernel(*refs, final):
    if final:
        ya_ref, yb_ref, yc_ref, sg_ref, x_ref, mod_ref, w_ref, fw_ref, o_ref = refs
    else:
        ya_ref, yb_ref, yc_ref, sg_ref, x_ref, mod_ref, w_ref, o_ref = refs
    acc = None
    for idx, y_ref in enumerate((ya_ref, yb_ref, yc_ref)):
        cols = slice(idx * Q_W, (idx + 1) * Q_W)
        u = (y_ref[0].astype(F32) * sg_ref[0, :, cols].astype(F32)).astype(BF16)
        part = _dot(u, w_ref[cols, :])
        acc = part if acc is None else acc + part
    xn = x_ref[0] + mod_ref[0, 2:3, :] * acc
    if final:
        xn = (xn * lax.rsqrt(jnp.mean(xn * xn, axis=-1, keepdims=True) + EPS)) * fw_ref[...]
    o_ref[0] = xn


def _out_proj(ya, yb, yc, sg, x, mod, w_bf, final_w, tm):
    b, n, _ = x.shape
    final = final_w is not None
    mod_idx = (lambda bi, i: (bi, 0, 0)) if mod.shape[0] > 1 else (lambda bi, i: (0, 0, 0))
    row = lambda w: pl.BlockSpec((1, tm, w), lambda bi, i: (bi, i, 0))
    in_specs = [row(Q_W), row(Q_W), row(Q_W), row(MIX_WIDTH), row(D_MODEL),
                pl.BlockSpec((1, 3, D_MODEL), mod_idx),
                pl.BlockSpec((MIX_WIDTH, D_MODEL), lambda bi, i: (0, 0))]
    args = [ya, yb, yc, sg, x, mod, w_bf]
    if final:
        in_specs.append(pl.BlockSpec((1, D_MODEL), lambda bi, i: (0, 0)))
        args.append(final_w)
    return pl.pallas_call(
        functools.partial(_out_proj_kernel, final=final),
        grid=(b, n // tm),
        in_specs=in_specs,
        out_specs=row(D_MODEL),
        out_shape=jax.ShapeDtypeStruct((b, n, D_MODEL), F32),
        compiler_params=_cparams(("parallel", "parallel")),
        name="out_proj_final" if final else "out_proj",
    )(*args)


def _rope_tables():
    t = jnp.arange(SEQ, dtype=jnp.int32)
    rows = (t // GRID_W).astype(F32)
    cols = (t % GRID_W).astype(F32)
    n_freq = HEAD_DIM // 4
    freq = ROPE_THETA ** (-jnp.arange(n_freq, dtype=F32) / n_freq)
    ang = jnp.concatenate([rows[:, None] * freq, cols[:, None] * freq], axis=-1)
    cos, sin = jnp.cos(ang), jnp.sin(ang)
    cos_t = jnp.tile(jnp.concatenate([cos, cos], axis=-1), (1, LANES // HEAD_DIM))
    sin_t = jnp.tile(jnp.concatenate([-sin, sin], axis=-1), (1, LANES // HEAD_DIM))
    return cos_t, sin_t


def kernel(x, c, ctx, c_ctx, norm_w, ada_w, ada_b, w_in, w_out, q_norm_a, k_norm_a, rpb_b, sink_c,
           final_norm_w):
    bsz = x.shape[0]
    cvecs = jnp.concatenate([c, c_ctx[None], jnp.zeros((8 - bsz - 1, D_MODEL), F32)], axis=0)
    mod = _modulation(cvecs, ada_w, ada_b).reshape(DEPTH, 8, 3, D_MODEL)
    bias = _nb_bias(rpb_b)
    rope_tabs = _rope_tables()
    cx = ctx
    for l in range(DEPTH):
        need_ctx = l < DEPTH - 1
        w_in_l = w_in[l].astype(BF16)
        w_out_l = w_out[l].astype(BF16)
        mod_x, mod_c = mod[l, :bsz], mod[l, bsz:bsz + 1]
        nw = norm_w[l][None]
        qn_t = jnp.tile(q_norm_a[l], N_HEADS)[None]
        kn_t = jnp.tile(k_norm_a[l], N_KV)[None]
        qa, ka, va, qb, kb, vb, qc, kc, vc, sg = _in_proj(
            x, mod_x, nw, w_in_l, qn_t, kn_t, rope_tabs, tm=ROW_TILE)
        qa_c, ka_c, va_c, qb_c, kb_c, vb_c, qc_c, kc_c, vc_c, sg_c = _in_proj(
            cx, mod_c, nw, w_in_l, qn_t, kn_t, None, tm=CTX_LEN)
        ya = _attn_a(qa, ka, va, ka_c, va_c)
        yb = _attn_b(qb, kb, vb, kb_c, vb_c, bias, l)
        yc = _attn_c(sink_c[l], qc, kc, vc, kc_c, vc_c)
        if need_ctx:
            ya_c, yb_c, yc_c = _attn_ctx(sink_c[l], qa_c, ka_c, va_c, qb_c, kb_c, vb_c, qc_c, kc_c, vc_c)
            cx = _out_proj(ya_c, yb_c, yc_c, sg_c, cx, mod_c, w_out_l, None, tm=CTX_LEN)
        x = _out_proj(ya, yb, yc, sg, x, mod_x, w_out_l,
                      None if need_ctx else final_norm_w[None], tm=ROW_TILE)
    return x
```
